```python
import math
import jax
import jax.numpy as jnp
from jax import lax
import numpy as np

D_MODEL = 4096
BATCH = 4
SEQ = 2048
DEPTH = 2
DEC_BATCH = 8
DEC_SEQ = 8
PAST_LEN = 16384
PAGE_SIZE = 128

N_MIXERS = 2
N_CONV_LAYERS = (DEPTH + 1) // 2
N_NSA_LAYERS = DEPTH // 2
CONV_WIDTH = 31
CONV_BUF = CONV_WIDTH - 1
N_HEADS = 32
HEAD_DIM = D_MODEL // N_HEADS
KV_GROUPS = 4
HEADS_PER_GROUP = N_HEADS // KV_GROUPS
KV_WIDTH = KV_GROUPS * HEAD_DIM
BLOCK = 64
N_SELECT = 16
WINDOW = 512
CMP_HIDDEN = 512
WIN_Q_BLOCK = 128
SEL_Q_CHUNK = 16
N_BRANCHES = 3
NSA_IN_COLS = N_HEADS * HEAD_DIM + N_BRANCHES * 2 * KV_WIDTH + N_BRANCHES * N_HEADS
REL_BUCKETS = 32
REL_MAX_DIST = 128
D_FF = 4 * D_MODEL
EPS = 1e-6
NEG_INF = -1e30

kernel_name = 'hybrid_conformer_nsa_decoder_step'


def rms_norm(x, g):
    x32 = x.astype(jnp.float32)
    y = x32 * lax.rsqrt(jnp.mean(x32 * x32, axis=-1, keepdims=True) + EPS) * g
    return y.astype(x.dtype)


def layer_norm(x, g, b):
    x32 = x.astype(jnp.float32)
    mu = jnp.mean(x32, axis=-1, keepdims=True)
    var = jnp.mean(jnp.square(x32 - mu), axis=-1, keepdims=True)
    return ((x32 - mu) * lax.rsqrt(var + EPS) * g + b).astype(x.dtype)


def squared_relu_mlp(h, w_up, w_down):
    return jnp.square(jax.nn.relu(h @ w_up)) @ w_down


def conv_module(h, buf, w_in, b_in, dw_w, dw_b, ln_g, ln_b, w_out, b_out):
    u = h @ w_in + b_in
    a, gate = jnp.split(u, 2, axis=-1)
    glu = a * jax.nn.sigmoid(gate)
    ext = jnp.concatenate([buf.astype(glu.dtype), glu], axis=1)
    conv = lax.conv_general_dilated(
        ext, dw_w[:, None, :].astype(ext.dtype), window_strides=(1,), padding='VALID',
        dimension_numbers=('NWC', 'WIO', 'NWC'), feature_group_count=D_MODEL) + dw_b
    z = jax.nn.silu(layer_norm(conv, ln_g, ln_b))
    return z @ w_out + b_out, ext[:, -CONV_BUF:]


def rel_bucket(dist):
    n = jnp.maximum(dist, 0)
    max_exact = REL_BUCKETS // 2
    nf = jnp.maximum(n, 1).astype(jnp.float32)
    large = max_exact + (jnp.log(nf / max_exact) / math.log(REL_MAX_DIST / max_exact)
                         * (REL_BUCKETS - max_exact)).astype(jnp.int32)
    large = jnp.minimum(large, REL_BUCKETS - 1)
    return jnp.where(n < max_exact, n, large)


def head_bias(rel_table, dist):
    b = rel_table[rel_bucket(dist)].astype(jnp.float32)
    return b.reshape(dist.shape + (KV_GROUPS, HEADS_PER_GROUP)).transpose(2, 3, 0, 1)


def nsa_project(h, w_in):
    B, S = h.shape[:2]
    qd = N_HEADS * HEAD_DIM
    kvd = N_BRANCHES * 2 * KV_WIDTH
    proj = h @ w_in
    q = proj[..., :qd].reshape(B, S, KV_GROUPS, HEADS_PER_GROUP, HEAD_DIM)
    kv = proj[..., qd:qd + kvd].reshape(B, S, N_BRANCHES, 2, KV_GROUPS, HEAD_DIM)
    gates = jax.nn.sigmoid(proj[..., qd + kvd:].astype(jnp.float32)).reshape(
        B, S, N_BRANCHES, KV_GROUPS, HEADS_PER_GROUP)
    return q, kv[:, :, 0], kv[:, :, 1], kv[:, :, 2], gates


def compress_blocks(rows, pos_emb, w1, b1, w2, b2):
    B, L = rows.shape[:2]
    nb = L // BLOCK
    blk = rows[:, :nb * BLOCK].reshape(B, nb, BLOCK, KV_GROUPS, HEAD_DIM) + pos_emb[:, None, :]
    blk = blk.transpose(0, 1, 3, 2, 4).reshape(B, nb, KV_GROUPS, BLOCK * HEAD_DIM)
    return jax.nn.silu(blk @ w1 + b1) @ w2 + b2


def compressed_branch(q, kc, vc, q_pos, rel_table):
    nb = kc.shape[1]
    blk_end = jnp.arange(nb) * BLOCK + (BLOCK - 1)
    dist = q_pos[:, None] - blk_end[None, :]
    valid = dist >= 0
    logits = jnp.einsum('bqgrd,bngd->bgrqn', q, kc).astype(jnp.float32) * HEAD_DIM ** -0.5
    logits = jnp.where(valid, logits + head_bias(rel_table, dist), NEG_INF)
    p = jnp.where(valid, jax.nn.softmax(logits, axis=-1), 0.0)
    out = jnp.einsum('bgrqn,bngd->bqgrd', p.astype(vc.dtype), vc)
    return out, p


def select_blocks(p_cmp, q_pos, n_blocks):
    imp = p_cmp.sum(axis=2)
    imp = jnp.pad(imp, ((0, 0), (0, 0), (0, 0), (0, n_blocks - imp.shape[-1])))
    j = jnp.arange(n_blocks)
    cur = (q_pos // BLOCK)[:, None]
    forced = (j == 0) | (j == cur) | (j == cur - 1)
    imp = jnp.where(j > cur, -jnp.inf, jnp.where(forced, jnp.inf, imp))
    _, idx = lax.top_k(imp, min(N_SELECT, n_blocks))
    return idx


def selected_attend(q, kvg, idx, q_pos, rel_table):
    kg = kvg[..., 0, :]
    vg = kvg[..., 1, :]
    k_pos = idx[..., None] * BLOCK + jnp.arange(BLOCK)
    dist = q_pos[None, None, :, None, None] - k_pos
    valid = dist >= 0
    table_g = rel_table.reshape(REL_BUCKETS, KV_GROUPS, HEADS_PER_GROUP).transpose(1, 0, 2)
    bias = table_g[jnp.arange(KV_GROUPS)[None, :, None, None, None], rel_bucket(dist)]
    bias = jnp.moveaxis(bias.astype(jnp.float32), -1, 3)
    logits = jnp.einsum('bqgrd,bgqnkd->bgqrnk', q, kg).astype(jnp.float32) * HEAD_DIM ** -0.5
    logits = jnp.where(valid[:, :, :, None], logits + bias, NEG_INF)
    shp = logits.shape
    p = jax.nn.softmax(logits.reshape(shp[:4] + (-1,)), axis=-1).reshape(shp)
    return jnp.einsum('bgqrnk,bgqnkd->bqgrd', p.astype(vg.dtype), vg)


def gather_contiguous(kv_rows, idx):
    B, T = kv_rows.shape[:2]
    blocks = kv_rows.reshape(B, T // BLOCK, BLOCK, 2, KV_GROUPS, HEAD_DIM)
    bi = jnp.arange(B)[:, None, None, None]
    gi = jnp.arange(KV_GROUPS)[None, :, None, None]
    return blocks[bi, idx, :, :, gi, :]


def gather_paged(pool, kv_new, page_table, idx):
    B, S = kv_new.shape[:2]
    bpp = PAGE_SIZE // BLOCK
    past_blocks = page_table.shape[1] * bpp
    bi = jnp.arange(B)[:, None, None, None]
    gi = jnp.arange(KV_GROUPS)[None, :, None, None]
    pool6 = pool.reshape(pool.shape[0], bpp, BLOCK, 2, KV_GROUPS, HEAD_DIM)
    safe = jnp.minimum(idx, past_blocks - 1)
    phys = page_table[bi, safe // bpp]
    past = pool6[phys, safe % bpp, :, :, gi, :]
    nnb = -(-S // BLOCK)
    newp = jnp.pad(kv_new, ((0, 0), (0, nnb * BLOCK - S), (0, 0), (0, 0), (0, 0)))
    newp = newp.reshape(B, nnb, BLOCK, 2, KV_GROUPS, HEAD_DIM)
    new = newp[bi, jnp.clip(idx - past_blocks, 0, nnb - 1), :, :, gi, :]
    return jnp.where((idx < past_blocks)[..., None, None, None], past, new)


def window_attend(q, k, v, q_pos, k_pos, rel_table):
    dist = q_pos[:, None] - k_pos[None, :]
    mask = (dist >= 0) & (dist <= WINDOW) & (k_pos[None, :] >= 0)
    logits = jnp.einsum('bqgrd,bkgd->bgrqk', q, k).astype(jnp.float32) * HEAD_DIM ** -0.5
    logits = jnp.where(mask, logits + head_bias(rel_table, dist), NEG_INF)
    p = jax.nn.softmax(logits, axis=-1)
    return jnp.einsum('bgrqk,bkgd->bqgrd', p.astype(v.dtype), v)


def window_branch_prompt(q, k, v, rel_table):
    B, T = q.shape[:2]
    pad = ((0, 0), (WINDOW, 0), (0, 0), (0, 0))
    kp = jnp.pad(k, pad)
    vp = jnp.pad(v, pad)
    n_qb = T // WIN_Q_BLOCK
    qb = q.reshape(B, n_qb, WIN_Q_BLOCK, KV_GROUPS, HEADS_PER_GROUP, HEAD_DIM).swapaxes(0, 1)

    def one(args):
        i, qi = args
        start = i * WIN_Q_BLOCK
        ki = lax.dynamic_slice_in_dim(kp, start, WIN_Q_BLOCK + WINDOW, axis=1)
        vi = lax.dynamic_slice_in_dim(vp, start, WIN_Q_BLOCK + WINDOW, axis=1)
        q_pos = start + jnp.arange(WIN_Q_BLOCK)
        k_pos = start - WINDOW + jnp.arange(WIN_Q_BLOCK + WINDOW)
        return window_attend(qi, ki, vi, q_pos, k_pos, rel_table)

    out = lax.map(one, (jnp.arange(n_qb), qb))
    return out.swapaxes(0, 1).reshape(B, T, KV_GROUPS, HEADS_PER_GROUP, HEAD_DIM)


def selected_branch_prompt(q, kv_sel, idx, rel_table):
    B, T = q.shape[:2]
    n = idx.shape[-1]
    nc = T // SEL_Q_CHUNK
    qc = q.reshape(B, nc, SEL_Q_CHUNK, KV_GROUPS, HEADS_PER_GROUP, HEAD_DIM).swapaxes(0, 1)
    ic = idx.reshape(B, KV_GROUPS, nc, SEL_Q_CHUNK, n).transpose(2, 0, 1, 3, 4)

    def one(args):
        c, qi, ii = args
        q_pos = c * SEL_Q_CHUNK + jnp.arange(SEL_Q_CHUNK)
        return selected_attend(qi, gather_contiguous(kv_sel, ii), ii, q_pos, rel_table)

    out = lax.map(one, (jnp.arange(nc), qc, ic))
    return out.swapaxes(0, 1).reshape(B, T, KV_GROUPS, HEADS_PER_GROUP, HEAD_DIM)


def nsa_combine(o_cmp, o_sel, o_win, gates, w_out):
    B, S = o_cmp.shape[:2]
    g = gates.astype(o_cmp.dtype)[..., None]
    o = g[:, :, 0] * o_cmp + g[:, :, 1] * o_sel + g[:, :, 2] * o_win
    return o.reshape(B, S, N_HEADS * HEAD_DIM) @ w_out


def nsa_prompt(h, w_in, cmp_pos, cmp_w1, cmp_b1, cmp_w2, cmp_b2, w_out, rel_table):
    B, T = h.shape[:2]
    q, kv_cmp, kv_sel, kv_win, gates = nsa_project(h, w_in)
    q_pos = jnp.arange(T)
    kc = compress_blocks(kv_cmp[:, :, 0], cmp_pos[0], cmp_w1[0], cmp_b1[0], cmp_w2[0], cmp_b2[0])
    vc = compress_blocks(kv_cmp[:, :, 1], cmp_pos[1], cmp_w1[1], cmp_b1[1], cmp_w2[1], cmp_b2[1])
    o_cmp, p_cmp = compressed_branch(q, kc, vc, q_pos, rel_table)
    idx = select_blocks(p_cmp, q_pos, -(-T // BLOCK))
    o_sel = selected_branch_prompt(q, kv_sel, idx, rel_table)
    o_win = window_branch_prompt(q, kv_win[:, :, 0], kv_win[:, :, 1], rel_table)
    out = nsa_combine(o_cmp, o_sel, o_win, gates, w_out)
    return out, kv_cmp, kv_sel, kv_win[:, T - min(WINDOW, T):]


def nsa_sample(h, pool_cmp, pool_sel, win_buf, page_table, w_in, cmp_pos, cmp_w1, cmp_b1,
               cmp_w2, cmp_b2, w_out, rel_table):
    B, S = h.shape[:2]
    past_len = page_table.shape[1] * PAGE_SIZE
    q, kv_cmp, kv_sel, kv_win, gates = nsa_project(h, w_in)
    q_pos = past_len + jnp.arange(S)
    past_cmp = pool_cmp[page_table].reshape(B, past_len, 2, KV_GROUPS, HEAD_DIM)
    all_cmp = jnp.concatenate([past_cmp, kv_cmp], axis=1)
    kc = compress_blocks(all_cmp[:, :, 0], cmp_pos[0], cmp_w1[0], cmp_b1[0], cmp_w2[0], cmp_b2[0])
    vc = compress_blocks(all_cmp[:, :, 1], cmp_pos[1], cmp_w1[1], cmp_b1[1], cmp_w2[1], cmp_b2[1])
    o_cmp, p_cmp = compressed_branch(q, kc, vc, q_pos, rel_table)
    idx = select_blocks(p_cmp, q_pos, -(-(past_len + S) // BLOCK))
    o_sel = selected_attend(q, gather_paged(pool_sel, kv_sel, page_table, idx), idx, q_pos, rel_table)
    lw = win_buf.shape[1]
    all_win = jnp.concatenate([win_buf.astype(kv_win.dtype), kv_win], axis=1)
    k_pos = past_len - lw + jnp.arange(lw + S)
    o_win = window_attend(q, all_win[:, :, 0], all_win[:, :, 1], q_pos, k_pos, rel_table)
    out = nsa_combine(o_cmp, o_sel, o_win, gates, w_out)
    return out, kv_cmp, kv_sel, all_win[:, lw + S - min(WINDOW, lw + S):]


def setup_inputs(seed: int = 0) -> dict:
    key = jax.random.key(seed)
    ks = jax.random.split(key, 32)
    f32 = jnp.float32

    def nrm(k, shape, scale):
        return jax.random.normal(k, shape, f32) * scale

    n_pages = PAST_LEN // PAGE_SIZE
    n_pool = (DEC_BATCH * n_pages * 5) // 4
    win_rows = min(WINDOW, PAST_LEN)
    perm = jax.random.permutation(ks[6], n_pool).astype(jnp.int32)
    page_table = perm[:DEC_BATCH * n_pages].reshape(DEC_BATCH, n_pages)
    d, cmp_in = D_MODEL, BLOCK * HEAD_DIM
    return {
        'x_prompt': nrm(ks[0], (BATCH, SEQ, d), 1.0),
        'x_sample': nrm(ks[1], (DEC_BATCH, DEC_SEQ, d), 1.0),
        'state_conv': nrm(ks[2], (N_CONV_LAYERS, DEC_BATCH, CONV_BUF, d), 1.0),
        'cache_kv_cmp': nrm(ks[3], (N_NSA_LAYERS, n_pool, PAGE_SIZE, 2, KV_GROUPS, HEAD_DIM), 1.0),
        'cache_kv_sel': nrm(ks[4], (N_NSA_LAYERS, n_pool, PAGE_SIZE, 2, KV_GROUPS, HEAD_DIM), 1.0),
        'state_kv_win': nrm(ks[5], (N_NSA_LAYERS, DEC_BATCH, win_rows, 2, KV_GROUPS, HEAD_DIM), 1.0),
        'page_table': page_table,
        'norm_mix_g': 1.0 + nrm(ks[7], (DEPTH, d), 0.02),
        'norm_ffn_g': 1.0 + nrm(ks[8], (DEPTH, d), 0.02),
        'norm_final_g': 1.0 + nrm(ks[9], (d,), 0.02),
        'rel_bias_table': nrm(ks[10], (REL_BUCKETS, N_HEADS), 0.5),
        'conv_w_in': nrm(ks[11], (N_CONV_LAYERS, d, 2 * d), d ** -0.5),
        'conv_b_in': nrm(ks[12], (N_CONV_LAYERS, 2 * d), 0.01),
        'conv_dw_w': nrm(ks[13], (N_CONV_LAYERS, CONV_WIDTH, d), CONV_WIDTH ** -0.5),
        'conv_dw_b': nrm(ks[14], (N_CONV_LAYERS, d), 0.01),
        'conv_ln_g': 1.0 + nrm(ks[15], (N_CONV_LAYERS, d), 0.02),
        'conv_ln_b': nrm(ks[16], (N_CONV_LAYERS, d), 0.01),
        'conv_w_out': nrm(ks[17], (N_CONV_LAYERS, d, d), d ** -0.5),
        'conv_b_out': nrm(ks[18], (N_CONV_LAYERS, d), 0.01),
        'nsa_w_in': nrm(ks[19], (N_NSA_LAYERS, d, NSA_IN_COLS), d ** -0.5),
        'nsa_cmp_pos': nrm(ks[20], (N_NSA_LAYERS, 2, BLOCK, HEAD_DIM), 0.1),
        'nsa_cmp_w1': nrm(ks[21], (N_NSA_LAYERS, 2, cmp_in, CMP_HIDDEN), cmp_in ** -0.5),
        'nsa_cmp_b1': nrm(ks[22], (N_NSA_LAYERS, 2, CMP_HIDDEN), 0.01),
        'nsa_cmp_w2': nrm(ks[23], (N_NSA_LAYERS, 2, CMP_HIDDEN, HEAD_DIM), CMP_HIDDEN ** -0.5),
        'nsa_cmp_b2': nrm(ks[24], (N_NSA_LAYERS, 2, HEAD_DIM), 0.01),
        'nsa_w_out': nrm(ks[25], (N_NSA_LAYERS, N_HEADS * HEAD_DIM, d), (N_HEADS * HEAD_DIM) ** -0.5),
        'ffn_w_up': nrm(ks[26], (DEPTH, d, D_FF), d ** -0.5),
        'ffn_w_down': nrm(ks[27], (DEPTH, D_FF, d), D_FF ** -0.5),
    }


def reference(x_prompt, x_sample, state_conv, cache_kv_cmp, cache_kv_sel, state_kv_win, page_table,
              norm_mix_g, norm_ffn_g, norm_final_g, rel_bias_table,
              conv_w_in, conv_b_in, conv_dw_w, conv_dw_b, conv_ln_g, conv_ln_b, conv_w_out, conv_b_out,
              nsa_w_in, nsa_cmp_pos, nsa_cmp_w1, nsa_cmp_b1, nsa_cmp_w2, nsa_cmp_b2, nsa_w_out,
              ffn_w_up, ffn_w_down):
    xp, xs = x_prompt, x_sample
    conv_p, conv_s = [], []
    cmp_p, cmp_s, sel_p, sel_s, win_p, win_s = [], [], [], [], [], []
    for i in range(DEPTH):
        j = i // N_MIXERS
        hp = rms_norm(xp, norm_mix_g[i])
        hs = rms_norm(xs, norm_mix_g[i])
        if i % N_MIXERS == 0:
            cp = (conv_w_in[j], conv_b_in[j], conv_dw_w[j], conv_dw_b[j], conv_ln_g[j], conv_ln_b[j],
                  conv_w_out[j], conv_b_out[j])
            zero_buf = jnp.zeros((xp.shape[0], CONV_BUF, D_MODEL), xp.dtype)
            op, bp = conv_module(hp, zero_buf, *cp)
            os_, bs = conv_module(hs, state_conv[j], *cp)
            conv_p.append(bp)
            conv_s.append(bs)
        else:
            npar = (nsa_cmp_pos[j], nsa_cmp_w1[j], nsa_cmp_b1[j], nsa_cmp_w2[j], nsa_cmp_b2[j], nsa_w_out[j])
            op, kcp, ksp, kwp = nsa_prompt(hp, nsa_w_in[j], *npar, rel_bias_table)
            os_, kcs, kss, kws = nsa_sample(hs, cache_kv_cmp[j], cache_kv_sel[j], state_kv_win[j], page_table,
                                            nsa_w_in[j], *npar, rel_bias_table)
            cmp_p.append(kcp)
            cmp_s.append(kcs)
            sel_p.append(ksp)
            sel_s.append(kss)
            win_p.append(kwp)
            win_s.append(kws)
        xp = xp + op
        xs = xs + os_
        xp = xp + squared_relu_mlp(rms_norm(xp, norm_ffn_g[i]), ffn_w_up[i], ffn_w_down[i])
        xs = xs + squared_relu_mlp(rms_norm(xs, norm_ffn_g[i]), ffn_w_up[i], ffn_w_down[i])
    y_prompt = rms_norm(xp, norm_final_g)
    y_sample = rms_norm(xs, norm_final_g)
    return (y_prompt, y_sample, jnp.stack(conv_p), jnp.stack(conv_s), jnp.stack(cmp_p), jnp.stack(cmp_s),
            jnp.stack(sel_p), jnp.stack(sel_s), jnp.stack(win_p), jnp.stack(win_s))
```

```python
import functools
import math

import jax
import jax.numpy as jnp
import numpy as np
from jax import lax
from jax.experimental import pallas as pl
from jax.experimental.pallas import tpu as pltpu

D_MODEL = 4096
CONV_WIDTH = 31
CONV_BUF = CONV_WIDTH - 1
N_HEADS = 32
HEAD_DIM = D_MODEL // N_HEADS
KV_GROUPS = 4
HEADS_PER_GROUP = N_HEADS // KV_GROUPS
KV_WIDTH = KV_GROUPS * HEAD_DIM
BLOCK = 64
N_SELECT = 16
WINDOW = 512
WIN_Q_BLOCK = 128
SEL_Q_CHUNK = 16
N_BRANCHES = 3
PAGE_SIZE = 128
REL_BUCKETS = 32
REL_MAX_DIST = 128
EPS = 1e-6
NEG_INF = -1e30

V7X_VMEM_LIMIT_BYTES = 56 * 1024 * 1024
BF16 = jnp.bfloat16
F32 = jnp.float32


def _apply_act(y, act):
    if act == "relu2":
        return jnp.square(jnp.maximum(y, 0.0))
    if act == "silu":
        return y * jax.nn.sigmoid(y)
    assert act is None
    return y


def _mm_kernel(*refs, act, has_bias, has_res, nk):
    x_ref, w_ref = refs[0], refs[1]
    pos = 2
    b_ref = r_ref = None
    if has_bias:
        b_ref = refs[pos]
        pos += 1
    if has_res:
        r_ref = refs[pos]
        pos += 1
    o_ref = refs[pos]
    acc_ref = refs[pos + 1] if nk > 1 else None

    def epilogue(y):
        if has_bias:
            y = y + b_ref[...]
        y = _apply_act(y, act)
        if has_res:
            y = y + r_ref[...]
        o_ref[...] = y.astype(o_ref.dtype)

    part = jnp.dot(x_ref[...], w_ref[...], preferred_element_type=F32)
    if nk == 1:
        epilogue(part)
        return
    k = pl.program_id(2)

    @pl.when(k == 0)
    def _():
        acc_ref[...] = part

    @pl.when(jnp.logical_and(k > 0, k < nk - 1))
    def _():
        acc_ref[...] += part

    @pl.when(k == nk - 1)
    def _():
        epilogue(acc_ref[...] + part)


def _pick(dim, pref):
    t = min(dim, pref)
    assert dim % t == 0, (dim, pref)
    return t


def matmul(x, w, bias=None, res=None, act=None, out_dtype=F32, tm=1024, tn=1024, tk=4096):
    M, K = x.shape
    K2, N = w.shape
    assert K == K2
    tm, tn, tk = _pick(M, tm), _pick(N, tn), _pick(K, tk)
    nk = K // tk
    in_specs = [pl.BlockSpec((tm, tk), lambda m, n, k: (m, k)),
                pl.BlockSpec((tk, tn), lambda m, n, k: (k, n))]
    args = [x, w]
    if bias is not None:
        in_specs.append(pl.BlockSpec((1, tn), lambda m, n, k: (0, n)))
        args.append(bias.reshape(1, N).astype(F32))
    if res is not None:
        in_specs.append(pl.BlockSpec((tm, tn), lambda m, n, k: (m, n)))
        args.append(res)
    scratch = [pltpu.VMEM((tm, tn), F32)] if nk > 1 else []
    return pl.pallas_call(
        functools.partial(_mm_kernel, act=act, has_bias=bias is not None,
                          has_res=res is not None, nk=nk),
        grid=(M // tm, N // tn, nk),
        in_specs=in_specs,
        out_specs=pl.BlockSpec((tm, tn), lambda m, n, k: (m, n)),
        out_shape=jax.ShapeDtypeStruct((M, N), out_dtype),
        scratch_shapes=scratch,
        compiler_params=pltpu.CompilerParams(
            dimension_semantics=("parallel", "parallel", "arbitrary"),
            vmem_limit_bytes=V7X_VMEM_LIMIT_BYTES),
    )(*args)


def _glu_kernel(x_ref, wa_ref, wg_ref, ba_ref, bg_ref, o_ref):
    x = x_ref[...]
    a = jnp.dot(x, wa_ref[...], preferred_element_type=F32) + ba_ref[...]
    g = jnp.dot(x, wg_ref[...], preferred_element_type=F32) + bg_ref[...]
    o_ref[...] = a * jax.nn.sigmoid(g)


def matmul_glu(x, wa, wg, ba, bg, tm=1024, tn=512):
    M, K = x.shape
    N = wa.shape[1]
    tm, tn = _pick(M, tm), _pick(N, tn)
    wspec = pl.BlockSpec((K, tn), lambda m, n: (0, n))
    bspec = pl.BlockSpec((1, tn), lambda m, n: (0, n))
    return pl.pallas_call(
        _glu_kernel,
        grid=(M // tm, N // tn),
        in_specs=[pl.BlockSpec((tm, K), lambda m, n: (m, 0)), wspec, wspec, bspec, bspec],
        out_specs=pl.BlockSpec((tm, tn), lambda m, n: (m, n)),
        out_shape=jax.ShapeDtypeStruct((M, N), F32),
        compiler_params=pltpu.CompilerParams(
            dimension_semantics=("parallel", "parallel"),
            vmem_limit_bytes=V7X_VMEM_LIMIT_BYTES),
    )(x, wa, wg, ba.reshape(1, N), bg.reshape(1, N))


def rms_norm(x, g):
    x32 = x.astype(F32)
    return x32 * lax.rsqrt(jnp.mean(x32 * x32, axis=-1, keepdims=True) + EPS) * g


def layer_norm(x, g, b):
    mu = jnp.mean(x, axis=-1, keepdims=True)
    var = jnp.mean(jnp.square(x - mu), axis=-1, keepdims=True)
    return (x - mu) * lax.rsqrt(var + EPS) * g + b


def mm3(x, w, **kw):
    B, S, K = x.shape
    res = kw.pop("res", None)
    if res is not None:
        res = res.reshape(B * S, -1)
    y = matmul(x.reshape(B * S, K).astype(BF16), w, res=res, **kw)
    return y.reshape(B, S, -1)


def conv_module(h, buf, p):
    B, S, D = h.shape
    glu = matmul_glu(h.reshape(B * S, D).astype(BF16), p["wa"], p["wg"], p["ba"], p["bg"]).reshape(B, S, D)
    ext = jnp.concatenate([buf, glu], axis=1)
    conv = lax.conv_general_dilated(
        ext, p["dw_w"][:, None, :], window_strides=(1,), padding='VALID',
        dimension_numbers=('NWC', 'WIO', 'NWC'), feature_group_count=D_MODEL,
        precision=lax.Precision.HIGHEST) + p["dw_b"]
    z = jax.nn.silu(layer_norm(conv, p["ln_g"], p["ln_b"]))
    return z, ext[:, -CONV_BUF:]


def rel_bucket(dist):
    n = jnp.maximum(dist, 0)
    max_exact = REL_BUCKETS // 2
    nf = jnp.maximum(n, 1).astype(jnp.float32)
    large = max_exact + (jnp.log(nf / max_exact) / math.log(REL_MAX_DIST / max_exact)
                         * (REL_BUCKETS - max_exact)).astype(jnp.int32)
    large = jnp.minimum(large, REL_BUCKETS - 1)
    return jnp.where(n < max_exact, n, large)


def head_bias(rel_table, dist):
    b = rel_table[rel_bucket(dist)].astype(jnp.float32)
    return b.reshape(dist.shape + (KV_GROUPS, HEADS_PER_GROUP)).transpose(2, 3, 0, 1)


def nsa_project(h, p):
    B, S, D = h.shape
    hb = h.reshape(B * S, D).astype(BF16)
    q = matmul(hb, p["w_q"]).reshape(B, S, KV_GROUPS, HEADS_PER_GROUP, HEAD_DIM)
    kvs = [matmul(hb, p["w_kv"][i]).reshape(B, S, 2, KV_GROUPS, HEAD_DIM) for i in range(N_BRANCHES)]
    gl = matmul(hb, p["w_gate"])[:, :N_BRANCHES * N_HEADS]
    gates = jax.nn.sigmoid(gl).reshape(B, S, N_BRANCHES, KV_GROUPS, HEADS_PER_GROUP)
    return q, kvs[0], kvs[1], kvs[2], gates


def compress_blocks(rows, pos_emb, w1, b1, w2, b2):
    B, L = rows.shape[:2]
    nb = L // BLOCK
    blk = rows[:, :nb * BLOCK].reshape(B, nb, BLOCK, KV_GROUPS, HEAD_DIM) + pos_emb[:, None, :]
    blk = blk.transpose(0, 1, 3, 2, 4).reshape(B * nb * KV_GROUPS, BLOCK * HEAD_DIM)
    hid = matmul(blk.astype(BF16), w1, bias=b1, act="silu", out_dtype=BF16, tm=512, tn=512, tk=8192)
    out = matmul(hid, w2, bias=b2, tm=512, tn=128, tk=512)
    return out.reshape(B, nb, KV_GROUPS, HEAD_DIM)


def compressed_branch(q, kc, vc, q_pos, rel_table):
    nb = kc.shape[1]
    blk_end = jnp.arange(nb) * BLOCK + (BLOCK - 1)
    dist = q_pos[:, None] - blk_end[None, :]
    valid = dist >= 0
    logits = jnp.einsum('bqgrd,bngd->bgrqn', q, kc).astype(jnp.float32) * HEAD_DIM ** -0.5
    logits = jnp.where(valid, logits + head_bias(rel_table, dist), NEG_INF)
    p = jnp.where(valid, jax.nn.softmax(logits, axis=-1), 0.0)
    out = jnp.einsum('bgrqn,bngd->bqgrd', p.astype(vc.dtype), vc)
    return out, p


def select_blocks(p_cmp, q_pos, n_blocks):
    imp = p_cmp.sum(axis=2)
    imp = jnp.pad(imp, ((0, 0), (0, 0), (0, 0), (0, n_blocks - imp.shape[-1])))
    j = jnp.arange(n_blocks)
    cur = (q_pos // BLOCK)[:, None]
    forced = (j == 0) | (j == cur) | (j == cur - 1)
    imp = jnp.where(j > cur, -jnp.inf, jnp.where(forced, jnp.inf, imp))
    _, idx = lax.top_k(imp, min(N_SELECT, n_blocks))
    return idx


def selected_attend(q, kvg, idx, q_pos, rel_table):
    kg = kvg[..., 0, :]
    vg = kvg[..., 1, :]
    k_pos = idx[..., None] * BLOCK + jnp.arange(BLOCK)
    dist = q_pos[None, None, :, None, None] - k_pos
    valid = dist >= 0
    table_g = rel_table.reshape(REL_BUCKETS, KV_GROUPS, HEADS_PER_GROUP).transpose(1, 0, 2)
    bias = table_g[jnp.arange(KV_GROUPS)[None, :, None, None, None], rel_bucket(dist)]
    bias = jnp.moveaxis(bias.astype(jnp.float32), -1, 3)
    logits = jnp.einsum('bqgrd,bgqnkd->bgqrnk', q, kg).astype(jnp.float32) * HEAD_DIM ** -0.5
    logits = jnp.where(valid[:, :, :, None], logits + bias, NEG_INF)
    shp = logits.shape
    p = jax.nn.softmax(logits.reshape(shp[:4] + (-1,)), axis=-1).reshape(shp)
    return jnp.einsum('bgqrnk,bgqnkd->bqgrd', p.astype(vg.dtype), vg)


def gather_contiguous(kv_rows, idx):
    B, T = kv_rows.shape[:2]
    blocks = kv_rows.reshape(B, T // BLOCK, BLOCK, 2, KV_GROUPS, HEAD_DIM)
    bi = jnp.arange(B)[:, None, None, None]
    gi = jnp.arange(KV_GROUPS)[None, :, None, None]
    return blocks[bi, idx, :, :, gi, :]


def gather_paged(pool, kv_new, page_table, idx):
    B, S = kv_new.shape[:2]
    bpp = PAGE_SIZE // BLOCK
    past_blocks = page_table.shape[1] * bpp
    bi = jnp.arange(B)[:, None, None, None]
    gi = jnp.arange(KV_GROUPS)[None, :, None, None]
    pool6 = pool.reshape(pool.shape[0], bpp, BLOCK, 2, KV_GROUPS, HEAD_DIM)
    safe = jnp.minimum(idx, past_blocks - 1)
    phys = page_table[bi, safe // bpp]
    past = pool6[phys, safe % bpp, :, :, gi, :]
    nnb = -(-S // BLOCK)
    newp = jnp.pad(kv_new, ((0, 0), (0, nnb * BLOCK - S), (0, 0), (0, 0), (0, 0)))
    newp = newp.reshape(B, nnb, BLOCK, 2, KV_GROUPS, HEAD_DIM)
    new = newp[bi, jnp.clip(idx - past_blocks, 0, nnb - 1), :, :, gi, :]
    return jnp.where((idx < past_blocks)[..., None, None, None], past, new)


def window_attend(q, k, v, q_pos, k_pos, rel_table):
    dist = q_pos[:, None] - k_pos[None, :]
    mask = (dist >= 0) & (dist <= WINDOW) & (k_pos[None, :] >= 0)
    logits = jnp.einsum('bqgrd,bkgd->bgrqk', q, k).astype(jnp.float32) * HEAD_DIM ** -0.5
    logits = jnp.where(mask, logits + head_bias(rel_table, dist), NEG_INF)
    p = jax.nn.softmax(logits, axis=-1)
    return jnp.einsum('bgrqk,bkgd->bqgrd', p.astype(v.dtype), v)


def window_branch_prompt(q, k, v, rel_table):
    B, T = q.shape[:2]
    pad = ((0, 0), (WINDOW, 0), (0, 0), (0, 0))
    kp = jnp.pad(k, pad)
    vp = jnp.pad(v, pad)
    n_qb = T // WIN_Q_BLOCK
    qb = q.reshape(B, n_qb, WIN_Q_BLOCK, KV_GROUPS, HEADS_PER_GROUP, HEAD_DIM).swapaxes(0, 1)

    def one(args):
        i, qi = args
        start = i * WIN_Q_BLOCK
        ki = lax.dynamic_slice_in_dim(kp, start, WIN_Q_BLOCK + WINDOW, axis=1)
        vi = lax.dynamic_slice_in_dim(vp, start, WIN_Q_BLOCK + WINDOW, axis=1)
        q_pos = start + jnp.arange(WIN_Q_BLOCK)
        k_pos = start - WINDOW + jnp.arange(WIN_Q_BLOCK + WINDOW)
        return window_attend(qi, ki, vi, q_pos, k_pos, rel_table)

    out = lax.map(one, (jnp.arange(n_qb), qb))
    return out.swapaxes(0, 1).reshape(B, T, KV_GROUPS, HEADS_PER_GROUP, HEAD_DIM)


def selected_branch_prompt(q, kv_sel, idx, rel_table):
    B, T = q.shape[:2]
    n = idx.shape[-1]
    nc = T // SEL_Q_CHUNK
    qc = q.reshape(B, nc, SEL_Q_CHUNK, KV_GROUPS, HEADS_PER_GROUP, HEAD_DIM).swapaxes(0, 1)
    ic = idx.reshape(B, KV_GROUPS, nc, SEL_Q_CHUNK, n).transpose(2, 0, 1, 3, 4)

    def one(args):
        c, qi, ii = args
        q_pos = c * SEL_Q_CHUNK + jnp.arange(SEL_Q_CHUNK)
        return selected_attend(qi, gather_contiguous(kv_sel, ii), ii, q_pos, rel_table)

    out = lax.map(one, (jnp.arange(nc), qc, ic))
    return out.swapaxes(0, 1).reshape(B, T, KV_GROUPS, HEADS_PER_GROUP, HEAD_DIM)


def nsa_combine(o_cmp, o_sel, o_win, gates):
    B, S = o_cmp.shape[:2]
    g = gates.astype(o_cmp.dtype)[..., None]
    o = g[:, :, 0] * o_cmp + g[:, :, 1] * o_sel + g[:, :, 2] * o_win
    return o.reshape(B, S, N_HEADS * HEAD_DIM)


def nsa_prompt(h, p, rel_table):
    B, T = h.shape[:2]
    q, kv_cmp, kv_sel, kv_win, gates = nsa_project(h, p)
    q_pos = jnp.arange(T)
    kc = compress_blocks(kv_cmp[:, :, 0], p["pos"][0], p["w1"][0], p["b1"][0], p["w2"][0], p["b2"][0])
    vc = compress_blocks(kv_cmp[:, :, 1], p["pos"][1], p["w1"][1], p["b1"][1], p["w2"][1], p["b2"][1])
    o_cmp, p_cmp = compressed_branch(q, kc, vc, q_pos, rel_table)
    idx = select_blocks(p_cmp, q_pos, -(-T // BLOCK))
    o_sel = selected_branch_prompt(q, kv_sel, idx, rel_table)
    o_win = window_branch_prompt(q, kv_win[:, :, 0], kv_win[:, :, 1], rel_table)
    o = nsa_combine(o_cmp, o_sel, o_win, gates)
    return o, kv_cmp, kv_sel, kv_win[:, T - min(WINDOW, T):]


def nsa_sample(h, pool_cmp, pool_sel, win_buf, page_table, p, rel_table):
    B, S = h.shape[:2]
    past_len = page_table.shape[1] * PAGE_SIZE
    q, kv_cmp, kv_sel, kv_win, gates = nsa_project(h, p)
    q_pos = past_len + jnp.arange(S)
    past_cmp = pool_cmp[page_table].reshape(B, past_len, 2, KV_GROUPS, HEAD_DIM)
    all_cmp = jnp.concatenate([past_cmp, kv_cmp], axis=1)
    kc = compress_blocks(all_cmp[:, :, 0], p["pos"][0], p["w1"][0], p["b1"][0], p["w2"][0], p["b2"][0])
    vc = compress_blocks(all_cmp[:, :, 1], p["pos"][1], p["w1"][1], p["b1"][1], p["w2"][1], p["b2"][1])
    o_cmp, p_cmp = compressed_branch(q, kc, vc, q_pos, rel_table)
    idx = select_blocks(p_cmp, q_pos, -(-(past_len + S) // BLOCK))
    o_sel = selected_attend(q, gather_paged(pool_sel, kv_sel, page_table, idx), idx, q_pos, rel_table)
    lw = win_buf.shape[1]
    all_win = jnp.concatenate([win_buf, kv_win], axis=1)
    k_pos = past_len - lw + jnp.arange(lw + S)
    o_win = window_attend(q, all_win[:, :, 0], all_win[:, :, 1], q_pos, k_pos, rel_table)
    o = nsa_combine(o_cmp, o_sel, o_win, gates)
    return o, kv_cmp, kv_sel, all_win[:, lw + S - min(WINDOW, lw + S):]


def ffn(x, g, w_up, w_down):
    B, S, D = x.shape
    h = rms_norm(x, g).reshape(B * S, D).astype(BF16)
    hid = matmul(h, w_up, act="relu2", out_dtype=BF16)
    return matmul(hid, w_down, res=x.reshape(B * S, D), tk=2048).reshape(B, S, D)


def kernel(x_prompt, x_sample, state_conv, cache_kv_cmp, cache_kv_sel, state_kv_win, page_table, norm_mix_g, norm_ffn_g, norm_final_g, rel_bias_table, conv_w_in, conv_b_in, conv_dw_w, conv_dw_b, conv_ln_g, conv_ln_b, conv_w_out, conv_b_out, nsa_w_in, nsa_cmp_pos, nsa_cmp_w1, nsa_cmp_b1, nsa_cmp_w2, nsa_cmp_b2, nsa_w_out, ffn_w_up, ffn_w_down):
    xp, xs = x_prompt, x_sample
    qd = N_HEADS * HEAD_DIM
    kvd = 2 * KV_WIDTH

    cp = dict(wa=conv_w_in[0][:, :D_MODEL].astype(BF16), wg=conv_w_in[0][:, D_MODEL:].astype(BF16),
              ba=conv_b_in[0][:D_MODEL], bg=conv_b_in[0][D_MODEL:], dw_w=conv_dw_w[0], dw_b=conv_dw_b[0],
              ln_g=conv_ln_g[0], ln_b=conv_ln_b[0])
    w_out0 = conv_w_out[0].astype(BF16)
    zero_buf = jnp.zeros((xp.shape[0], CONV_BUF, D_MODEL), F32)
    zp, conv_p = conv_module(rms_norm(xp, norm_mix_g[0]), zero_buf, cp)
    zs, conv_s = conv_module(rms_norm(xs, norm_mix_g[0]), state_conv[0], cp)
    xp = mm3(zp, w_out0, bias=conv_b_out[0], res=xp)
    xs = mm3(zs, w_out0, bias=conv_b_out[0], res=xs)
    w_up, w_dn = ffn_w_up[0].astype(BF16), ffn_w_down[0].astype(BF16)
    xp = ffn(xp, norm_ffn_g[0], w_up, w_dn)
    xs = ffn(xs, norm_ffn_g[0], w_up, w_dn)

    w_in = nsa_w_in[0]
    gate_w = jnp.pad(w_in[:, qd + N_BRANCHES * kvd:], ((0, 0), (0, 128 - N_BRANCHES * N_HEADS)))
    npar = dict(w_q=w_in[:, :qd].astype(BF16),
                w_kv=[w_in[:, qd + i * kvd: qd + (i + 1) * kvd].astype(BF16) for i in range(N_BRANCHES)],
                w_gate=gate_w.astype(BF16), pos=nsa_cmp_pos[0], w1=nsa_cmp_w1[0].astype(BF16),
                b1=nsa_cmp_b1[0], w2=nsa_cmp_w2[0].astype(BF16), b2=nsa_cmp_b2[0])
    w_out1 = nsa_w_out[0].astype(BF16)
    op, kcp, ksp, kwp = nsa_prompt(rms_norm(xp, norm_mix_g[1]), npar, rel_bias_table)
    os_, kcs, kss, kws = nsa_sample(rms_norm(xs, norm_mix_g[1]), cache_kv_cmp[0], cache_kv_sel[0],
                                    state_kv_win[0], page_table, npar, rel_bias_table)
    xp = mm3(op, w_out1, res=xp)
    xs = mm3(os_, w_out1, res=xs)
    w_up, w_dn = ffn_w_up[1].astype(BF16), ffn_w_down[1].astype(BF16)
    xp = ffn(xp, norm_ffn_g[1], w_up, w_dn)
    xs = ffn(xs, norm_ffn_g[1], w_up, w_dn)

    y_prompt = rms_norm(xp, norm_final_g)
    y_sample = rms_norm(xs, norm_final_g)
    return (y_prompt, y_sample, conv_p[None], conv_s[None], kcp[None], kcs[None],
            ksp[None], kss[None], kwp[None], kws[None])
```

```python
import functools
import math

import jax
import jax.numpy as jnp
import numpy as np
from jax import lax
from jax.experimental import pallas as pl
from jax.experimental.pallas import tpu as pltpu

D_MODEL = 4096
CONV_WIDTH = 31
CONV_BUF = CONV_WIDTH - 1
N_HEADS = 32
HEAD_DIM = D_MODEL // N_HEADS
KV_GROUPS = 4
HEADS_PER_GROUP = N_HEADS // KV_GROUPS
KV_WIDTH = KV_GROUPS * HEAD_DIM
BLOCK = 64
N_SELECT = 16
WINDOW = 512
WIN_Q_BLOCK = 128
SEL_Q_CHUNK = 16
N_BRANCHES = 3
PAGE_SIZE = 128
REL_BUCKETS = 32
REL_MAX_DIST = 128
EPS = 1e-6
NEG_INF = -1e30

V7X_VMEM_LIMIT_BYTES = 56 * 1024 * 1024
BF16 = jnp.bfloat16
F32 = jnp.float32


def _apply_act(y, act):
    if act == "relu2":
        return jnp.square(jnp.maximum(y, 0.0))
    if act == "silu":
        return y * jax.nn.sigmoid(y)
    assert act is None
    return y


def _mm_kernel(*refs, act, has_bias, has_res, nk):
    x_ref, w_ref = refs[0], refs[1]
    pos = 2
    b_ref = r_ref = None
    if has_bias:
        b_ref = refs[pos]
        pos += 1
    if has_res:
        r_ref = refs[pos]
        pos += 1
    o_ref = refs[pos]
    acc_ref = refs[pos + 1] if nk > 1 else None

    def epilogue(y):
        if has_bias:
            y = y + b_ref[...]
        y = _apply_act(y, act)
        if has_res:
            y = y + r_ref[...]
        o_ref[...] = y.astype(o_ref.dtype)

    part = jnp.dot(x_ref[...], w_ref[...], preferred_element_type=F32)
    if nk == 1:
        epilogue(part)
        return
    k = pl.program_id(2)

    @pl.when(k == 0)
    def _():
        acc_ref[...] = part

    @pl.when(jnp.logical_and(k > 0, k < nk - 1))
    def _():
        acc_ref[...] += part

    @pl.when(k == nk - 1)
    def _():
        epilogue(acc_ref[...] + part)


def _pick(dim, pref):
    t = min(dim, pref)
    assert dim % t == 0, (dim, pref)
    return t


def matmul(x, w, bias=None, res=None, act=None, out_dtype=F32, tm=1024, tn=1024, tk=4096):
    M, K = x.shape
    K2, N = w.shape
    assert K == K2
    tm, tn, tk = _pick(M, tm), _pick(N, tn), _pick(K, tk)
    nk = K // tk
    in_specs = [pl.BlockSpec((tm, tk), lambda m, n, k: (m, k)),
                pl.BlockSpec((tk, tn), lambda m, n, k: (k, n))]
    args = [x, w]
    if bias is not None:
        in_specs.append(pl.BlockSpec((1, tn), lambda m, n, k: (0, n)))
        args.append(bias.reshape(1, N).astype(F32))
    if res is not None:
        in_specs.append(pl.BlockSpec((tm, tn), lambda m, n, k: (m, n)))
        args.append(res)
    scratch = [pltpu.VMEM((tm, tn), F32)] if nk > 1 else []
    return pl.pallas_call(
        functools.partial(_mm_kernel, act=act, has_bias=bias is not None,
                          has_res=res is not None, nk=nk),
        grid=(M // tm, N // tn, nk),
        in_specs=in_specs,
        out_specs=pl.BlockSpec((tm, tn), lambda m, n, k: (m, n)),
        out_shape=jax.ShapeDtypeStruct((M, N), out_dtype),
        scratch_shapes=scratch,
        compiler_params=pltpu.CompilerParams(
            dimension_semantics=("parallel", "parallel", "arbitrary"),
            vmem_limit_bytes=V7X_VMEM_LIMIT_BYTES),
    )(*args)


def _glu_kernel(x_ref, wa_ref, wg_ref, ba_ref, bg_ref, o_ref):
    x = x_ref[...]
    a = jnp.dot(x, wa_ref[...], preferred_element_type=F32) + ba_ref[...]
    g = jnp.dot(x, wg_ref[...], preferred_element_type=F32) + bg_ref[...]
    o_ref[...] = a * jax.nn.sigmoid(g)


def matmul_glu(x, wa, wg, ba, bg, tm=1024, tn=512):
    M, K = x.shape
    N = wa.shape[1]
    tm, tn = _pick(M, tm), _pick(N, tn)
    wspec = pl.BlockSpec((K, tn), lambda m, n: (0, n))
    bspec = pl.BlockSpec((1, tn), lambda m, n: (0, n))
    return pl.pallas_call(
        _glu_kernel,
        grid=(M // tm, N // tn),
        in_specs=[pl.BlockSpec((tm, K), lambda m, n: (m, 0)), wspec, wspec, bspec, bspec],
        out_specs=pl.BlockSpec((tm, tn), lambda m, n: (m, n)),
        out_shape=jax.ShapeDtypeStruct((M, N), F32),
        compiler_params=pltpu.CompilerParams(
            dimension_semantics=("parallel", "parallel"),
            vmem_limit_bytes=V7X_VMEM_LIMIT_BYTES),
    )(x, wa, wg, ba.reshape(1, N), bg.reshape(1, N))


TQ = 128
IMP_FORCED = 16.0
IMP_FUTURE = -1.0
_NT = (((1,), (1,)), ((), ()))


def _nsa_prompt_kernel(q_ref, ks_ref, vs_ref, kw_ref, vw_ref, kc_ref, vc_ref, gl_ref, toe_ref, cb_ref,
                       e_ref, o_ref, selm_ref, m_ref, l_ref, acc_ref, *, T):
    R = HEADS_PER_GROUP
    nb = T // BLOCK
    nqt = T // TQ
    i = pl.program_id(2)
    scale = HEAD_DIM ** -0.5
    qb = q_ref[...]
    qm = jnp.concatenate([qb[:, r * HEAD_DIM:(r + 1) * HEAD_DIM] for r in range(R)], axis=0)
    t_loc = lax.broadcasted_iota(jnp.int32, (TQ, 1), 0)
    q_pos = i * TQ + t_loc

    kc = kc_ref[0, 0].astype(BF16)
    vc = vc_ref[0, 0].astype(BF16)
    sc = lax.dot_general(qm, kc, _NT, preferred_element_type=F32) * scale
    sc = sc.reshape(R, TQ, nb) + cb_ref[0]
    jn = lax.broadcasted_iota(jnp.int32, (TQ, nb), 1)
    valid = (q_pos - (jn * BLOCK + (BLOCK - 1))) >= 0
    sc = jnp.where(valid[None], sc, NEG_INF)
    mc = jnp.max(sc, axis=-1, keepdims=True)
    ec = jnp.where(valid[None], jnp.exp(sc - mc), 0.0)
    lc = jnp.sum(ec, axis=-1, keepdims=True)
    pc = ec / jnp.where(lc > 0.0, lc, 1.0)
    o_cmp = jnp.dot(pc.reshape(R * TQ, nb).astype(BF16), vc, preferred_element_type=F32).reshape(R, TQ, HEAD_DIM)
    imp = pc[0]
    for r in range(1, R):
        imp = imp + pc[r]

    cur = q_pos // BLOCK
    forced = (jn == 0) | (jn == cur) | (jn == cur - 1)
    impp = jnp.where(jn > cur, IMP_FUTURE, jnp.where(forced, IMP_FORCED, imp))
    rank = jnp.zeros((TQ, nb), jnp.int32)
    for jp in range(nb):
        col = impp[:, jp:jp + 1]
        beats = (col > impp) | ((col == impp) & (jn > jp))
        rank = rank + beats.astype(jnp.int32)
    sel = (rank < min(N_SELECT, nb)).astype(BF16)
    selm = jnp.dot(sel, e_ref[...], preferred_element_type=F32)
    for kt in range(nqt):
        selm_ref[kt] = selm[:, kt * TQ:(kt + 1) * TQ]

    j_loc = lax.broadcasted_iota(jnp.int32, (1, TQ), 1)

    def run_branch(k_ref, v_ref, kt_lo, use_sel):
        m_ref[...] = jnp.full(m_ref.shape, NEG_INF, F32)
        l_ref[...] = jnp.zeros(l_ref.shape, F32)
        acc_ref[...] = jnp.zeros(acc_ref.shape, F32)

        def body(kt, carry):
            ks = pl.multiple_of(kt * TQ, TQ)
            k_t = k_ref[pl.ds(ks, TQ), :].astype(BF16)
            v_t = v_ref[pl.ds(ks, TQ), :].astype(BF16)
            s = lax.dot_general(qm, k_t, _NT, preferred_element_type=F32) * scale
            s = s + toe_ref[0, jnp.minimum(i - kt, 2)]
            dist = (i - kt) * TQ + t_loc - j_loc
            if use_sel:
                mask = (dist >= 0) & (selm_ref[kt] > 0.5)
            else:
                mask = (dist >= 0) & (dist <= WINDOW)
            s = jnp.where(mask[None], s.reshape(R, TQ, TQ), NEG_INF)
            m_old = m_ref[...]
            m_new = jnp.maximum(m_old, jnp.max(s, axis=-1, keepdims=True))
            p = jnp.where(mask[None], jnp.exp(s - m_new), 0.0)
            alpha = jnp.exp(m_old - m_new)
            l_ref[...] = alpha * l_ref[...] + jnp.sum(p, axis=-1, keepdims=True)
            pv = jnp.dot(p.reshape(R * TQ, TQ).astype(BF16), v_t, preferred_element_type=F32)
            acc_ref[...] = alpha * acc_ref[...] + pv.reshape(R, TQ, HEAD_DIM)
            m_ref[...] = m_new
            return carry

        lax.fori_loop(kt_lo, i + 1, body, 0)
        return acc_ref[...] / l_ref[...]

    o_sel = run_branch(ks_ref, vs_ref, 0, True)
    o_win = run_branch(kw_ref, vw_ref, jnp.maximum(i - WINDOW // TQ, 0), False)

    gates = jax.nn.sigmoid(gl_ref[...])
    for r in range(R):
        o = (gates[:, r:r + 1] * o_cmp[r] + gates[:, R + r:R + r + 1] * o_sel[r]
             + gates[:, 2 * R + r:2 * R + r + 1] * o_win[r])
        o_ref[:, r * HEAD_DIM:(r + 1) * HEAD_DIM] = o.astype(o_ref.dtype)


def nsa_prompt_attention(q2, kv_sel, kv_win, kc, vc, gl, toe, cb, B, T):
    G, R = KV_GROUPS, HEADS_PER_GROUP
    nb, nqt = T // BLOCK, T // TQ
    expand = (np.arange(T)[None, :] // BLOCK == np.arange(nb)[:, None])
    e = jnp.asarray(expand, BF16)
    qspec = pl.BlockSpec((TQ, R * HEAD_DIM), lambda b, g, i: (b * nqt + i, g))
    kspec = pl.BlockSpec((T, HEAD_DIM), lambda b, g, i: (b, g))
    vspec = pl.BlockSpec((T, HEAD_DIM), lambda b, g, i: (b, G + g))
    cspec = pl.BlockSpec((1, 1, nb, HEAD_DIM), lambda b, g, i: (b, g, 0, 0))
    return pl.pallas_call(
        functools.partial(_nsa_prompt_kernel, T=T),
        grid=(B, G, nqt),
        in_specs=[qspec, kspec, vspec, kspec, vspec, cspec, cspec,
                  pl.BlockSpec((TQ, 128), lambda b, g, i: (b * nqt + i, g)),
                  pl.BlockSpec((1, 3, R * TQ, TQ), lambda b, g, i: (g, 0, 0, 0)),
                  pl.BlockSpec((1, R, TQ, nb), lambda b, g, i: (g, 0, i, 0)),
                  pl.BlockSpec((nb, T), lambda b, g, i: (0, 0))],
        out_specs=qspec,
        out_shape=jax.ShapeDtypeStruct((B * T, G * R * HEAD_DIM), BF16),
        scratch_shapes=[pltpu.VMEM((nqt, TQ, TQ), F32), pltpu.VMEM((R, TQ, 1), F32),
                        pltpu.VMEM((R, TQ, 1), F32), pltpu.VMEM((R, TQ, HEAD_DIM), F32)],
        compiler_params=pltpu.CompilerParams(
            dimension_semantics=("parallel", "parallel", "arbitrary"),
            vmem_limit_bytes=V7X_VMEM_LIMIT_BYTES),
    )(q2, kv_sel, kv_sel, kv_win, kv_win, kc, vc, gl, toe, cb, e)


def toeplitz_bias(rel_table):
    d = jnp.arange(3)[:, None, None] * TQ + jnp.arange(TQ)[None, :, None] - jnp.arange(TQ)[None, None, :]
    b = rel_table[rel_bucket(d)].astype(F32)
    b = b.reshape(3, TQ, TQ, KV_GROUPS, HEADS_PER_GROUP).transpose(3, 0, 4, 1, 2)
    return b.reshape(KV_GROUPS, 3, HEADS_PER_GROUP * TQ, TQ)


def rms_norm(x, g):
    x32 = x.astype(F32)
    return x32 * lax.rsqrt(jnp.mean(x32 * x32, axis=-1, keepdims=True) + EPS) * g


def layer_norm(x, g, b):
    mu = jnp.mean(x, axis=-1, keepdims=True)
    var = jnp.mean(jnp.square(x - mu), axis=-1, keepdims=True)
    return (x - mu) * lax.rsqrt(var + EPS) * g + b


def mm3(x, w, **kw):
    B, S, K = x.shape
    res = kw.pop("res", None)
    if res is not None:
        res = res.reshape(B * S, -1)
    y = matmul(x.reshape(B * S, K).astype(BF16), w, res=res, **kw)
    return y.reshape(B, S, -1)


def conv_module(h, buf, p):
    B, S, D = h.shape
    glu = matmul_glu(h.reshape(B * S, D).astype(BF16), p["wa"], p["wg"], p["ba"], p["bg"]).reshape(B, S, D)
    ext = jnp.concatenate([buf, glu], axis=1)
    conv = lax.conv_general_dilated(
        ext, p["dw_w"][:, None, :], window_strides=(1,), padding='VALID',
        dimension_numbers=('NWC', 'WIO', 'NWC'), feature_group_count=D_MODEL,
        precision=lax.Precision.HIGHEST) + p["dw_b"]
    z = jax.nn.silu(layer_norm(conv, p["ln_g"], p["ln_b"]))
    return z, ext[:, -CONV_BUF:]


def rel_bucket(dist):
    n = jnp.maximum(dist, 0)
    max_exact = REL_BUCKETS // 2
    nf = jnp.maximum(n, 1).astype(jnp.float32)
    large = max_exact + (jnp.log(nf / max_exact) / math.log(REL_MAX_DIST / max_exact)
                         * (REL_BUCKETS - max_exact)).astype(jnp.int32)
    large = jnp.minimum(large, REL_BUCKETS - 1)
    return jnp.where(n < max_exact, n, large)


def head_bias(rel_table, dist):
    b = rel_table[rel_bucket(dist)].astype(jnp.float32)
    return b.reshape(dist.shape + (KV_GROUPS, HEADS_PER_GROUP)).transpose(2, 3, 0, 1)


def nsa_project(h, p):
    B, S, D = h.shape
    hb = h.reshape(B * S, D).astype(BF16)
    q2 = matmul(hb, p["w_q"], out_dtype=BF16)
    kvs = [matmul(hb, p["w_kv"][i]) for i in range(N_BRANCHES)]
    gl = matmul(hb, p["w_gate"])
    return q2, kvs, gl


def unpack_projection(q2, kvs, gl, B, S):
    q = q2.astype(F32).reshape(B, S, KV_GROUPS, HEADS_PER_GROUP, HEAD_DIM)
    kv5 = [kv.reshape(B, S, 2, KV_GROUPS, HEAD_DIM) for kv in kvs]
    g = gl.reshape(B, S, KV_GROUPS, 128)[..., :N_BRANCHES * HEADS_PER_GROUP]
    gates = jax.nn.sigmoid(g).reshape(B, S, KV_GROUPS, N_BRANCHES, HEADS_PER_GROUP).transpose(0, 1, 3, 2, 4)
    return q, kv5, gates


def compress_blocks(rows, pos_emb, w1, b1, w2, b2):
    B, L = rows.shape[:2]
    nb = L // BLOCK
    blk = rows[:, :nb * BLOCK].reshape(B, nb, BLOCK, KV_GROUPS, HEAD_DIM) + pos_emb[:, None, :]
    blk = blk.transpose(0, 1, 3, 2, 4).reshape(B * nb * KV_GROUPS, BLOCK * HEAD_DIM)
    hid = matmul(blk.astype(BF16), w1, bias=b1, act="silu", out_dtype=BF16, tm=512, tn=512, tk=8192)
    out = matmul(hid, w2, bias=b2, tm=512, tn=128, tk=512)
    return out.reshape(B, nb, KV_GROUPS, HEAD_DIM)


def compressed_branch(q, kc, vc, q_pos, rel_table):
    nb = kc.shape[1]
    blk_end = jnp.arange(nb) * BLOCK + (BLOCK - 1)
    dist = q_pos[:, None] - blk_end[None, :]
    valid = dist >= 0
    logits = jnp.einsum('bqgrd,bngd->bgrqn', q, kc).astype(jnp.float32) * HEAD_DIM ** -0.5
    logits = jnp.where(valid, logits + head_bias(rel_table, dist), NEG_INF)
    p = jnp.where(valid, jax.nn.softmax(logits, axis=-1), 0.0)
    out = jnp.einsum('bgrqn,bngd->bqgrd', p.astype(vc.dtype), vc)
    return out, p


def select_blocks(p_cmp, q_pos, n_blocks):
    imp = p_cmp.sum(axis=2)
    imp = jnp.pad(imp, ((0, 0), (0, 0), (0, 0), (0, n_blocks - imp.shape[-1])))
    j = jnp.arange(n_blocks)
    cur = (q_pos // BLOCK)[:, None]
    forced = (j == 0) | (j == cur) | (j == cur - 1)
    imp = jnp.where(j > cur, -jnp.inf, jnp.where(forced, jnp.inf, imp))
    _, idx = lax.top_k(imp, min(N_SELECT, n_blocks))
    return idx


def selected_attend(q, kvg, idx, q_pos, rel_table):
    kg = kvg[..., 0, :]
    vg = kvg[..., 1, :]
    k_pos = idx[..., None] * BLOCK + jnp.arange(BLOCK)
    dist = q_pos[None, None, :, None, None] - k_pos
    valid = dist >= 0
    table_g = rel_table.reshape(REL_BUCKETS, KV_GROUPS, HEADS_PER_GROUP).transpose(1, 0, 2)
    bias = table_g[jnp.arange(KV_GROUPS)[None, :, None, None, None], rel_bucket(dist)]
    bias = jnp.moveaxis(bias.astype(jnp.float32), -1, 3)
    logits = jnp.einsum('bqgrd,bgqnkd->bgqrnk', q, kg).astype(jnp.float32) * HEAD_DIM ** -0.5
    logits = jnp.where(valid[:, :, :, None], logits + bias, NEG_INF)
    shp = logits.shape
    p = jax.nn.softmax(logits.reshape(shp[:4] + (-1,)), axis=-1).reshape(shp)
    return jnp.einsum('bgqrnk,bgqnkd->bqgrd', p.astype(vg.dtype), vg)


def gather_paged(pool, kv_new, page_table, idx):
    B, S = kv_new.shape[:2]
    bpp = PAGE_SIZE // BLOCK
    past_blocks = page_table.shape[1] * bpp
    bi = jnp.arange(B)[:, None, None, None]
    gi = jnp.arange(KV_GROUPS)[None, :, None, None]
    pool6 = pool.reshape(pool.shape[0], bpp, BLOCK, 2, KV_GROUPS, HEAD_DIM)
    safe = jnp.minimum(idx, past_blocks - 1)
    phys = page_table[bi, safe // bpp]
    past = pool6[phys, safe % bpp, :, :, gi, :]
    nnb = -(-S // BLOCK)
    newp = jnp.pad(kv_new, ((0, 0), (0, nnb * BLOCK - S), (0, 0), (0, 0), (0, 0)))
    newp = newp.reshape(B, nnb, BLOCK, 2, KV_GROUPS, HEAD_DIM)
    new = newp[bi, jnp.clip(idx - past_blocks, 0, nnb - 1), :, :, gi, :]
    return jnp.where((idx < past_blocks)[..., None, None, None], past, new)


def window_attend(q, k, v, q_pos, k_pos, rel_table):
    dist = q_pos[:, None] - k_pos[None, :]
    mask = (dist >= 0) & (dist <= WINDOW) & (k_pos[None, :] >= 0)
    logits = jnp.einsum('bqgrd,bkgd->bgrqk', q, k).astype(jnp.float32) * HEAD_DIM ** -0.5
    logits = jnp.where(mask, logits + head_bias(rel_table, dist), NEG_INF)
    p = jax.nn.softmax(logits, axis=-1)
    return jnp.einsum('bgrqk,bkgd->bqgrd', p.astype(v.dtype), v)


def nsa_combine(o_cmp, o_sel, o_win, gates):
    B, S = o_cmp.shape[:2]
    g = gates.astype(o_cmp.dtype)[..., None]
    o = g[:, :, 0] * o_cmp + g[:, :, 1] * o_sel + g[:, :, 2] * o_win
    return o.reshape(B, S, N_HEADS * HEAD_DIM)


def nsa_prompt(h, p, rel_table):
    B, T = h.shape[:2]
    q2, kvs, gl = nsa_project(h, p)
    kv_cmp, kv_sel, kv_win = [kv.reshape(B, T, 2, KV_GROUPS, HEAD_DIM) for kv in kvs]
    kc = compress_blocks(kv_cmp[:, :, 0], p["pos"][0], p["w1"][0], p["b1"][0], p["w2"][0], p["b2"][0])
    vc = compress_blocks(kv_cmp[:, :, 1], p["pos"][1], p["w1"][1], p["b1"][1], p["w2"][1], p["b2"][1])
    nb = T // BLOCK
    dist_c = jnp.arange(T)[:, None] - (jnp.arange(nb) * BLOCK + (BLOCK - 1))[None, :]
    o = nsa_prompt_attention(q2, kvs[1], kvs[2], kc.transpose(0, 2, 1, 3), vc.transpose(0, 2, 1, 3), gl,
                             toeplitz_bias(rel_table), head_bias(rel_table, dist_c), B, T)
    return o.reshape(B, T, -1), kv_cmp, kv_sel, kv_win[:, T - min(WINDOW, T):]


def nsa_sample(h, pool_cmp, pool_sel, win_buf, page_table, p, rel_table):
    B, S = h.shape[:2]
    past_len = page_table.shape[1] * PAGE_SIZE
    q, (kv_cmp, kv_sel, kv_win), gates = unpack_projection(*nsa_project(h, p), B, S)
    q_pos = past_len + jnp.arange(S)
    past_cmp = pool_cmp[page_table].reshape(B, past_len, 2, KV_GROUPS, HEAD_DIM)
    all_cmp = jnp.concatenate([past_cmp, kv_cmp], axis=1)
    kc = compress_blocks(all_cmp[:, :, 0], p["pos"][0], p["w1"][0], p["b1"][0], p["w2"][0], p["b2"][0])
    vc = compress_blocks(all_cmp[:, :, 1], p["pos"][1], p["w1"][1], p["b1"][1], p["w2"][1], p["b2"][1])
    o_cmp, p_cmp = compressed_branch(q, kc, vc, q_pos, rel_table)
    idx = select_blocks(p_cmp, q_pos, -(-(past_len + S) // BLOCK))
    o_sel = selected_attend(q, gather_paged(pool_sel, kv_sel, page_table, idx), idx, q_pos, rel_table)
    lw = win_buf.shape[1]
    all_win = jnp.concatenate([win_buf, kv_win], axis=1)
    k_pos = past_len - lw + jnp.arange(lw + S)
    o_win = window_attend(q, all_win[:, :, 0], all_win[:, :, 1], q_pos, k_pos, rel_table)
    o = nsa_combine(o_cmp, o_sel, o_win, gates)
    return o, kv_cmp, kv_sel, all_win[:, lw + S - min(WINDOW, lw + S):]


def ffn(x, g, w_up, w_down):
    B, S, D = x.shape
    h = rms_norm(x, g).reshape(B * S, D).astype(BF16)
    hid = matmul(h, w_up, act="relu2", out_dtype=BF16)
    return matmul(hid, w_down, res=x.reshape(B * S, D), tk=2048).reshape(B, S, D)


def kernel(x_prompt, x_sample, state_conv, cache_kv_cmp, cache_kv_sel, state_kv_win, page_table, norm_mix_g, norm_ffn_g, norm_final_g, rel_bias_table, conv_w_in, conv_b_in, conv_dw_w, conv_dw_b, conv_ln_g, conv_ln_b, conv_w_out, conv_b_out, nsa_w_in, nsa_cmp_pos, nsa_cmp_w1, nsa_cmp_b1, nsa_cmp_w2, nsa_cmp_b2, nsa_w_out, ffn_w_up, ffn_w_down):
    xp, xs = x_prompt, x_sample
    qd = N_HEADS * HEAD_DIM
    kvd = 2 * KV_WIDTH

    cp = dict(wa=conv_w_in[0][:, :D_MODEL].astype(BF16), wg=conv_w_in[0][:, D_MODEL:].astype(BF16),
              ba=conv_b_in[0][:D_MODEL], bg=conv_b_in[0][D_MODEL:], dw_w=conv_dw_w[0], dw_b=conv_dw_b[0],
              ln_g=conv_ln_g[0], ln_b=conv_ln_b[0])
    w_out0 = conv_w_out[0].astype(BF16)
    zero_buf = jnp.zeros((xp.shape[0], CONV_BUF, D_MODEL), F32)
    zp, conv_p = conv_module(rms_norm(xp, norm_mix_g[0]), zero_buf, cp)
    zs, conv_s = conv_module(rms_norm(xs, norm_mix_g[0]), state_conv[0], cp)
    xp = mm3(zp, w_out0, bias=conv_b_out[0], res=xp)
    xs = mm3(zs, w_out0, bias=conv_b_out[0], res=xs)
    w_up, w_dn = ffn_w_up[0].astype(BF16), ffn_w_down[0].astype(BF16)
    xp = ffn(xp, norm_ffn_g[0], w_up, w_dn)
    xs = ffn(xs, norm_ffn_g[0], w_up, w_dn)

    w_in = nsa_w_in[0]
    nbr = N_BRANCHES * HEADS_PER_GROUP
    gate_w = w_in[:, qd + N_BRANCHES * kvd:].reshape(D_MODEL, N_BRANCHES, KV_GROUPS, HEADS_PER_GROUP)
    gate_w = gate_w.transpose(0, 2, 1, 3).reshape(D_MODEL, KV_GROUPS, nbr)
    gate_w = jnp.pad(gate_w, ((0, 0), (0, 0), (0, 128 - nbr))).reshape(D_MODEL, KV_GROUPS * 128)
    npar = dict(w_q=w_in[:, :qd].astype(BF16),
                w_kv=[w_in[:, qd + i * kvd: qd + (i + 1) * kvd].astype(BF16) for i in range(N_BRANCHES)],
                w_gate=gate_w.astype(BF16), pos=nsa_cmp_pos[0], w1=nsa_cmp_w1[0].astype(BF16),
                b1=nsa_cmp_b1[0], w2=nsa_cmp_w2[0].astype(BF16), b2=nsa_cmp_b2[0])
    w_out1 = nsa_w_out[0].astype(BF16)
    op, kcp, ksp, kwp = nsa_prompt(rms_norm(xp, norm_mix_g[1]), npar, rel_bias_table)
    os_, kcs, kss, kws = nsa_sample(rms_norm(xs, norm_mix_g[1]), cache_kv_cmp[0], cache_kv_sel[0],
                                    state_kv_win[0], page_table, npar, rel_bias_table)
    xp = mm3(op, w_out1, res=xp)
    xs = mm3(os_, w_out1, res=xs)
    w_up, w_dn = ffn_w_up[1].astype(BF16), ffn_w_down[1].astype(BF16)
    xp = ffn(xp, norm_ffn_g[1], w_up, w_dn)
    xs = ffn(xs, norm_ffn_g[1], w_up, w_dn)

    y_prompt = rms_norm(xp, norm_final_g)
    y_sample = rms_norm(xs, norm_final_g)
    return (y_prompt, y_sample, conv_p[None], conv_s[None], kcp[None], kcs[None],
            ksp[None], kss[None], kwp[None], kws[None])
```

```python
import functools
import math

import jax
import jax.numpy as jnp
import numpy as np
from jax import lax
from jax.experimental import pallas as pl
from jax.experimental.pallas import tpu as pltpu

D_MODEL = 4096
CONV_WIDTH = 31
CONV_BUF = CONV_WIDTH - 1
N_HEADS = 32
HEAD_DIM = D_MODEL // N_HEADS
KV_GROUPS = 4
HEADS_PER_GROUP = N_HEADS // KV_GROUPS
KV_WIDTH = KV_GROUPS * HEAD_DIM
BLOCK = 64
N_SELECT = 16
WINDOW = 512
WIN_Q_BLOCK = 128
SEL_Q_CHUNK = 16
N_BRANCHES = 3
PAGE_SIZE = 128
REL_BUCKETS = 32
REL_MAX_DIST = 128
EPS = 1e-6
NEG_INF = -1e30

V7X_VMEM_LIMIT_BYTES = 56 * 1024 * 1024
BF16 = jnp.bfloat16
F32 = jnp.float32


def _apply_act(y, act):
    if act == "relu2":
        return jnp.square(jnp.maximum(y, 0.0))
    if act == "silu":
        return y * jax.nn.sigmoid(y)
    assert act is None
    return y


def _mm_kernel(*refs, act, has_bias, has_res, nk):
    x_ref, w_ref = refs[0], refs[1]
    pos = 2
    b_ref = r_ref = None
    if has_bias:
        b_ref = refs[pos]
        pos += 1
    if has_res:
        r_ref = refs[pos]
        pos += 1
    o_ref = refs[pos]
    acc_ref = refs[pos + 1] if nk > 1 else None

    def epilogue(y):
        if has_bias:
            y = y + b_ref[...]
        y = _apply_act(y, act)
        if has_res:
            y = y + r_ref[...]
        o_ref[...] = y.astype(o_ref.dtype)

    part = jnp.dot(x_ref[...], w_ref[...], preferred_element_type=F32)
    if nk == 1:
        epilogue(part)
        return
    k = pl.program_id(2)

    @pl.when(k == 0)
    def _():
        acc_ref[...] = part

    @pl.when(jnp.logical_and(k > 0, k < nk - 1))
    def _():
        acc_ref[...] += part

    @pl.when(k == nk - 1)
    def _():
        epilogue(acc_ref[...] + part)


def _pick(dim, pref):
    t = min(dim, pref)
    assert dim % t == 0, (dim, pref)
    return t


def matmul(x, w, bias=None, res=None, act=None, out_dtype=F32, tm=1024, tn=1024, tk=4096):
    M, K = x.shape
    K2, N = w.shape
    assert K == K2
    tm, tn, tk = _pick(M, tm), _pick(N, tn), _pick(K, tk)
    nk = K // tk
    in_specs = [pl.BlockSpec((tm, tk), lambda m, n, k: (m, k)),
                pl.BlockSpec((tk, tn), lambda m, n, k: (k, n))]
    args = [x, w]
    if bias is not None:
        in_specs.append(pl.BlockSpec((1, tn), lambda m, n, k: (0, n)))
        args.append(bias.reshape(1, N).astype(F32))
    if res is not None:
        in_specs.append(pl.BlockSpec((tm, tn), lambda m, n, k: (m, n)))
        args.append(res)
    scratch = [pltpu.VMEM((tm, tn), F32)] if nk > 1 else []
    return pl.pallas_call(
        functools.partial(_mm_kernel, act=act, has_bias=bias is not None,
                          has_res=res is not None, nk=nk),
        grid=(M // tm, N // tn, nk),
        in_specs=in_specs,
        out_specs=pl.BlockSpec((tm, tn), lambda m, n, k: (m, n)),
        out_shape=jax.ShapeDtypeStruct((M, N), out_dtype),
        scratch_shapes=scratch,
        compiler_params=pltpu.CompilerParams(
            dimension_semantics=("parallel", "parallel", "arbitrary"),
            vmem_limit_bytes=V7X_VMEM_LIMIT_BYTES),
    )(*args)


def _glu_kernel(x_ref, wa_ref, wg_ref, ba_ref, bg_ref, o_ref):
    x = x_ref[...]
    a = jnp.dot(x, wa_ref[...], preferred_element_type=F32) + ba_ref[...]
    g = jnp.dot(x, wg_ref[...], preferred_element_type=F32) + bg_ref[...]
    o_ref[...] = a * jax.nn.sigmoid(g)


def matmul_glu(x, wa, wg, ba, bg, tm=1024, tn=512):
    M, K = x.shape
    N = wa.shape[1]
    tm, tn = _pick(M, tm), _pick(N, tn)
    wspec = pl.BlockSpec((K, tn), lambda m, n: (0, n))
    bspec = pl.BlockSpec((1, tn), lambda m, n: (0, n))
    return pl.pallas_call(
        _glu_kernel,
        grid=(M // tm, N // tn),
        in_specs=[pl.BlockSpec((tm, K), lambda m, n: (m, 0)), wspec, wspec, bspec, bspec],
        out_specs=pl.BlockSpec((tm, tn), lambda m, n: (m, n)),
        out_shape=jax.ShapeDtypeStruct((M, N), F32),
        compiler_params=pltpu.CompilerParams(
            dimension_semantics=("parallel", "parallel"),
            vmem_limit_bytes=V7X_VMEM_LIMIT_BYTES),
    )(x, wa, wg, ba.reshape(1, N), bg.reshape(1, N))


TQ = 128
SEL_KT = 4 * TQ
IMP_FORCED = 16.0
IMP_FUTURE = -1.0
_NT = (((1,), (1,)), ((), ()))


def _nsa_prompt_kernel(q_ref, ks_ref, vs_ref, kw_ref, vw_ref, kc_ref, vc_ref, gl_ref, toe_ref, cb_ref,
                       e_ref, o_ref, selm_ref, m_ref, l_ref, acc_ref, *, T):
    R = HEADS_PER_GROUP
    nb = T // BLOCK
    nqt = T // TQ
    i = pl.program_id(2)
    scale = HEAD_DIM ** -0.5
    qb = q_ref[...]
    qm = jnp.concatenate([qb[:, r * HEAD_DIM:(r + 1) * HEAD_DIM] for r in range(R)], axis=0)
    t_loc = lax.broadcasted_iota(jnp.int32, (TQ, 1), 0)
    q_pos = i * TQ + t_loc

    kc = kc_ref[0, 0].astype(BF16)
    vc = vc_ref[0, 0].astype(BF16)
    sc = lax.dot_general(qm, kc, _NT, preferred_element_type=F32) * scale
    sc = sc.reshape(R, TQ, nb) + cb_ref[0]
    jn = lax.broadcasted_iota(jnp.int32, (TQ, nb), 1)
    valid = (q_pos - (jn * BLOCK + (BLOCK - 1))) >= 0
    sc = jnp.where(valid[None], sc, NEG_INF)
    mc = jnp.max(sc, axis=-1, keepdims=True)
    ec = jnp.where(valid[None], jnp.exp(sc - mc), 0.0)
    lc = jnp.sum(ec, axis=-1, keepdims=True)
    pc = ec / jnp.where(lc > 0.0, lc, 1.0)
    o_cmp = jnp.dot(pc.reshape(R * TQ, nb).astype(BF16), vc, preferred_element_type=F32).reshape(R, TQ, HEAD_DIM)
    imp = pc[0]
    for r in range(1, R):
        imp = imp + pc[r]

    cur = q_pos // BLOCK
    forced = (jn == 0) | (jn == cur) | (jn == cur - 1)
    impp = jnp.where(jn > cur, IMP_FUTURE, jnp.where(forced, IMP_FORCED, imp))
    rank = jnp.zeros((TQ, nb), jnp.int32)
    for jp in range(nb):
        col = impp[:, jp:jp + 1]
        beats = (col > impp) | ((col == impp) & (jn > jp))
        rank = rank + beats.astype(jnp.int32)
    sel = (rank < min(N_SELECT, nb)).astype(BF16)
    selm = jnp.dot(sel, e_ref[...], preferred_element_type=F32)
    for st in range(T // SEL_KT):
        selm_ref[st] = selm[:, st * SEL_KT:(st + 1) * SEL_KT]

    def scores(k_t, kt0, n_sub):
        s = lax.dot_general(qm, k_t, _NT, preferred_element_type=F32) * scale
        bias = jnp.concatenate([toe_ref[0, jnp.clip(i - kt0 - c, 0, 2)] for c in range(n_sub)], axis=1)
        j_loc = lax.broadcasted_iota(jnp.int32, (1, n_sub * TQ), 1)
        dist = (i - kt0) * TQ + t_loc - j_loc
        return (s + bias).reshape(R, TQ, n_sub * TQ), dist

    m_ref[...] = jnp.full(m_ref.shape, NEG_INF, F32)
    l_ref[...] = jnp.zeros(l_ref.shape, F32)
    acc_ref[...] = jnp.zeros(acc_ref.shape, F32)
    sub = SEL_KT // TQ

    def sel_body(st, carry):
        ks = pl.multiple_of(st * SEL_KT, SEL_KT)
        s, dist = scores(ks_ref[pl.ds(ks, SEL_KT), :].astype(BF16), st * sub, sub)
        mask = (dist >= 0) & (selm_ref[st] > 0.5)
        s = jnp.where(mask[None], s, NEG_INF)
        m_old = m_ref[...]
        m_new = jnp.maximum(m_old, jnp.max(s, axis=-1, keepdims=True))
        p = jnp.exp(s - m_new)
        alpha = jnp.exp(m_old - m_new)
        l_ref[...] = alpha * l_ref[...] + jnp.sum(p, axis=-1, keepdims=True)
        pv = jnp.dot(p.reshape(R * TQ, SEL_KT).astype(BF16), vs_ref[pl.ds(ks, SEL_KT), :].astype(BF16),
                     preferred_element_type=F32)
        acc_ref[...] = alpha * acc_ref[...] + pv.reshape(R, TQ, HEAD_DIM)
        m_ref[...] = m_new
        return carry

    lax.fori_loop(0, i // sub + 1, sel_body, 0)
    o_sel = acc_ref[...] / l_ref[...]

    n_win = WINDOW // TQ + 1
    kt0 = jnp.clip(i - WINDOW // TQ, 0, nqt - n_win)
    ws = pl.multiple_of(kt0 * TQ, TQ)
    s, dist = scores(kw_ref[pl.ds(ws, n_win * TQ), :].astype(BF16), kt0, n_win)
    s = jnp.where(((dist >= 0) & (dist <= WINDOW))[None], s, NEG_INF)
    p = jnp.exp(s - jnp.max(s, axis=-1, keepdims=True))
    lw = jnp.sum(p, axis=-1, keepdims=True)
    pv = jnp.dot(p.reshape(R * TQ, n_win * TQ).astype(BF16), vw_ref[pl.ds(ws, n_win * TQ), :].astype(BF16),
                 preferred_element_type=F32)
    o_win = pv.reshape(R, TQ, HEAD_DIM) / lw

    gates = jax.nn.sigmoid(gl_ref[...])
    for r in range(R):
        o = (gates[:, r:r + 1] * o_cmp[r] + gates[:, R + r:R + r + 1] * o_sel[r]
             + gates[:, 2 * R + r:2 * R + r + 1] * o_win[r])
        o_ref[:, r * HEAD_DIM:(r + 1) * HEAD_DIM] = o.astype(o_ref.dtype)


def nsa_prompt_attention(q2, kv_sel, kv_win, kc, vc, gl, toe, cb, B, T):
    G, R = KV_GROUPS, HEADS_PER_GROUP
    nb, nqt = T // BLOCK, T // TQ
    assert T % SEL_KT == 0 and nqt >= WINDOW // TQ + 1
    expand = (np.arange(T)[None, :] // BLOCK == np.arange(nb)[:, None])
    e = jnp.asarray(expand, BF16)
    qspec = pl.BlockSpec((TQ, R * HEAD_DIM), lambda b, g, i: (b * nqt + i, g))
    kspec = pl.BlockSpec((T, HEAD_DIM), lambda b, g, i: (b, g))
    vspec = pl.BlockSpec((T, HEAD_DIM), lambda b, g, i: (b, G + g))
    cspec = pl.BlockSpec((1, 1, nb, HEAD_DIM), lambda b, g, i: (b, g, 0, 0))
    return pl.pallas_call(
        functools.partial(_nsa_prompt_kernel, T=T), name="nsa_prompt_attn",
        grid=(B, G, nqt),
        in_specs=[qspec, kspec, vspec, kspec, vspec, cspec, cspec,
                  pl.BlockSpec((TQ, 128), lambda b, g, i: (b * nqt + i, g)),
                  pl.BlockSpec((1, 3, R * TQ, TQ), lambda b, g, i: (g, 0, 0, 0)),
                  pl.BlockSpec((1, R, TQ, nb), lambda b, g, i: (g, 0, i, 0)),
                  pl.BlockSpec((nb, T), lambda b, g, i: (0, 0))],
        out_specs=qspec,
        out_shape=jax.ShapeDtypeStruct((B * T, G * R * HEAD_DIM), BF16),
        scratch_shapes=[pltpu.VMEM((T // SEL_KT, TQ, SEL_KT), F32), pltpu.VMEM((R, TQ, 1), F32),
                        pltpu.VMEM((R, TQ, 1), F32), pltpu.VMEM((R, TQ, HEAD_DIM), F32)],
        compiler_params=pltpu.CompilerParams(
            dimension_semantics=("parallel", "parallel", "arbitrary"),
            vmem_limit_bytes=V7X_VMEM_LIMIT_BYTES),
    )(q2, kv_sel, kv_sel, kv_win, kv_win, kc, vc, gl, toe, cb, e)


def toeplitz_bias(rel_table):
    d = jnp.arange(3)[:, None, None] * TQ + jnp.arange(TQ)[None, :, None] - jnp.arange(TQ)[None, None, :]
    b = rel_table[rel_bucket(d)].astype(F32)
    b = b.reshape(3, TQ, TQ, KV_GROUPS, HEADS_PER_GROUP).transpose(3, 0, 4, 1, 2)
    return b.reshape(KV_GROUPS, 3, HEADS_PER_GROUP * TQ, TQ)


NEW_PAD = TQ
PAGES_PER_STEP = 4
SEL_DONE = -3.0
SEL_PAD = -2.0


def _blockify_kernel(pt_ref, *refs):
    del pt_ref
    pages, pos_ref, o_ref = refs[:PAGES_PER_STEP], refs[PAGES_PER_STEP], refs[PAGES_PER_STEP + 1]
    bpp = PAGE_SIZE // BLOCK
    for pi, pg in enumerate(pages):
        for kv in range(2):
            pos = pos_ref[kv]
            for g in range(KV_GROUPS):
                c = (kv * KV_GROUPS + g) * HEAD_DIM
                for h in range(bpp):
                    slab = pg[0, h * BLOCK:(h + 1) * BLOCK, c:c + HEAD_DIM] + pos
                    o_ref[kv, 0, g, pi * bpp + h] = slab.astype(o_ref.dtype)


def blockify(pool, page_table, pos):
    B, n_pages = page_table.shape
    bpp = PAGE_SIZE // BLOCK
    W = 2 * KV_WIDTH
    assert n_pages % PAGES_PER_STEP == 0
    steps = n_pages // PAGES_PER_STEP
    page_specs = [pl.BlockSpec((1, PAGE_SIZE, W), functools.partial(
        lambda b, s, pt, k: (pt[b, s * PAGES_PER_STEP + k], 0, 0), k=k)) for k in range(PAGES_PER_STEP)]
    nbs = PAGES_PER_STEP * bpp
    grid_spec = pltpu.PrefetchScalarGridSpec(
        num_scalar_prefetch=1, grid=(B, steps),
        in_specs=page_specs + [pl.BlockSpec((2, BLOCK, HEAD_DIM), lambda b, s, pt: (0, 0, 0))],
        out_specs=pl.BlockSpec((2, 1, KV_GROUPS, nbs, BLOCK, HEAD_DIM), lambda b, s, pt: (0, b, 0, s, 0, 0)))
    return pl.pallas_call(
        _blockify_kernel, grid_spec=grid_spec, name="blockify",
        out_shape=jax.ShapeDtypeStruct((2, B, KV_GROUPS, n_pages * bpp, BLOCK, HEAD_DIM), BF16),
        compiler_params=pltpu.CompilerParams(dimension_semantics=("parallel", "arbitrary"),
                                             vmem_limit_bytes=V7X_VMEM_LIMIT_BYTES),
    )(page_table, *([pool] * PAGES_PER_STEP), pos)


def _compress_mlp_kernel(a_ref, w1_ref, b1_ref, w2_ref, b2_ref, o_ref):
    h = jnp.dot(a_ref[0], w1_ref[0], preferred_element_type=F32) + b1_ref[0]
    h = h * jax.nn.sigmoid(h)
    o_ref[0] = jnp.dot(h.astype(BF16), w2_ref[0], preferred_element_type=F32) + b2_ref[0]


def compress_mlp(a, w1, b1, w2, b2, tm=512):
    _, M, K = a.shape
    H = w1.shape[-1]
    tm = _pick(M, tm)
    return pl.pallas_call(
        _compress_mlp_kernel, grid=(2, M // tm), name="compress_mlp",
        in_specs=[pl.BlockSpec((1, tm, K), lambda k, m: (k, m, 0)),
                  pl.BlockSpec((1, K, H), lambda k, m: (k, 0, 0)),
                  pl.BlockSpec((1, 1, H), lambda k, m: (k, 0, 0)),
                  pl.BlockSpec((1, H, HEAD_DIM), lambda k, m: (k, 0, 0)),
                  pl.BlockSpec((1, 1, HEAD_DIM), lambda k, m: (k, 0, 0))],
        out_specs=pl.BlockSpec((1, tm, HEAD_DIM), lambda k, m: (k, m, 0)),
        out_shape=jax.ShapeDtypeStruct((2, M, HEAD_DIM), F32),
        compiler_params=pltpu.CompilerParams(dimension_semantics=("parallel", "parallel"),
                                             vmem_limit_bytes=V7X_VMEM_LIMIT_BYTES),
    )(a, w1, b1.reshape(2, 1, H), w2, b2.reshape(2, 1, HEAD_DIM))


def compress_paged(pool, page_table, p):
    a = blockify(pool, page_table, p["pos"])
    _, B, G, nb = a.shape[:4]
    out = compress_mlp(a.reshape(2, B * G * nb, BLOCK * HEAD_DIM), p["w1"], p["b1"], p["w2"], p["b2"])
    out = out.reshape(2, B, G, nb, HEAD_DIM)
    return out[0], out[1]


def _nsa_sample_kernel(pt_ref, *refs, S, n_pages, lw):
    del pt_ref
    PP = PAGES_PER_STEP
    pages = refs[:PP]
    (q_ref, kc_ref, vc_ref, winb_ref, nsel_ref, nwin_ref, gl_ref, cb_ref, sb_ref, e_ref,
     o_ref, qm_ref, selsc_ref, ocmp_ref, m_ref, l_ref, acc_ref) = refs[PP:]
    G, R = KV_GROUPS, HEADS_PER_GROUP
    RS = R * S
    nbc = kc_ref.shape[2]
    nbp = selsc_ref.shape[1] * 8
    n_blocks = -(-(n_pages * PAGE_SIZE + S) // BLOCK)
    past_len = n_pages * PAGE_SIZE
    steps = n_pages // PP
    s_id = pl.program_id(1)
    scale = HEAD_DIM ** -0.5
    qi = lax.broadcasted_iota(jnp.int32, (RS, 1), 0) % S
    jl = lax.broadcasted_iota(jnp.int32, (1, TQ), 1)

    def expand_rows(x8):
        return jnp.concatenate([x8] * R, axis=0)

    def attend(g, k_t, v_t, bias, mask):
        s = lax.dot_general(qm_ref[g], k_t.astype(BF16), _NT, preferred_element_type=F32) * scale + bias
        s = jnp.where(mask, s, NEG_INF)
        m_old = m_ref[g]
        m_new = jnp.maximum(m_old, jnp.max(s, axis=-1, keepdims=True))
        p = jnp.where(mask, jnp.exp(s - m_new), 0.0)
        alpha = jnp.exp(m_old - m_new)
        l_ref[g] = alpha * l_ref[g] + jnp.sum(p, axis=-1, keepdims=True)
        acc_ref[g] = alpha * acc_ref[g] + jnp.dot(p.astype(BF16), v_t.astype(BF16), preferred_element_type=F32)
        m_ref[g] = m_new

    def reset_state():
        m_ref[...] = jnp.full(m_ref.shape, NEG_INF, F32)
        l_ref[...] = jnp.zeros(l_ref.shape, F32)
        acc_ref[...] = jnp.zeros(acc_ref.shape, F32)

    @pl.when(s_id == 0)
    def _():
        reset_state()
        q = q_ref[...]
        jn = lax.broadcasted_iota(jnp.int32, (S, nbp), 1)
        jf = jn.astype(F32)
        qpos8 = past_len + lax.broadcasted_iota(jnp.int32, (S, 1), 0)
        cur = qpos8 // BLOCK
        forced = (jn == 0) | (jn == cur) | (jn == cur - 1)
        jc = lax.broadcasted_iota(jnp.int32, (RS, nbc), 1)
        valid = (past_len + qi - (jc * BLOCK + (BLOCK - 1))) >= 0
        for g in range(G):
            qg = jnp.concatenate([q[:, (g * R + r) * HEAD_DIM:(g * R + r + 1) * HEAD_DIM] for r in range(R)], axis=0)
            qm_ref[g] = qg.astype(BF16)
            sc = lax.dot_general(qm_ref[g], kc_ref[0, g].astype(BF16), _NT, preferred_element_type=F32) * scale
            sc = jnp.where(valid, sc + cb_ref[g], NEG_INF)
            mc = jnp.max(sc, axis=-1, keepdims=True)
            ec = jnp.where(valid, jnp.exp(sc - mc), 0.0)
            lc = jnp.sum(ec, axis=-1, keepdims=True)
            pc = ec / jnp.where(lc > 0.0, lc, 1.0)
            ocmp_ref[g] = jnp.dot(pc.astype(BF16), vc_ref[0, g].astype(BF16), preferred_element_type=F32)
            imp = pc[0:S]
            for r in range(1, R):
                imp = imp + pc[r * S:(r + 1) * S]
            imp = jnp.concatenate([imp, jnp.zeros((S, nbp - nbc), F32)], axis=1)
            work = jnp.where(jn > cur, IMP_FUTURE, jnp.where(forced, IMP_FORCED, imp))
            work = jnp.where(jn >= n_blocks, SEL_PAD, work)
            sel = jnp.zeros((S, nbp), F32)
            for _ in range(min(N_SELECT, n_blocks)):
                mx = jnp.max(work, axis=-1, keepdims=True)
                first = jnp.min(jnp.where(work == mx, jf, float(nbp)), axis=-1, keepdims=True)
                hit = jf == first
                sel = jnp.where(hit, 1.0, sel)
                work = jnp.where(hit, SEL_DONE, work)
            for st in range(nbp // 8):
                selsc_ref[g, st] = jnp.concatenate(
                    [sel[:, st * 8:(st + 1) * 8], jnp.zeros((S, 128 - 8), F32)], axis=1)

    for g in range(G):
        selm = jnp.dot(selsc_ref[g, s_id].astype(BF16), e_ref[...], preferred_element_type=F32)
        selm = expand_rows(selm) > 0.5
        for pi in range(PP):
            tbl = jnp.minimum(n_pages - (s_id * PP + pi), 2)
            pg = pages[pi]
            attend(g, pg[0, :, g * HEAD_DIM:(g + 1) * HEAD_DIM],
                   pg[0, :, (G + g) * HEAD_DIM:(G + g + 1) * HEAD_DIM],
                   sb_ref[g, tbl], selm[:, pi * PAGE_SIZE:(pi + 1) * PAGE_SIZE])

    @pl.when(s_id == steps - 1)
    def _():
        dist_new = qi - jl
        o_sel, o_win = [], []
        for g in range(G):
            selm = jnp.dot(selsc_ref[g, steps].astype(BF16), e_ref[...], preferred_element_type=F32)
            mask = (expand_rows(selm[:, :TQ]) > 0.5) & (dist_new >= 0)
            attend(g, nsel_ref[0, :, g * HEAD_DIM:(g + 1) * HEAD_DIM],
                   nsel_ref[0, :, (G + g) * HEAD_DIM:(G + g + 1) * HEAD_DIM], sb_ref[g, 0], mask)
            o_sel.append(acc_ref[g] / l_ref[g])
        reset_state()
        for g in range(G):
            for t in range(lw // TQ):
                base = lw - t * TQ
                dist = base + qi - jl
                attend(g, winb_ref[0, t * TQ:(t + 1) * TQ, g * HEAD_DIM:(g + 1) * HEAD_DIM],
                       winb_ref[0, t * TQ:(t + 1) * TQ, (G + g) * HEAD_DIM:(G + g + 1) * HEAD_DIM],
                       sb_ref[g, min(base // TQ, 2)], (dist >= 0) & (dist <= WINDOW))
            attend(g, nwin_ref[0, :, g * HEAD_DIM:(g + 1) * HEAD_DIM],
                   nwin_ref[0, :, (G + g) * HEAD_DIM:(G + g + 1) * HEAD_DIM], sb_ref[g, 0],
                   (dist_new >= 0) & (dist_new <= WINDOW))
            o_win.append(acc_ref[g] / l_ref[g])
        gates = jax.nn.sigmoid(gl_ref[...])
        for g in range(G):
            oc = ocmp_ref[g]
            for r in range(R):
                c = g * 128 + r
                rows = slice(r * S, (r + 1) * S)
                o = (gates[:, c:c + 1] * oc[rows] + gates[:, c + R:c + R + 1] * o_sel[g][rows]
                     + gates[:, c + 2 * R:c + 2 * R + 1] * o_win[g][rows])
                o_ref[:, (g * R + r) * HEAD_DIM:(g * R + r + 1) * HEAD_DIM] = o


def sample_bias_tables(rel_table, S, past_len, nbc):
    G, R = KV_GROUPS, HEADS_PER_GROUP
    d = jnp.arange(3)[:, None, None] * TQ + jnp.arange(S)[None, :, None] - jnp.arange(TQ)[None, None, :]
    b = rel_table[rel_bucket(d)].astype(F32).reshape(3, S, TQ, G, R)
    sb = b.transpose(3, 0, 4, 1, 2).reshape(G, 3, R * S, TQ)
    dc = past_len + jnp.arange(S)[:, None] - (jnp.arange(nbc) * BLOCK + (BLOCK - 1))[None, :]
    cb = head_bias(rel_table, dc).reshape(G, R * S, nbc)
    return sb, cb


def nsa_sample_attention(q, kc, vc, pool_sel, page_table, win_buf, new_sel, new_win, gl, rel_table):
    G, R = KV_GROUPS, HEADS_PER_GROUP
    B, n_pages = page_table.shape
    S = new_sel.shape[1]
    lw = win_buf.shape[1]
    W = 2 * KV_WIDTH
    nbc = kc.shape[2]
    PP = PAGES_PER_STEP
    steps = n_pages // PP
    assert n_pages % PP == 0 and lw % TQ == 0 and S <= BLOCK and PP * PAGE_SIZE // BLOCK == 8
    sb, cb = sample_bias_tables(rel_table, S, n_pages * PAGE_SIZE, nbc)
    pad = ((0, 0), (0, NEW_PAD - S), (0, 0))
    new_sel, new_win = jnp.pad(new_sel, pad), jnp.pad(new_win, pad)
    expand = np.zeros((128, PP * PAGE_SIZE), np.float32)
    expand[np.arange(PP * PAGE_SIZE) // BLOCK, np.arange(PP * PAGE_SIZE)] = 1.0
    e = jnp.asarray(expand, BF16)
    page_specs = [pl.BlockSpec((1, PAGE_SIZE, W), functools.partial(
        lambda b, s, pt, k: (pt[b, s * PP + k], 0, 0), k=k)) for k in range(PP)]
    row = lambda shape: pl.BlockSpec(shape, lambda b, s, pt: (b,) + (0,) * (len(shape) - 1))
    const = lambda shape: pl.BlockSpec(shape, lambda b, s, pt: (0,) * len(shape))
    grid_spec = pltpu.PrefetchScalarGridSpec(
        num_scalar_prefetch=1, grid=(B, steps),
        in_specs=page_specs + [row((S, G * R * HEAD_DIM)), row((1, G, nbc, HEAD_DIM)), row((1, G, nbc, HEAD_DIM)),
                               row((1, lw, W)), row((1, NEW_PAD, W)), row((1, NEW_PAD, W)), row((S, G * 128)),
                               const((G, R * S, nbc)), const((G, 3, R * S, TQ)), const((128, PP * PAGE_SIZE))],
        out_specs=row((S, G * R * HEAD_DIM)),
        scratch_shapes=[pltpu.VMEM((G, R * S, HEAD_DIM), BF16), pltpu.VMEM((G, steps + 1, S, 128), F32),
                        pltpu.VMEM((G, R * S, HEAD_DIM), F32), pltpu.VMEM((G, R * S, 1), F32),
                        pltpu.VMEM((G, R * S, 1), F32), pltpu.VMEM((G, R * S, HEAD_DIM), F32)])
    return pl.pallas_call(
        functools.partial(_nsa_sample_kernel, S=S, n_pages=n_pages, lw=lw), grid_spec=grid_spec,
        name="nsa_sample_attn",
        out_shape=jax.ShapeDtypeStruct((B * S, G * R * HEAD_DIM), F32),
        compiler_params=pltpu.CompilerParams(dimension_semantics=("parallel", "arbitrary"),
                                             vmem_limit_bytes=V7X_VMEM_LIMIT_BYTES),
    )(page_table, *([pool_sel] * PP), q, kc, vc, win_buf, new_sel, new_win, gl, cb, sb, e)


def rms_norm(x, g):
    x32 = x.astype(F32)
    return x32 * lax.rsqrt(jnp.mean(x32 * x32, axis=-1, keepdims=True) + EPS) * g


def layer_norm(x, g, b):
    mu = jnp.mean(x, axis=-1, keepdims=True)
    var = jnp.mean(jnp.square(x - mu), axis=-1, keepdims=True)
    return (x - mu) * lax.rsqrt(var + EPS) * g + b


def mm3(x, w, **kw):
    B, S, K = x.shape
    res = kw.pop("res", None)
    if res is not None:
        res = res.reshape(B * S, -1)
    y = matmul(x.reshape(B * S, K).astype(BF16), w, res=res, **kw)
    return y.reshape(B, S, -1)


def conv_module(h, buf, p):
    B, S, D = h.shape
    glu = matmul_glu(h.reshape(B * S, D).astype(BF16), p["wa"], p["wg"], p["ba"], p["bg"]).reshape(B, S, D)
    ext = jnp.concatenate([buf, glu], axis=1)
    conv = lax.conv_general_dilated(
        ext, p["dw_w"][:, None, :], window_strides=(1,), padding='VALID',
        dimension_numbers=('NWC', 'WIO', 'NWC'), feature_group_count=D_MODEL,
        precision=lax.Precision.HIGHEST) + p["dw_b"]
    z = jax.nn.silu(layer_norm(conv, p["ln_g"], p["ln_b"]))
    return z, ext[:, -CONV_BUF:]


def rel_bucket(dist):
    n = jnp.maximum(dist, 0)
    max_exact = REL_BUCKETS // 2
    nf = jnp.maximum(n, 1).astype(jnp.float32)
    large = max_exact + (jnp.log(nf / max_exact) / math.log(REL_MAX_DIST / max_exact)
                         * (REL_BUCKETS - max_exact)).astype(jnp.int32)
    large = jnp.minimum(large, REL_BUCKETS - 1)
    return jnp.where(n < max_exact, n, large)


def head_bias(rel_table, dist):
    b = rel_table[rel_bucket(dist)].astype(jnp.float32)
    return b.reshape(dist.shape + (KV_GROUPS, HEADS_PER_GROUP)).transpose(2, 3, 0, 1)


def nsa_project(h, p, q_dtype):
    B, S, D = h.shape
    hb = h.reshape(B * S, D).astype(BF16)
    q2 = matmul(hb, p["w_q"], out_dtype=q_dtype)
    kvs = [matmul(hb, p["w_kv"][i]) for i in range(N_BRANCHES)]
    gl = matmul(hb, p["w_gate"])
    return q2, kvs, gl


def nsa_prompt(h, p, rel_table):
    B, T = h.shape[:2]
    W = 2 * KV_WIDTH
    q2, kvs, gl = nsa_project(h, p, BF16)
    ident = jnp.arange(B * T // PAGE_SIZE, dtype=jnp.int32).reshape(B, T // PAGE_SIZE)
    kc, vc = compress_paged(kvs[0].reshape(B * T // PAGE_SIZE, PAGE_SIZE, W), ident, p)
    nb = T // BLOCK
    dist_c = jnp.arange(T)[:, None] - (jnp.arange(nb) * BLOCK + (BLOCK - 1))[None, :]
    o = nsa_prompt_attention(q2, kvs[1], kvs[2], kc, vc, gl,
                             toeplitz_bias(rel_table), head_bias(rel_table, dist_c), B, T)
    kv_cmp, kv_sel, kv_win = [kv.reshape(B, T, 2, KV_GROUPS, HEAD_DIM) for kv in kvs]
    return o.reshape(B, T, -1), kv_cmp, kv_sel, kv_win[:, T - min(WINDOW, T):]


def nsa_sample(h, pool_cmp, pool_sel, win_buf, page_table, p, rel_table):
    B, S = h.shape[:2]
    W = 2 * KV_WIDTH
    n_pages = page_table.shape[1]
    q2, kvs, gl = nsa_project(h, p, F32)
    assert (n_pages * PAGE_SIZE + S) // BLOCK == n_pages * PAGE_SIZE // BLOCK
    kc, vc = compress_paged(pool_cmp.reshape(-1, PAGE_SIZE, W), page_table, p)
    lw = win_buf.shape[1]
    o = nsa_sample_attention(q2, kc, vc, pool_sel.reshape(-1, PAGE_SIZE, W), page_table,
                             win_buf.reshape(B, lw, W), kvs[1].reshape(B, S, W), kvs[2].reshape(B, S, W),
                             gl, rel_table)
    kv_cmp, kv_sel, kv_win = [kv.reshape(B, S, 2, KV_GROUPS, HEAD_DIM) for kv in kvs]
    all_win = jnp.concatenate([win_buf, kv_win], axis=1)
    return o.reshape(B, S, -1), kv_cmp, kv_sel, all_win[:, lw + S - min(WINDOW, lw + S):]


def ffn(x, g, w_up, w_down):
    B, S, D = x.shape
    h = rms_norm(x, g).reshape(B * S, D).astype(BF16)
    hid = matmul(h, w_up, act="relu2", out_dtype=BF16)
    return matmul(hid, w_down, res=x.reshape(B * S, D), tk=2048).reshape(B, S, D)


def kernel(x_prompt, x_sample, state_conv, cache_kv_cmp, cache_kv_sel, state_kv_win, page_table, norm_mix_g, norm_ffn_g, norm_final_g, rel_bias_table, conv_w_in, conv_b_in, conv_dw_w, conv_dw_b, conv_ln_g, conv_ln_b, conv_w_out, conv_b_out, nsa_w_in, nsa_cmp_pos, nsa_cmp_w1, nsa_cmp_b1, nsa_cmp_w2, nsa_cmp_b2, nsa_w_out, ffn_w_up, ffn_w_down):
    xp, xs = x_prompt, x_sample
    qd = N_HEADS * HEAD_DIM
    kvd = 2 * KV_WIDTH

    cp = dict(wa=conv_w_in[0][:, :D_MODEL].astype(BF16), wg=conv_w_in[0][:, D_MODEL:].astype(BF16),
              ba=conv_b_in[0][:D_MODEL], bg=conv_b_in[0][D_MODEL:], dw_w=conv_dw_w[0], dw_b=conv_dw_b[0],
              ln_g=conv_ln_g[0], ln_b=conv_ln_b[0])
    w_out0 = conv_w_out[0].astype(BF16)
    zero_buf = jnp.zeros((xp.shape[0], CONV_BUF, D_MODEL), F32)
    zp, conv_p = conv_module(rms_norm(xp, norm_mix_g[0]), zero_buf, cp)
    zs, conv_s = conv_module(rms_norm(xs, norm_mix_g[0]), state_conv[0], cp)
    xp = mm3(zp, w_out0, bias=conv_b_out[0], res=xp)
    xs = mm3(zs, w_out0, bias=conv_b_out[0], res=xs)
    w_up, w_dn = ffn_w_up[0].astype(BF16), ffn_w_down[0].astype(BF16)
    xp = ffn(xp, norm_ffn_g[0], w_up, w_dn)
    xs = ffn(xs, norm_ffn_g[0], w_up, w_dn)

    w_in = nsa_w_in[0]
    nbr = N_BRANCHES * HEADS_PER_GROUP
    gate_w = w_in[:, qd + N_BRANCHES * kvd:].reshape(D_MODEL, N_BRANCHES, KV_GROUPS, HEADS_PER_GROUP)
    gate_w = gate_w.transpose(0, 2, 1, 3).reshape(D_MODEL, KV_GROUPS, nbr)
    gate_w = jnp.pad(gate_w, ((0, 0), (0, 0), (0, 128 - nbr))).reshape(D_MODEL, KV_GROUPS * 128)
    npar = dict(w_q=w_in[:, :qd].astype(BF16),
                w_kv=[w_in[:, qd + i * kvd: qd + (i + 1) * kvd].astype(BF16) for i in range(N_BRANCHES)],
                w_gate=gate_w.astype(BF16), pos=nsa_cmp_pos[0], w1=nsa_cmp_w1[0].astype(BF16),
                b1=nsa_cmp_b1[0], w2=nsa_cmp_w2[0].astype(BF16), b2=nsa_cmp_b2[0])
    w_out1 = nsa_w_out[0].astype(BF16)
    op, kcp, ksp, kwp = nsa_prompt(rms_norm(xp, norm_mix_g[1]), npar, rel_bias_table)
    os_, kcs, kss, kws = nsa_sample(rms_norm(xs, norm_mix_g[1]), cache_kv_cmp[0], cache_kv_sel[0],
                                    state_kv_win[0], page_table, npar, rel_bias_table)
    xp = mm3(op, w_out1, res=xp)
    xs = mm3(os_, w_out1, res=xs)
    w_up, w_dn = ffn_w_up[1].astype(BF16), ffn_w_down[1].astype(BF16)
    xp = ffn(xp, norm_ffn_g[1], w_up, w_dn)
    xs = ffn(xs, norm_ffn_g[1], w_up, w_dn)

    y_prompt = rms_norm(xp, norm_final_g)
    y_sample = rms_norm(xs, norm_final_g)
    return (y_prompt, y_sample, conv_p[None], conv_s[None], kcp[None], kcs[None],
            ksp[None], kss[None], kwp[None], kws[None])
```

```python
import functools
import math

import jax
import jax.numpy as jnp
import numpy as np
from jax import lax
from jax.experimental import pallas as pl
from jax.experimental.pallas import tpu as pltpu

D_MODEL = 4096
CONV_WIDTH = 31
CONV_BUF = CONV_WIDTH - 1
N_HEADS = 32
HEAD_DIM = D_MODEL // N_HEADS
KV_GROUPS = 4
HEADS_PER_GROUP = N_HEADS // KV_GROUPS
KV_WIDTH = KV_GROUPS * HEAD_DIM
BLOCK = 64
N_SELECT = 16
WINDOW = 512
WIN_Q_BLOCK = 128
SEL_Q_CHUNK = 16
N_BRANCHES = 3
PAGE_SIZE = 128
REL_BUCKETS = 32
REL_MAX_DIST = 128
EPS = 1e-6
NEG_INF = -1e30

V7X_VMEM_LIMIT_BYTES = 56 * 1024 * 1024
BF16 = jnp.bfloat16
F32 = jnp.float32


def _apply_act(y, act):
    if act == "relu2":
        return jnp.square(jnp.maximum(y, 0.0))
    if act == "silu":
        return y * jax.nn.sigmoid(y)
    assert act is None
    return y


def _mm_kernel(*refs, act, has_bias, has_res, nk):
    x_ref, w_ref = refs[0], refs[1]
    pos = 2
    b_ref = r_ref = None
    if has_bias:
        b_ref = refs[pos]
        pos += 1
    if has_res:
        r_ref = refs[pos]
        pos += 1
    o_ref = refs[pos]
    acc_ref = refs[pos + 1] if nk > 1 else None

    def epilogue(y):
        if has_bias:
            y = y + b_ref[...]
        y = _apply_act(y, act)
        if has_res:
            y = y + r_ref[...]
        o_ref[...] = y.astype(o_ref.dtype)

    part = jnp.dot(x_ref[...], w_ref[...], preferred_element_type=F32)
    if nk == 1:
        epilogue(part)
        return
    k = pl.program_id(2)

    @pl.when(k == 0)
    def _():
        acc_ref[...] = part

    @pl.when(jnp.logical_and(k > 0, k < nk - 1))
    def _():
        acc_ref[...] += part

    @pl.when(k == nk - 1)
    def _():
        epilogue(acc_ref[...] + part)


def _pick(dim, pref):
    t = min(dim, pref)
    assert dim % t == 0, (dim, pref)
    return t


def matmul(x, w, bias=None, res=None, act=None, out_dtype=F32, tm=1024, tn=1024, tk=4096):
    M, K = x.shape
    K2, N = w.shape
    assert K == K2
    tm, tn, tk = _pick(M, tm), _pick(N, tn), _pick(K, tk)
    nk = K // tk
    in_specs = [pl.BlockSpec((tm, tk), lambda m, n, k: (m, k)),
                pl.BlockSpec((tk, tn), lambda m, n, k: (k, n))]
    args = [x, w]
    if bias is not None:
        in_specs.append(pl.BlockSpec((1, tn), lambda m, n, k: (0, n)))
        args.append(bias.reshape(1, N).astype(F32))
    if res is not None:
        in_specs.append(pl.BlockSpec((tm, tn), lambda m, n, k: (m, n)))
        args.append(res)
    scratch = [pltpu.VMEM((tm, tn), F32)] if nk > 1 else []
    return pl.pallas_call(
        functools.partial(_mm_kernel, act=act, has_bias=bias is not None,
                          has_res=res is not None, nk=nk),
        grid=(M // tm, N // tn, nk),
        in_specs=in_specs,
        out_specs=pl.BlockSpec((tm, tn), lambda m, n, k: (m, n)),
        out_shape=jax.ShapeDtypeStruct((M, N), out_dtype),
        scratch_shapes=scratch,
        compiler_params=pltpu.CompilerParams(
            dimension_semantics=("parallel", "parallel", "arbitrary"),
            vmem_limit_bytes=V7X_VMEM_LIMIT_BYTES),
    )(*args)


def _glu_kernel(x_ref, wa_ref, wg_ref, ba_ref, bg_ref, o_ref):
    x = x_ref[...]
    a = jnp.dot(x, wa_ref[...], preferred_element_type=F32) + ba_ref[...]
    g = jnp.dot(x, wg_ref[...], preferred_element_type=F32) + bg_ref[...]
    o_ref[...] = a * jax.nn.sigmoid(g)


def matmul_glu(x, wa, wg, ba, bg, tm=1024, tn=512):
    M, K = x.shape
    N = wa.shape[1]
    tm, tn = _pick(M, tm), _pick(N, tn)
    wspec = pl.BlockSpec((K, tn), lambda m, n: (0, n))
    bspec = pl.BlockSpec((1, tn), lambda m, n: (0, n))
    return pl.pallas_call(
        _glu_kernel,
        grid=(M // tm, N // tn),
        in_specs=[pl.BlockSpec((tm, K), lambda m, n: (m, 0)), wspec, wspec, bspec, bspec],
        out_specs=pl.BlockSpec((tm, tn), lambda m, n: (m, n)),
        out_shape=jax.ShapeDtypeStruct((M, N), F32),
        compiler_params=pltpu.CompilerParams(
            dimension_semantics=("parallel", "parallel"),
            vmem_limit_bytes=V7X_VMEM_LIMIT_BYTES),
    )(x, wa, wg, ba.reshape(1, N), bg.reshape(1, N))


TQ = 128
SEL_KT = 4 * TQ
IMP_FORCED = 16.0
IMP_FUTURE = -1.0
_NT = (((1,), (1,)), ((), ()))


def _nsa_prompt_kernel(q_ref, ks_ref, vs_ref, kw_ref, vw_ref, kc_ref, vc_ref, gl_ref, toe_ref, cb_ref,
                       e_ref, o_ref, selm_ref, m_ref, l_ref, acc_ref, *, T):
    R = HEADS_PER_GROUP
    nb = T // BLOCK
    nqt = T // TQ
    i = pl.program_id(2)
    scale = HEAD_DIM ** -0.5
    qb = q_ref[...]
    qm = jnp.concatenate([qb[:, r * HEAD_DIM:(r + 1) * HEAD_DIM] for r in range(R)], axis=0)
    t_loc = lax.broadcasted_iota(jnp.int32, (TQ, 1), 0)
    q_pos = i * TQ + t_loc

    kc = kc_ref[0, 0].astype(BF16)
    vc = vc_ref[0, 0].astype(BF16)
    sc = lax.dot_general(qm, kc, _NT, preferred_element_type=F32) * scale
    sc = sc.reshape(R, TQ, nb) + cb_ref[0]
    jn = lax.broadcasted_iota(jnp.int32, (TQ, nb), 1)
    valid = (q_pos - (jn * BLOCK + (BLOCK - 1))) >= 0
    sc = jnp.where(valid[None], sc, NEG_INF)
    mc = jnp.max(sc, axis=-1, keepdims=True)
    ec = jnp.where(valid[None], jnp.exp(sc - mc), 0.0)
    lc = jnp.sum(ec, axis=-1, keepdims=True)
    pc = ec / jnp.where(lc > 0.0, lc, 1.0)
    o_cmp = jnp.dot(pc.reshape(R * TQ, nb).astype(BF16), vc, preferred_element_type=F32).reshape(R, TQ, HEAD_DIM)
    imp = pc[0]
    for r in range(1, R):
        imp = imp + pc[r]

    cur = q_pos // BLOCK
    forced = (jn == 0) | (jn == cur) | (jn == cur - 1)
    impp = jnp.where(jn > cur, IMP_FUTURE, jnp.where(forced, IMP_FORCED, imp))
    rank = jnp.zeros((TQ, nb), jnp.int32)
    for jp in range(nb):
        col = impp[:, jp:jp + 1]
        beats = (col > impp) | ((col == impp) & (jn > jp))
        rank = rank + beats.astype(jnp.int32)
    sel = (rank < min(N_SELECT, nb)).astype(BF16)
    selm = jnp.dot(sel, e_ref[...], preferred_element_type=F32)
    for st in range(T // SEL_KT):
        selm_ref[st] = selm[:, st * SEL_KT:(st + 1) * SEL_KT]

    def scores(k_t, kt0, n_sub):
        s = lax.dot_general(qm, k_t, _NT, preferred_element_type=F32) * scale
        bias = jnp.concatenate([toe_ref[0, jnp.clip(i - kt0 - c, 0, 2)] for c in range(n_sub)], axis=1)
        j_loc = lax.broadcasted_iota(jnp.int32, (1, n_sub * TQ), 1)
        dist = (i - kt0) * TQ + t_loc - j_loc
        return (s + bias).reshape(R, TQ, n_sub * TQ), dist

    m_ref[...] = jnp.full(m_ref.shape, NEG_INF, F32)
    l_ref[...] = jnp.zeros(l_ref.shape, F32)
    acc_ref[...] = jnp.zeros(acc_ref.shape, F32)
    sub = SEL_KT // TQ

    def sel_body(st, carry):
        ks = pl.multiple_of(st * SEL_KT, SEL_KT)
        s, dist = scores(ks_ref[pl.ds(ks, SEL_KT), :].astype(BF16), st * sub, sub)
        mask = (dist >= 0) & (selm_ref[st] > 0.5)
        s = jnp.where(mask[None], s, NEG_INF)
        m_old = m_ref[...]
        m_new = jnp.maximum(m_old, jnp.max(s, axis=-1, keepdims=True))
        p = jnp.exp(s - m_new)
        alpha = jnp.exp(m_old - m_new)
        l_ref[...] = alpha * l_ref[...] + jnp.sum(p, axis=-1, keepdims=True)
        pv = jnp.dot(p.reshape(R * TQ, SEL_KT).astype(BF16), vs_ref[pl.ds(ks, SEL_KT), :].astype(BF16),
                     preferred_element_type=F32)
        acc_ref[...] = alpha * acc_ref[...] + pv.reshape(R, TQ, HEAD_DIM)
        m_ref[...] = m_new
        return carry

    lax.fori_loop(0, i // sub + 1, sel_body, 0)
    o_sel = acc_ref[...] / l_ref[...]

    n_win = WINDOW // TQ + 1
    kt0 = jnp.clip(i - WINDOW // TQ, 0, nqt - n_win)
    ws = pl.multiple_of(kt0 * TQ, TQ)
    s, dist = scores(kw_ref[pl.ds(ws, n_win * TQ), :].astype(BF16), kt0, n_win)
    s = jnp.where(((dist >= 0) & (dist <= WINDOW))[None], s, NEG_INF)
    p = jnp.exp(s - jnp.max(s, axis=-1, keepdims=True))
    lw = jnp.sum(p, axis=-1, keepdims=True)
    pv = jnp.dot(p.reshape(R * TQ, n_win * TQ).astype(BF16), vw_ref[pl.ds(ws, n_win * TQ), :].astype(BF16),
                 preferred_element_type=F32)
    o_win = pv.reshape(R, TQ, HEAD_DIM) / lw

    gates = jax.nn.sigmoid(gl_ref[...])
    for r in range(R):
        o = (gates[:, r:r + 1] * o_cmp[r] + gates[:, R + r:R + r + 1] * o_sel[r]
             + gates[:, 2 * R + r:2 * R + r + 1] * o_win[r])
        o_ref[:, r * HEAD_DIM:(r + 1) * HEAD_DIM] = o.astype(o_ref.dtype)


def nsa_prompt_attention(q2, kv_sel, kv_win, kc, vc, gl, toe, cb, B, T):
    G, R = KV_GROUPS, HEADS_PER_GROUP
    nb, nqt = T // BLOCK, T // TQ
    assert T % SEL_KT == 0 and nqt >= WINDOW // TQ + 1
    expand = (np.arange(T)[None, :] // BLOCK == np.arange(nb)[:, None])
    e = jnp.asarray(expand, BF16)
    qspec = pl.BlockSpec((TQ, R * HEAD_DIM), lambda b, g, i: (b * nqt + i, g))
    kspec = pl.BlockSpec((T, HEAD_DIM), lambda b, g, i: (b, g))
    vspec = pl.BlockSpec((T, HEAD_DIM), lambda b, g, i: (b, G + g))
    cspec = pl.BlockSpec((1, 1, nb, HEAD_DIM), lambda b, g, i: (b, g, 0, 0))
    return pl.pallas_call(
        functools.partial(_nsa_prompt_kernel, T=T), name="nsa_prompt_attn",
        grid=(B, G, nqt),
        in_specs=[qspec, kspec, vspec, kspec, vspec, cspec, cspec,
                  pl.BlockSpec((TQ, 128), lambda b, g, i: (b * nqt + i, g)),
                  pl.BlockSpec((1, 3, R * TQ, TQ), lambda b, g, i: (g, 0, 0, 0)),
                  pl.BlockSpec((1, R, TQ, nb), lambda b, g, i: (g, 0, i, 0)),
                  pl.BlockSpec((nb, T), lambda b, g, i: (0, 0))],
        out_specs=qspec,
        out_shape=jax.ShapeDtypeStruct((B * T, G * R * HEAD_DIM), BF16),
        scratch_shapes=[pltpu.VMEM((T // SEL_KT, TQ, SEL_KT), F32), pltpu.VMEM((R, TQ, 1), F32),
                        pltpu.VMEM((R, TQ, 1), F32), pltpu.VMEM((R, TQ, HEAD_DIM), F32)],
        compiler_params=pltpu.CompilerParams(
            dimension_semantics=("parallel", "parallel", "arbitrary"),
            vmem_limit_bytes=V7X_VMEM_LIMIT_BYTES),
    )(q2, kv_sel, kv_sel, kv_win, kv_win, kc, vc, gl, toe, cb, e)


def toeplitz_bias(rel_table):
    d = jnp.arange(3)[:, None, None] * TQ + jnp.arange(TQ)[None, :, None] - jnp.arange(TQ)[None, None, :]
    b = rel_table[rel_bucket(d)].astype(F32)
    b = b.reshape(3, TQ, TQ, KV_GROUPS, HEADS_PER_GROUP).transpose(3, 0, 4, 1, 2)
    return b.reshape(KV_GROUPS, 3, HEADS_PER_GROUP * TQ, TQ)


NEW_PAD = TQ
PAGES_PER_STEP = 4
SEL_DONE = -3.0
SEL_PAD = -2.0


BLOCKIFY_PAGES = 8
KV_ROWS = 2 * KV_GROUPS


def _blockify_kernel(pt_ref, *refs, tall):
    del pt_ref
    NP = BLOCKIFY_PAGES
    pages, pos_ref, o_ref, xs_ref = refs[:NP], refs[NP], refs[NP + 1], refs[NP + 2]
    rows = pages[0].shape[0]
    for pi, pg in enumerate(pages):
        if tall:
            xs_ref[pi * rows:(pi + 1) * rows, :] = pg[...]
        else:
            for c in range(KV_ROWS):
                xs_ref[c, pi * rows:(pi + 1) * rows, :] = pg[:, c * HEAD_DIM:(c + 1) * HEAD_DIM]
    nblk = NP * PAGE_SIZE // BLOCK
    for kv in range(2):
        for g in range(KV_GROUPS):
            c = kv * KV_GROUPS + g
            for j in range(BLOCK):
                if tall:
                    v = xs_ref[pl.ds(j * KV_ROWS + c, nblk, stride=BLOCK * KV_ROWS), :]
                else:
                    v = xs_ref[c, pl.ds(j, nblk, stride=BLOCK), :]
                o_ref[kv, 0, g, :, j * HEAD_DIM:(j + 1) * HEAD_DIM] = (v + pos_ref[kv, j:j + 1, :]).astype(o_ref.dtype)


def blockify(pool, page_table, pos, tall):
    B, n_pages = page_table.shape
    NP = BLOCKIFY_PAGES
    assert n_pages % NP == 0
    blk = (PAGE_SIZE * KV_ROWS, HEAD_DIM) if tall else (PAGE_SIZE, KV_ROWS * HEAD_DIM)
    page_specs = [pl.BlockSpec(blk, functools.partial(lambda b, s, pt, k: (pt[b, s * NP + k], 0), k=k))
                  for k in range(NP)]
    nbs = NP * PAGE_SIZE // BLOCK
    grid_spec = pltpu.PrefetchScalarGridSpec(
        num_scalar_prefetch=1, grid=(B, n_pages // NP),
        in_specs=page_specs + [pl.BlockSpec((2, BLOCK, HEAD_DIM), lambda b, s, pt: (0, 0, 0))],
        out_specs=pl.BlockSpec((2, 1, KV_GROUPS, nbs, BLOCK * HEAD_DIM), lambda b, s, pt: (0, b, 0, s, 0)),
        scratch_shapes=[pltpu.VMEM((NP * PAGE_SIZE * KV_ROWS, HEAD_DIM) if tall else
                                   (KV_ROWS, NP * PAGE_SIZE, HEAD_DIM), F32)])
    return pl.pallas_call(
        functools.partial(_blockify_kernel, tall=tall), grid_spec=grid_spec, name="blockify",
        out_shape=jax.ShapeDtypeStruct((2, B, KV_GROUPS, n_pages * PAGE_SIZE // BLOCK, BLOCK * HEAD_DIM), BF16),
        compiler_params=pltpu.CompilerParams(dimension_semantics=("parallel", "arbitrary"),
                                             vmem_limit_bytes=V7X_VMEM_LIMIT_BYTES),
    )(page_table, *([pool] * NP), pos)


def _compress_mlp_kernel(a_ref, w1_ref, b1_ref, w2_ref, b2_ref, o_ref):
    h = jnp.dot(a_ref[0], w1_ref[0], preferred_element_type=F32) + b1_ref[0]
    h = h * jax.nn.sigmoid(h)
    o_ref[0] = jnp.dot(h.astype(BF16), w2_ref[0], preferred_element_type=F32) + b2_ref[0]


def compress_mlp(a, w1, b1, w2, b2, tm=512):
    _, M, K = a.shape
    H = w1.shape[-1]
    tm = _pick(M, tm)
    return pl.pallas_call(
        _compress_mlp_kernel, grid=(2, M // tm), name="compress_mlp",
        in_specs=[pl.BlockSpec((1, tm, K), lambda k, m: (k, m, 0)),
                  pl.BlockSpec((1, K, H), lambda k, m: (k, 0, 0)),
                  pl.BlockSpec((1, 1, H), lambda k, m: (k, 0, 0)),
                  pl.BlockSpec((1, H, HEAD_DIM), lambda k, m: (k, 0, 0)),
                  pl.BlockSpec((1, 1, HEAD_DIM), lambda k, m: (k, 0, 0))],
        out_specs=pl.BlockSpec((1, tm, HEAD_DIM), lambda k, m: (k, m, 0)),
        out_shape=jax.ShapeDtypeStruct((2, M, HEAD_DIM), F32),
        compiler_params=pltpu.CompilerParams(dimension_semantics=("parallel", "parallel"),
                                             vmem_limit_bytes=V7X_VMEM_LIMIT_BYTES),
    )(a, w1, b1.reshape(2, 1, H), w2, b2.reshape(2, 1, HEAD_DIM))


def compress_paged(pool, page_table, p, tall):
    a = blockify(pool, page_table, p["pos"], tall)
    _, B, G, nb = a.shape[:4]
    out = compress_mlp(a.reshape(2, B * G * nb, BLOCK * HEAD_DIM), p["w1"], p["b1"], p["w2"], p["b2"])
    out = out.reshape(2, B, G, nb, HEAD_DIM)
    return out[0], out[1]


def _nsa_sample_kernel(pt_ref, *refs, S, n_pages, lw):
    del pt_ref
    PP = PAGES_PER_STEP
    pages = refs[:PP]
    (q_ref, kc_ref, vc_ref, winb_ref, nsel_ref, nwin_ref, gl_ref, cb_ref, sb_ref, e_ref,
     o_ref, qm_ref, selsc_ref, ocmp_ref, m_ref, l_ref, acc_ref) = refs[PP:]
    G, R = KV_GROUPS, HEADS_PER_GROUP
    RS = R * S
    nbc = kc_ref.shape[2]
    nbp = selsc_ref.shape[1] * 8
    n_blocks = -(-(n_pages * PAGE_SIZE + S) // BLOCK)
    past_len = n_pages * PAGE_SIZE
    steps = n_pages // PP
    s_id = pl.program_id(1)
    scale = HEAD_DIM ** -0.5
    qi = lax.broadcasted_iota(jnp.int32, (RS, 1), 0) % S
    jl = lax.broadcasted_iota(jnp.int32, (1, TQ), 1)

    def expand_rows(x8):
        return jnp.concatenate([x8] * R, axis=0)

    def attend(g, k_t, v_t, bias, mask):
        s = lax.dot_general(qm_ref[g], k_t.astype(BF16), _NT, preferred_element_type=F32) * scale + bias
        s = jnp.where(mask, s, NEG_INF)
        m_old = m_ref[g]
        m_new = jnp.maximum(m_old, jnp.max(s, axis=-1, keepdims=True))
        p = jnp.where(mask, jnp.exp(s - m_new), 0.0)
        alpha = jnp.exp(m_old - m_new)
        l_ref[g] = alpha * l_ref[g] + jnp.sum(p, axis=-1, keepdims=True)
        acc_ref[g] = alpha * acc_ref[g] + jnp.dot(p.astype(BF16), v_t.astype(BF16), preferred_element_type=F32)
        m_ref[g] = m_new

    def reset_state():
        m_ref[...] = jnp.full(m_ref.shape, NEG_INF, F32)
        l_ref[...] = jnp.zeros(l_ref.shape, F32)
        acc_ref[...] = jnp.zeros(acc_ref.shape, F32)

    @pl.when(s_id == 0)
    def _():
        reset_state()
        q = q_ref[...]
        jn = lax.broadcasted_iota(jnp.int32, (S, nbp), 1)
        jf = jn.astype(F32)
        qpos8 = past_len + lax.broadcasted_iota(jnp.int32, (S, 1), 0)
        cur = qpos8 // BLOCK
        forced = (jn == 0) | (jn == cur) | (jn == cur - 1)
        jc = lax.broadcasted_iota(jnp.int32, (RS, nbc), 1)
        valid = (past_len + qi - (jc * BLOCK + (BLOCK - 1))) >= 0
        for g in range(G):
            qg = jnp.concatenate([q[:, (g * R + r) * HEAD_DIM:(g * R + r + 1) * HEAD_DIM] for r in range(R)], axis=0)
            qm_ref[g] = qg.astype(BF16)
            sc = lax.dot_general(qm_ref[g], kc_ref[0, g].astype(BF16), _NT, preferred_element_type=F32) * scale
            sc = jnp.where(valid, sc + cb_ref[g], NEG_INF)
            mc = jnp.max(sc, axis=-1, keepdims=True)
            ec = jnp.where(valid, jnp.exp(sc - mc), 0.0)
            lc = jnp.sum(ec, axis=-1, keepdims=True)
            pc = ec / jnp.where(lc > 0.0, lc, 1.0)
            ocmp_ref[g] = jnp.dot(pc.astype(BF16), vc_ref[0, g].astype(BF16), preferred_element_type=F32)
            imp = pc[0:S]
            for r in range(1, R):
                imp = imp + pc[r * S:(r + 1) * S]
            imp = jnp.concatenate([imp, jnp.zeros((S, nbp - nbc), F32)], axis=1)
            work = jnp.where(jn > cur, IMP_FUTURE, jnp.where(forced, IMP_FORCED, imp))
            work = jnp.where(jn >= n_blocks, SEL_PAD, work)
            sel = jnp.zeros((S, nbp), F32)
            for _ in range(min(N_SELECT, n_blocks)):
                mx = jnp.max(work, axis=-1, keepdims=True)
                first = jnp.min(jnp.where(work == mx, jf, float(nbp)), axis=-1, keepdims=True)
                hit = jf == first
                sel = jnp.where(hit, 1.0, sel)
                work = jnp.where(hit, SEL_DONE, work)
            for st in range(nbp // 8):
                selsc_ref[g, st] = jnp.concatenate(
                    [sel[:, st * 8:(st + 1) * 8], jnp.zeros((S, 128 - 8), F32)], axis=1)

    for g in range(G):
        selm = jnp.dot(selsc_ref[g, s_id].astype(BF16), e_ref[...], preferred_element_type=F32)
        selm = expand_rows(selm) > 0.5
        for pi in range(PP):
            tbl = jnp.minimum(n_pages - (s_id * PP + pi), 2)
            pg = pages[pi]
            attend(g, pg[pl.ds(g, PAGE_SIZE, stride=KV_ROWS), :], pg[pl.ds(G + g, PAGE_SIZE, stride=KV_ROWS), :],
                   sb_ref[g, tbl], selm[:, pi * PAGE_SIZE:(pi + 1) * PAGE_SIZE])

    @pl.when(s_id == steps - 1)
    def _():
        dist_new = qi - jl
        o_sel, o_win = [], []
        for g in range(G):
            selm = jnp.dot(selsc_ref[g, steps].astype(BF16), e_ref[...], preferred_element_type=F32)
            mask = (expand_rows(selm[:, :TQ]) > 0.5) & (dist_new >= 0)
            attend(g, nsel_ref[0, :, g * HEAD_DIM:(g + 1) * HEAD_DIM],
                   nsel_ref[0, :, (G + g) * HEAD_DIM:(G + g + 1) * HEAD_DIM], sb_ref[g, 0], mask)
            o_sel.append(acc_ref[g] / l_ref[g])
        reset_state()
        for g in range(G):
            for t in range(lw // TQ):
                base = lw - t * TQ
                dist = base + qi - jl
                attend(g, winb_ref[pl.ds(t * TQ * KV_ROWS + g, TQ, stride=KV_ROWS), :],
                       winb_ref[pl.ds(t * TQ * KV_ROWS + G + g, TQ, stride=KV_ROWS), :],
                       sb_ref[g, min(base // TQ, 2)], (dist >= 0) & (dist <= WINDOW))
            attend(g, nwin_ref[0, :, g * HEAD_DIM:(g + 1) * HEAD_DIM],
                   nwin_ref[0, :, (G + g) * HEAD_DIM:(G + g + 1) * HEAD_DIM], sb_ref[g, 0],
                   (dist_new >= 0) & (dist_new <= WINDOW))
            o_win.append(acc_ref[g] / l_ref[g])
        gates = jax.nn.sigmoid(gl_ref[...])
        for g in range(G):
            oc = ocmp_ref[g]
            for r in range(R):
                c = g * 128 + r
                rows = slice(r * S, (r + 1) * S)
                o = (gates[:, c:c + 1] * oc[rows] + gates[:, c + R:c + R + 1] * o_sel[g][rows]
                     + gates[:, c + 2 * R:c + 2 * R + 1] * o_win[g][rows])
                o_ref[:, (g * R + r) * HEAD_DIM:(g * R + r + 1) * HEAD_DIM] = o


def sample_bias_tables(rel_table, S, past_len, nbc):
    G, R = KV_GROUPS, HEADS_PER_GROUP
    d = jnp.arange(3)[:, None, None] * TQ + jnp.arange(S)[None, :, None] - jnp.arange(TQ)[None, None, :]
    b = rel_table[rel_bucket(d)].astype(F32).reshape(3, S, TQ, G, R)
    sb = b.transpose(3, 0, 4, 1, 2).reshape(G, 3, R * S, TQ)
    dc = past_len + jnp.arange(S)[:, None] - (jnp.arange(nbc) * BLOCK + (BLOCK - 1))[None, :]
    cb = head_bias(rel_table, dc).reshape(G, R * S, nbc)
    return sb, cb


def nsa_sample_attention(q, kc, vc, pool_sel, page_table, win_buf, new_sel, new_win, gl, rel_table):
    G, R = KV_GROUPS, HEADS_PER_GROUP
    B, n_pages = page_table.shape
    S = new_sel.shape[1]
    lw = win_buf.shape[0] // (B * KV_ROWS)
    W = 2 * KV_WIDTH
    nbc = kc.shape[2]
    PP = PAGES_PER_STEP
    steps = n_pages // PP
    assert n_pages % PP == 0 and lw % TQ == 0 and S <= BLOCK and PP * PAGE_SIZE // BLOCK == 8
    sb, cb = sample_bias_tables(rel_table, S, n_pages * PAGE_SIZE, nbc)
    pad = ((0, 0), (0, NEW_PAD - S), (0, 0))
    new_sel, new_win = jnp.pad(new_sel, pad), jnp.pad(new_win, pad)
    expand = np.zeros((128, PP * PAGE_SIZE), np.float32)
    expand[np.arange(PP * PAGE_SIZE) // BLOCK, np.arange(PP * PAGE_SIZE)] = 1.0
    e = jnp.asarray(expand, BF16)
    page_specs = [pl.BlockSpec((PAGE_SIZE * KV_ROWS, HEAD_DIM), functools.partial(
        lambda b, s, pt, k: (pt[b, s * PP + k], 0), k=k)) for k in range(PP)]
    row = lambda shape: pl.BlockSpec(shape, lambda b, s, pt: (b,) + (0,) * (len(shape) - 1))
    const = lambda shape: pl.BlockSpec(shape, lambda b, s, pt: (0,) * len(shape))
    grid_spec = pltpu.PrefetchScalarGridSpec(
        num_scalar_prefetch=1, grid=(B, steps),
        in_specs=page_specs + [row((S, G * R * HEAD_DIM)), row((1, G, nbc, HEAD_DIM)), row((1, G, nbc, HEAD_DIM)),
                               row((lw * KV_ROWS, HEAD_DIM)), row((1, NEW_PAD, W)), row((1, NEW_PAD, W)),
                               row((S, G * 128)),
                               const((G, R * S, nbc)), const((G, 3, R * S, TQ)), const((128, PP * PAGE_SIZE))],
        out_specs=row((S, G * R * HEAD_DIM)),
        scratch_shapes=[pltpu.VMEM((G, R * S, HEAD_DIM), BF16), pltpu.VMEM((G, steps + 1, S, 128), F32),
                        pltpu.VMEM((G, R * S, HEAD_DIM), F32), pltpu.VMEM((G, R * S, 1), F32),
                        pltpu.VMEM((G, R * S, 1), F32), pltpu.VMEM((G, R * S, HEAD_DIM), F32)])
    return pl.pallas_call(
        functools.partial(_nsa_sample_kernel, S=S, n_pages=n_pages, lw=lw), grid_spec=grid_spec,
        name="nsa_sample_attn",
        out_shape=jax.ShapeDtypeStruct((B * S, G * R * HEAD_DIM), F32),
        compiler_params=pltpu.CompilerParams(dimension_semantics=("parallel", "arbitrary"),
                                             vmem_limit_bytes=V7X_VMEM_LIMIT_BYTES),
    )(page_table, *([pool_sel] * PP), q, kc, vc, win_buf, new_sel, new_win, gl, cb, sb, e)


CONV_HALO = 32


def _rmsnorm_kernel(x_ref, g_ref, o_ref):
    x = x_ref[...]
    y = x * lax.rsqrt(jnp.mean(x * x, axis=-1, keepdims=True) + EPS) * g_ref[...]
    o_ref[...] = y.astype(o_ref.dtype)


def rmsnorm(x, g, out_dtype, tm=256):
    M, D = x.shape
    tm = _pick(M, tm)
    return pl.pallas_call(
        _rmsnorm_kernel, grid=(M // tm,), name="rmsnorm",
        in_specs=[pl.BlockSpec((tm, D), lambda m: (m, 0)), pl.BlockSpec((1, D), lambda m: (0, 0))],
        out_specs=pl.BlockSpec((tm, D), lambda m: (m, 0)),
        out_shape=jax.ShapeDtypeStruct((M, D), out_dtype),
        compiler_params=pltpu.CompilerParams(dimension_semantics=("parallel",),
                                             vmem_limit_bytes=V7X_VMEM_LIMIT_BYTES),
    )(x, g.reshape(1, D))


def _conv_kernel(glu_ref, buf_ref, w_ref, par_ref, z_ref, st_ref, x_ref, *, tt):
    t = pl.program_id(1)
    lead = CONV_HALO - CONV_BUF

    @pl.when(t == 0)
    def _():
        x_ref[0:lead, :] = jnp.zeros((lead, x_ref.shape[1]), F32)
        x_ref[lead:CONV_HALO, :] = buf_ref[0]

    x_ref[CONV_HALO:, :] = glu_ref[0]
    acc = x_ref[pl.ds(lead, tt), :] * w_ref[0:1, :]
    for k in range(1, CONV_WIDTH):
        acc = acc + x_ref[pl.ds(lead + k, tt), :] * w_ref[k:k + 1, :]
    conv = acc + par_ref[0:1, :]
    mu = jnp.mean(conv, axis=-1, keepdims=True)
    c = conv - mu
    var = jnp.mean(c * c, axis=-1, keepdims=True)
    y = c * lax.rsqrt(var + EPS) * par_ref[1:2, :] + par_ref[2:3, :]
    z_ref[0] = (y * jax.nn.sigmoid(y)).astype(z_ref.dtype)

    @pl.when(t == pl.num_programs(1) - 1)
    def _():
        st_ref[0] = x_ref[pl.ds(tt + lead, CONV_BUF), :]

    x_ref[0:CONV_HALO, :] = x_ref[pl.ds(tt, CONV_HALO), :]


def conv_ln_silu(glu, buf, dw_w, dw_b, ln_g, ln_b, tt=256):
    B, S, D = glu.shape
    tt = _pick(S, tt)
    assert tt % 8 == 0
    par = jnp.stack([dw_b, ln_g, ln_b])
    return pl.pallas_call(
        functools.partial(_conv_kernel, tt=tt), grid=(B, S // tt), name="conv_ln_silu",
        in_specs=[pl.BlockSpec((1, tt, D), lambda b, t: (b, t, 0)),
                  pl.BlockSpec((1, CONV_BUF, D), lambda b, t: (b, 0, 0)),
                  pl.BlockSpec((CONV_WIDTH, D), lambda b, t: (0, 0)),
                  pl.BlockSpec((3, D), lambda b, t: (0, 0))],
        out_specs=[pl.BlockSpec((1, tt, D), lambda b, t: (b, t, 0)),
                   pl.BlockSpec((1, CONV_BUF, D), lambda b, t: (b, 0, 0))],
        out_shape=[jax.ShapeDtypeStruct((B, S, D), BF16), jax.ShapeDtypeStruct((B, CONV_BUF, D), F32)],
        scratch_shapes=[pltpu.VMEM((CONV_HALO + tt, D), F32)],
        compiler_params=pltpu.CompilerParams(dimension_semantics=("parallel", "arbitrary"),
                                             vmem_limit_bytes=V7X_VMEM_LIMIT_BYTES),
    )(glu, buf, dw_w, par)


def rel_bucket(dist):
    n = jnp.maximum(dist, 0)
    max_exact = REL_BUCKETS // 2
    nf = jnp.maximum(n, 1).astype(jnp.float32)
    large = max_exact + (jnp.log(nf / max_exact) / math.log(REL_MAX_DIST / max_exact)
                         * (REL_BUCKETS - max_exact)).astype(jnp.int32)
    large = jnp.minimum(large, REL_BUCKETS - 1)
    return jnp.where(n < max_exact, n, large)


def head_bias(rel_table, dist):
    b = rel_table[rel_bucket(dist)].astype(jnp.float32)
    return b.reshape(dist.shape + (KV_GROUPS, HEADS_PER_GROUP)).transpose(2, 3, 0, 1)


def nsa_project(hb, p, q_dtype):
    q2 = matmul(hb, p["w_q"], out_dtype=q_dtype)
    kvs = [matmul(hb, p["w_kv"][i]) for i in range(N_BRANCHES)]
    gl = matmul(hb, p["w_gate"])
    return q2, kvs, gl


def nsa_prompt(hb, B, T, p, rel_table):
    q2, kvs, gl = nsa_project(hb, p, BF16)
    ident = jnp.arange(B * T // PAGE_SIZE, dtype=jnp.int32).reshape(B, T // PAGE_SIZE)
    kc, vc = compress_paged(kvs[0], ident, p, tall=False)
    nb = T // BLOCK
    dist_c = jnp.arange(T)[:, None] - (jnp.arange(nb) * BLOCK + (BLOCK - 1))[None, :]
    o = nsa_prompt_attention(q2, kvs[1], kvs[2], kc, vc, gl,
                             toeplitz_bias(rel_table), head_bias(rel_table, dist_c), B, T)
    kv_cmp, kv_sel, kv_win = [kv.reshape(B, T, 2, KV_GROUPS, HEAD_DIM) for kv in kvs]
    return o, kv_cmp, kv_sel, kv_win[:, T - min(WINDOW, T):]


def nsa_sample(hb, B, S, pool_cmp, pool_sel, win_buf, page_table, p, rel_table):
    W = 2 * KV_WIDTH
    n_pages = page_table.shape[1]
    q2, kvs, gl = nsa_project(hb, p, F32)
    assert (n_pages * PAGE_SIZE + S) // BLOCK == n_pages * PAGE_SIZE // BLOCK
    kc, vc = compress_paged(pool_cmp.reshape(-1, HEAD_DIM), page_table, p, tall=True)
    lw = win_buf.shape[1]
    o = nsa_sample_attention(q2, kc, vc, pool_sel.reshape(-1, HEAD_DIM), page_table,
                             win_buf.reshape(-1, HEAD_DIM), kvs[1].reshape(B, S, W), kvs[2].reshape(B, S, W),
                             gl, rel_table)
    kv_cmp, kv_sel, kv_win = [kv.reshape(B, S, 2, KV_GROUPS, HEAD_DIM) for kv in kvs]
    all_win = jnp.concatenate([win_buf, kv_win], axis=1)
    return o, kv_cmp, kv_sel, all_win[:, lw + S - min(WINDOW, lw + S):]


def ffn(x, g, w_up, w_down):
    hid = matmul(rmsnorm(x, g, BF16), w_up, act="relu2", out_dtype=BF16)
    return matmul(hid, w_down, res=x, tk=2048)


def kernel(x_prompt, x_sample, state_conv, cache_kv_cmp, cache_kv_sel, state_kv_win, page_table, norm_mix_g, norm_ffn_g, norm_final_g, rel_bias_table, conv_w_in, conv_b_in, conv_dw_w, conv_dw_b, conv_ln_g, conv_ln_b, conv_w_out, conv_b_out, nsa_w_in, nsa_cmp_pos, nsa_cmp_w1, nsa_cmp_b1, nsa_cmp_w2, nsa_cmp_b2, nsa_w_out, ffn_w_up, ffn_w_down):
    Bp, T, D = x_prompt.shape
    Bs, S, _ = x_sample.shape
    xp, xs = x_prompt.reshape(Bp * T, D), x_sample.reshape(Bs * S, D)
    qd = N_HEADS * HEAD_DIM
    kvd = 2 * KV_WIDTH

    wa, wg = conv_w_in[0][:, :D].astype(BF16), conv_w_in[0][:, D:].astype(BF16)
    ba, bg = conv_b_in[0][:D], conv_b_in[0][D:]
    w_out0 = conv_w_out[0].astype(BF16)

    def conv_layer(x, B, L, buf):
        glu = matmul_glu(rmsnorm(x, norm_mix_g[0], BF16), wa, wg, ba, bg)
        z, new_buf = conv_ln_silu(glu.reshape(B, L, D), buf, conv_dw_w[0], conv_dw_b[0], conv_ln_g[0], conv_ln_b[0])
        return matmul(z.reshape(B * L, D), w_out0, bias=conv_b_out[0], res=x), new_buf

    xp, conv_p = conv_layer(xp, Bp, T, jnp.zeros((Bp, CONV_BUF, D), F32))
    xs, conv_s = conv_layer(xs, Bs, S, state_conv[0])
    w_up, w_dn = ffn_w_up[0].astype(BF16), ffn_w_down[0].astype(BF16)
    xp = ffn(xp, norm_ffn_g[0], w_up, w_dn)
    xs = ffn(xs, norm_ffn_g[0], w_up, w_dn)

    w_in = nsa_w_in[0]
    nbr = N_BRANCHES * HEADS_PER_GROUP
    gate_w = w_in[:, qd + N_BRANCHES * kvd:].reshape(D, N_BRANCHES, KV_GROUPS, HEADS_PER_GROUP)
    gate_w = gate_w.transpose(0, 2, 1, 3).reshape(D, KV_GROUPS, nbr)
    gate_w = jnp.pad(gate_w, ((0, 0), (0, 0), (0, 128 - nbr))).reshape(D, KV_GROUPS * 128)
    npar = dict(w_q=w_in[:, :qd].astype(BF16),
                w_kv=[w_in[:, qd + i * kvd: qd + (i + 1) * kvd].astype(BF16) for i in range(N_BRANCHES)],
                w_gate=gate_w.astype(BF16), pos=nsa_cmp_pos[0], w1=nsa_cmp_w1[0].astype(BF16),
                b1=nsa_cmp_b1[0], w2=nsa_cmp_w2[0].astype(BF16), b2=nsa_cmp_b2[0])
    w_out1 = nsa_w_out[0].astype(BF16)
    op, kcp, ksp, kwp = nsa_prompt(rmsnorm(xp, norm_mix_g[1], BF16), Bp, T, npar, rel_bias_table)
    os_, kcs, kss, kws = nsa_sample(rmsnorm(xs, norm_mix_g[1], BF16), Bs, S, cache_kv_cmp[0], cache_kv_sel[0],
                                    state_kv_win[0], page_table, npar, rel_bias_table)
    xp = matmul(op, w_out1, res=xp)
    xs = matmul(os_.astype(BF16), w_out1, res=xs)
    w_up, w_dn = ffn_w_up[1].astype(BF16), ffn_w_down[1].astype(BF16)
    xp = ffn(xp, norm_ffn_g[1], w_up, w_dn)
    xs = ffn(xs, norm_ffn_g[1], w_up, w_dn)

    y_prompt = rmsnorm(xp, norm_final_g, F32).reshape(Bp, T, D)
    y_sample = rmsnorm(xs, norm_final_g, F32).reshape(Bs, S, D)
    return (y_prompt, y_sample, conv_p[None], conv_s[None], kcp[None], kcs[None],
            ksp[None], kss[None], kwp[None], kws[None])
```

```python
import functools
import math

import jax
import jax.numpy as jnp
import numpy as np
from jax import lax
from jax.experimental import pallas as pl
from jax.experimental.pallas import tpu as pltpu

D_MODEL = 4096
CONV_WIDTH = 31
CONV_BUF = CONV_WIDTH - 1
N_HEADS = 32
HEAD_DIM = D_MODEL // N_HEADS
KV_GROUPS = 4
HEADS_PER_GROUP = N_HEADS // KV_GROUPS
KV_WIDTH = KV_GROUPS * HEAD_DIM
BLOCK = 64
N_SELECT = 16
WINDOW = 512
WIN_Q_BLOCK = 128
SEL_Q_CHUNK = 16
N_BRANCHES = 3
PAGE_SIZE = 128
REL_BUCKETS = 32
REL_MAX_DIST = 128
EPS = 1e-6
NEG_INF = -1e30

V7X_VMEM_LIMIT_BYTES = 56 * 1024 * 1024
BF16 = jnp.bfloat16
F32 = jnp.float32


def _apply_act(y, act):
    if act == "relu2":
        return jnp.square(jnp.maximum(y, 0.0))
    if act == "silu":
        return y * jax.nn.sigmoid(y)
    assert act is None
    return y


def _mm_kernel(*refs, act, has_bias, has_res, nk):
    x_ref, w_ref = refs[0], refs[1]
    pos = 2
    b_ref = r_ref = None
    if has_bias:
        b_ref = refs[pos]
        pos += 1
    if has_res:
        r_ref = refs[pos]
        pos += 1
    o_ref = refs[pos]
    acc_ref = refs[pos + 1] if nk > 1 else None

    def epilogue(y):
        if has_bias:
            y = y + b_ref[...]
        y = _apply_act(y, act)
        if has_res:
            y = y + r_ref[...]
        o_ref[...] = y.astype(o_ref.dtype)

    part = jnp.dot(x_ref[...], w_ref[...], preferred_element_type=F32)
    if nk == 1:
        epilogue(part)
        return
    k = pl.program_id(2)

    @pl.when(k == 0)
    def _():
        acc_ref[...] = part

    @pl.when(jnp.logical_and(k > 0, k < nk - 1))
    def _():
        acc_ref[...] += part

    @pl.when(k == nk - 1)
    def _():
        epilogue(acc_ref[...] + part)


def _pick(dim, pref):
    t = min(dim, pref)
    assert dim % t == 0, (dim, pref)
    return t


def matmul(x, w, bias=None, res=None, act=None, out_dtype=F32, tm=1024, tn=1024, tk=4096):
    M, K = x.shape
    K2, N = w.shape
    assert K == K2
    tm, tn, tk = _pick(M, tm), _pick(N, tn), _pick(K, tk)
    nk = K // tk
    in_specs = [pl.BlockSpec((tm, tk), lambda m, n, k: (m, k)),
                pl.BlockSpec((tk, tn), lambda m, n, k: (k, n))]
    args = [x, w]
    if bias is not None:
        in_specs.append(pl.BlockSpec((1, tn), lambda m, n, k: (0, n)))
        args.append(bias.reshape(1, N).astype(F32))
    if res is not None:
        in_specs.append(pl.BlockSpec((tm, tn), lambda m, n, k: (m, n)))
        args.append(res)
    scratch = [pltpu.VMEM((tm, tn), F32)] if nk > 1 else []
    return pl.pallas_call(
        functools.partial(_mm_kernel, act=act, has_bias=bias is not None,
                          has_res=res is not None, nk=nk),
        grid=(M // tm, N // tn, nk),
        in_specs=in_specs,
        out_specs=pl.BlockSpec((tm, tn), lambda m, n, k: (m, n)),
        out_shape=jax.ShapeDtypeStruct((M, N), out_dtype),
        scratch_shapes=scratch,
        compiler_params=pltpu.CompilerParams(
            dimension_semantics=("parallel", "parallel", "arbitrary"),
            vmem_limit_bytes=V7X_VMEM_LIMIT_BYTES),
    )(*args)


def _glu_kernel(x_ref, wa_ref, wg_ref, ba_ref, bg_ref, o_ref):
    x = x_ref[...]
    a = jnp.dot(x, wa_ref[...], preferred_element_type=F32) + ba_ref[...]
    g = jnp.dot(x, wg_ref[...], preferred_element_type=F32) + bg_ref[...]
    o_ref[...] = a * jax.nn.sigmoid(g)


def matmul_glu(x, wa, wg, ba, bg, tm=1024, tn=512):
    M, K = x.shape
    N = wa.shape[1]
    tm, tn = _pick(M, tm), _pick(N, tn)
    wspec = pl.BlockSpec((K, tn), lambda m, n: (0, n))
    bspec = pl.BlockSpec((1, tn), lambda m, n: (0, n))
    return pl.pallas_call(
        _glu_kernel,
        grid=(M // tm, N // tn),
        in_specs=[pl.BlockSpec((tm, K), lambda m, n: (m, 0)), wspec, wspec, bspec, bspec],
        out_specs=pl.BlockSpec((tm, tn), lambda m, n: (m, n)),
        out_shape=jax.ShapeDtypeStruct((M, N), F32),
        compiler_params=pltpu.CompilerParams(
            dimension_semantics=("parallel", "parallel"),
            vmem_limit_bytes=V7X_VMEM_LIMIT_BYTES),
    )(x, wa, wg, ba.reshape(1, N), bg.reshape(1, N))


TQ = 128
SEL_KT = 4 * TQ
IMP_FORCED = 16.0
IMP_FUTURE = -1.0
_NT = (((1,), (1,)), ((), ()))


def _nsa_prompt_kernel(q_ref, ks_ref, vs_ref, kw_ref, vw_ref, kc_ref, vc_ref, gl_ref, toe_ref, cb_ref,
                       e_ref, o_ref, selm_ref, m_ref, l_ref, acc_ref, *, T):
    R = HEADS_PER_GROUP
    nb = T // BLOCK
    nqt = T // TQ
    i = pl.program_id(2)
    scale = HEAD_DIM ** -0.5
    qb = q_ref[...]
    qm = jnp.concatenate([qb[:, r * HEAD_DIM:(r + 1) * HEAD_DIM] for r in range(R)], axis=0)
    t_loc = lax.broadcasted_iota(jnp.int32, (TQ, 1), 0)
    q_pos = i * TQ + t_loc

    kc = kc_ref[0, 0].astype(BF16)
    vc = vc_ref[0, 0].astype(BF16)
    sc = lax.dot_general(qm, kc, _NT, preferred_element_type=F32) * scale
    sc = sc.reshape(R, TQ, nb) + cb_ref[0]
    jn = lax.broadcasted_iota(jnp.int32, (TQ, nb), 1)
    valid = (q_pos - (jn * BLOCK + (BLOCK - 1))) >= 0
    sc = jnp.where(valid[None], sc, NEG_INF)
    mc = jnp.max(sc, axis=-1, keepdims=True)
    ec = jnp.where(valid[None], jnp.exp(sc - mc), 0.0)
    lc = jnp.sum(ec, axis=-1, keepdims=True)
    pc = ec / jnp.where(lc > 0.0, lc, 1.0)
    o_cmp = jnp.dot(pc.reshape(R * TQ, nb).astype(BF16), vc, preferred_element_type=F32).reshape(R, TQ, HEAD_DIM)
    imp = pc[0]
    for r in range(1, R):
        imp = imp + pc[r]

    cur = q_pos // BLOCK
    forced = (jn == 0) | (jn == cur) | (jn == cur - 1)
    impp = jnp.where(jn > cur, IMP_FUTURE, jnp.where(forced, IMP_FORCED, imp))
    rank = jnp.zeros((TQ, nb), jnp.int32)
    for jp in range(nb):
        col = impp[:, jp:jp + 1]
        beats = (col > impp) | ((col == impp) & (jn > jp))
        rank = rank + beats.astype(jnp.int32)
    sel = (rank < min(N_SELECT, nb)).astype(BF16)
    selm = jnp.dot(sel, e_ref[...], preferred_element_type=F32)
    for st in range(T // SEL_KT):
        selm_ref[st] = selm[:, st * SEL_KT:(st + 1) * SEL_KT]

    def scores(k_t, kt0, n_sub):
        s = lax.dot_general(qm, k_t, _NT, preferred_element_type=F32) * scale
        bias = jnp.concatenate([toe_ref[0, jnp.clip(i - kt0 - c, 0, 2)] for c in range(n_sub)], axis=1)
        j_loc = lax.broadcasted_iota(jnp.int32, (1, n_sub * TQ), 1)
        dist = (i - kt0) * TQ + t_loc - j_loc
        return (s + bias).reshape(R, TQ, n_sub * TQ), dist

    m_ref[...] = jnp.full(m_ref.shape, NEG_INF, F32)
    l_ref[...] = jnp.zeros(l_ref.shape, F32)
    acc_ref[...] = jnp.zeros(acc_ref.shape, F32)
    sub = SEL_KT // TQ

    def sel_body(st, carry):
        ks = pl.multiple_of(st * SEL_KT, SEL_KT)
        s, dist = scores(ks_ref[pl.ds(ks, SEL_KT), :].astype(BF16), st * sub, sub)
        mask = (dist >= 0) & (selm_ref[st] > 0.5)
        s = jnp.where(mask[None], s, NEG_INF)
        m_old = m_ref[...]
        m_new = jnp.maximum(m_old, jnp.max(s, axis=-1, keepdims=True))
        p = jnp.exp(s - m_new)
        alpha = jnp.exp(m_old - m_new)
        l_ref[...] = alpha * l_ref[...] + jnp.sum(p, axis=-1, keepdims=True)
        pv = jnp.dot(p.reshape(R * TQ, SEL_KT).astype(BF16), vs_ref[pl.ds(ks, SEL_KT), :].astype(BF16),
                     preferred_element_type=F32)
        acc_ref[...] = alpha * acc_ref[...] + pv.reshape(R, TQ, HEAD_DIM)
        m_ref[...] = m_new
        return carry

    lax.fori_loop(0, i // sub + 1, sel_body, 0)
    o_sel = acc_ref[...] / l_ref[...]

    n_win = WINDOW // TQ + 1
    kt0 = jnp.clip(i - WINDOW // TQ, 0, nqt - n_win)
    ws = pl.multiple_of(kt0 * TQ, TQ)
    s, dist = scores(kw_ref[pl.ds(ws, n_win * TQ), :].astype(BF16), kt0, n_win)
    s = jnp.where(((dist >= 0) & (dist <= WINDOW))[None], s, NEG_INF)
    p = jnp.exp(s - jnp.max(s, axis=-1, keepdims=True))
    lw = jnp.sum(p, axis=-1, keepdims=True)
    pv = jnp.dot(p.reshape(R * TQ, n_win * TQ).astype(BF16), vw_ref[pl.ds(ws, n_win * TQ), :].astype(BF16),
                 preferred_element_type=F32)
    o_win = pv.reshape(R, TQ, HEAD_DIM) / lw

    gates = jax.nn.sigmoid(gl_ref[...])
    for r in range(R):
        o = (gates[:, r:r + 1] * o_cmp[r] + gates[:, R + r:R + r + 1] * o_sel[r]
             + gates[:, 2 * R + r:2 * R + r + 1] * o_win[r])
        o_ref[:, r * HEAD_DIM:(r + 1) * HEAD_DIM] = o.astype(o_ref.dtype)


def nsa_prompt_attention(q2, kv_sel, kv_win, kc, vc, gl, toe, cb, B, T):
    G, R = KV_GROUPS, HEADS_PER_GROUP
    nb, nqt = T // BLOCK, T // TQ
    assert T % SEL_KT == 0 and nqt >= WINDOW // TQ + 1
    expand = (np.arange(T)[None, :] // BLOCK == np.arange(nb)[:, None])
    e = jnp.asarray(expand, BF16)
    qspec = pl.BlockSpec((TQ, R * HEAD_DIM), lambda b, g, i: (b * nqt + i, g))
    kspec = pl.BlockSpec((T, HEAD_DIM), lambda b, g, i: (b, g))
    vspec = pl.BlockSpec((T, HEAD_DIM), lambda b, g, i: (b, G + g))
    cspec = pl.BlockSpec((1, 1, nb, HEAD_DIM), lambda b, g, i: (b, g, 0, 0))
    return pl.pallas_call(
        functools.partial(_nsa_prompt_kernel, T=T), name="nsa_prompt_attn",
        grid=(B, G, nqt),
        in_specs=[qspec, kspec, vspec, kspec, vspec, cspec, cspec,
                  pl.BlockSpec((TQ, 128), lambda b, g, i: (b * nqt + i, g)),
                  pl.BlockSpec((1, 3, R * TQ, TQ), lambda b, g, i: (g, 0, 0, 0)),
                  pl.BlockSpec((1, R, TQ, nb), lambda b, g, i: (g, 0, i, 0)),
                  pl.BlockSpec((nb, T), lambda b, g, i: (0, 0))],
        out_specs=qspec,
        out_shape=jax.ShapeDtypeStruct((B * T, G * R * HEAD_DIM), BF16),
        scratch_shapes=[pltpu.VMEM((T // SEL_KT, TQ, SEL_KT), F32), pltpu.VMEM((R, TQ, 1), F32),
                        pltpu.VMEM((R, TQ, 1), F32), pltpu.VMEM((R, TQ, HEAD_DIM), F32)],
        compiler_params=pltpu.CompilerParams(
            dimension_semantics=("parallel", "parallel", "arbitrary"),
            vmem_limit_bytes=V7X_VMEM_LIMIT_BYTES),
    )(q2, kv_sel, kv_sel, kv_win, kv_win, kc, vc, gl, toe, cb, e)


def toeplitz_bias(rel_table):
    d = jnp.arange(3)[:, None, None] * TQ + jnp.arange(TQ)[None, :, None] - jnp.arange(TQ)[None, None, :]
    b = rel_table[rel_bucket(d)].astype(F32)
    b = b.reshape(3, TQ, TQ, KV_GROUPS, HEADS_PER_GROUP).transpose(3, 0, 4, 1, 2)
    return b.reshape(KV_GROUPS, 3, HEADS_PER_GROUP * TQ, TQ)


NEW_PAD = TQ
PAGES_PER_STEP = 4
SEL_DONE = -3.0
SEL_PAD = -2.0


BLOCKIFY_PAGES = 8
KV_ROWS = 2 * KV_GROUPS


def _blockify_kernel(pt_ref, *refs, tall):
    del pt_ref
    NP = BLOCKIFY_PAGES
    pages, pos_ref, o_ref, xs_ref = refs[:NP], refs[NP], refs[NP + 1], refs[NP + 2]
    rows = pages[0].shape[0]
    for pi, pg in enumerate(pages):
        if tall:
            xs_ref[pi * rows:(pi + 1) * rows, :] = pg[...]
        else:
            for c in range(KV_ROWS):
                xs_ref[c, pi * rows:(pi + 1) * rows, :] = pg[:, c * HEAD_DIM:(c + 1) * HEAD_DIM]
    nblk = NP * PAGE_SIZE // BLOCK
    for kv in range(2):
        for g in range(KV_GROUPS):
            c = kv * KV_GROUPS + g
            for j in range(BLOCK):
                if tall:
                    v = xs_ref[pl.ds(j * KV_ROWS + c, nblk, stride=BLOCK * KV_ROWS), :]
                else:
                    v = xs_ref[c, pl.ds(j, nblk, stride=BLOCK), :]
                o_ref[kv, 0, g, :, j * HEAD_DIM:(j + 1) * HEAD_DIM] = (v + pos_ref[kv, j:j + 1, :]).astype(o_ref.dtype)


def blockify(pool, page_table, pos, tall):
    B, n_pages = page_table.shape
    NP = BLOCKIFY_PAGES
    assert n_pages % NP == 0
    blk = (PAGE_SIZE * KV_ROWS, HEAD_DIM) if tall else (PAGE_SIZE, KV_ROWS * HEAD_DIM)
    page_specs = [pl.BlockSpec(blk, functools.partial(lambda b, s, pt, k: (pt[b, s * NP + k], 0), k=k))
                  for k in range(NP)]
    nbs = NP * PAGE_SIZE // BLOCK
    grid_spec = pltpu.PrefetchScalarGridSpec(
        num_scalar_prefetch=1, grid=(B, n_pages // NP),
        in_specs=page_specs + [pl.BlockSpec((2, BLOCK, HEAD_DIM), lambda b, s, pt: (0, 0, 0))],
        out_specs=pl.BlockSpec((2, 1, KV_GROUPS, nbs, BLOCK * HEAD_DIM), lambda b, s, pt: (0, b, 0, s, 0)),
        scratch_shapes=[pltpu.VMEM((NP * PAGE_SIZE * KV_ROWS, HEAD_DIM) if tall else
                                   (KV_ROWS, NP * PAGE_SIZE, HEAD_DIM), F32)])
    return pl.pallas_call(
        functools.partial(_blockify_kernel, tall=tall), grid_spec=grid_spec, name="blockify",
        out_shape=jax.ShapeDtypeStruct((2, B, KV_GROUPS, n_pages * PAGE_SIZE // BLOCK, BLOCK * HEAD_DIM), BF16),
        compiler_params=pltpu.CompilerParams(dimension_semantics=("parallel", "arbitrary"),
                                             vmem_limit_bytes=V7X_VMEM_LIMIT_BYTES),
    )(page_table, *([pool] * NP), pos)


def _compress_mlp_kernel(a_ref, w1_ref, b1_ref, w2_ref, b2_ref, o_ref):
    h = jnp.dot(a_ref[0], w1_ref[0], preferred_element_type=F32) + b1_ref[0]
    h = h * jax.nn.sigmoid(h)
    o_ref[0] = jnp.dot(h.astype(BF16), w2_ref[0], preferred_element_type=F32) + b2_ref[0]


def compress_mlp(a, w1, b1, w2, b2, tm=512):
    _, M, K = a.shape
    H = w1.shape[-1]
    tm = _pick(M, tm)
    return pl.pallas_call(
        _compress_mlp_kernel, grid=(2, M // tm), name="compress_mlp",
        in_specs=[pl.BlockSpec((1, tm, K), lambda k, m: (k, m, 0)),
                  pl.BlockSpec((1, K, H), lambda k, m: (k, 0, 0)),
                  pl.BlockSpec((1, 1, H), lambda k, m: (k, 0, 0)),
                  pl.BlockSpec((1, H, HEAD_DIM), lambda k, m: (k, 0, 0)),
                  pl.BlockSpec((1, 1, HEAD_DIM), lambda k, m: (k, 0, 0))],
        out_specs=pl.BlockSpec((1, tm, HEAD_DIM), lambda k, m: (k, m, 0)),
        out_shape=jax.ShapeDtypeStruct((2, M, HEAD_DIM), F32),
        compiler_params=pltpu.CompilerParams(dimension_semantics=("parallel", "parallel"),
                                             vmem_limit_bytes=V7X_VMEM_LIMIT_BYTES),
    )(a, w1, b1.reshape(2, 1, H), w2, b2.reshape(2, 1, HEAD_DIM))


def compress_paged(pool, page_table, p, tall):
    a = blockify(pool, page_table, p["pos"], tall)
    _, B, G, nb = a.shape[:4]
    out = compress_mlp(a.reshape(2, B * G * nb, BLOCK * HEAD_DIM), p["w1"], p["b1"], p["w2"], p["b2"])
    out = out.reshape(2, B, G, nb, HEAD_DIM)
    return out[0], out[1]


def _expand_rows(x8):
    return jnp.concatenate([x8] * HEADS_PER_GROUP, axis=0)


def _head_rows(q, g):
    R = HEADS_PER_GROUP
    return jnp.concatenate([q[:, (g * R + r) * HEAD_DIM:(g * R + r + 1) * HEAD_DIM] for r in range(R)], axis=0)


def _nsa_sample_select_kernel(q_ref, kc_ref, vc_ref, cb_ref, ocmp_ref, selsc_ref, sel_ref, *, S, n_pages):
    G, R = KV_GROUPS, HEADS_PER_GROUP
    RS = R * S
    nbc = kc_ref.shape[2]
    nbp = sel_ref.shape[3]
    n_blocks = -(-(n_pages * PAGE_SIZE + S) // BLOCK)
    past_len = n_pages * PAGE_SIZE
    scale = HEAD_DIM ** -0.5
    qi = lax.broadcasted_iota(jnp.int32, (RS, 1), 0) % S
    q = q_ref[...]
    jn = lax.broadcasted_iota(jnp.int32, (S, nbp), 1)
    jf = jn.astype(F32)
    cur = (past_len + lax.broadcasted_iota(jnp.int32, (S, 1), 0)) // BLOCK
    forced = (jn == 0) | (jn == cur) | (jn == cur - 1)
    jc = lax.broadcasted_iota(jnp.int32, (RS, nbc), 1)
    valid = (past_len + qi - (jc * BLOCK + (BLOCK - 1))) >= 0
    for g in range(G):
        qg = _head_rows(q, g).astype(BF16)
        sc = lax.dot_general(qg, kc_ref[0, g].astype(BF16), _NT, preferred_element_type=F32) * scale
        sc = jnp.where(valid, sc + cb_ref[g], NEG_INF)
        ec = jnp.where(valid, jnp.exp(sc - jnp.max(sc, axis=-1, keepdims=True)), 0.0)
        lc = jnp.sum(ec, axis=-1, keepdims=True)
        pc = ec / jnp.where(lc > 0.0, lc, 1.0)
        ocmp_ref[0, g] = jnp.dot(pc.astype(BF16), vc_ref[0, g].astype(BF16), preferred_element_type=F32)
        imp = pc[0:S]
        for r in range(1, R):
            imp = imp + pc[r * S:(r + 1) * S]
        imp = jnp.concatenate([imp, jnp.zeros((S, nbp - nbc), F32)], axis=1)
        work = jnp.where(jn > cur, IMP_FUTURE, jnp.where(forced, IMP_FORCED, imp))
        work = jnp.where(jn >= n_blocks, SEL_PAD, work)
        sel = jnp.zeros((S, nbp), F32)
        for _ in range(min(N_SELECT, n_blocks)):
            mx = jnp.max(work, axis=-1, keepdims=True)
            first = jnp.min(jnp.where(work == mx, jf, float(nbp)), axis=-1, keepdims=True)
            hit = jf == first
            sel = jnp.where(hit, 1.0, sel)
            work = jnp.where(hit, SEL_DONE, work)
        sel_ref[0, g] = sel
        for st in range(nbp // 8):
            selsc_ref[0, g, st] = jnp.concatenate(
                [sel[:, st * 8:(st + 1) * 8], jnp.zeros((S, 128 - 8), F32)], axis=1)


def _nsa_sample_kernel(pt_ref, fl_ref, *refs, S, n_pages, lw):
    del pt_ref
    PP = PAGES_PER_STEP
    pages = refs[:PP]
    (q_ref, ocmp_ref, selsc_ref, winb_ref, nsel_ref, nwin_ref, gl_ref, sb_ref, e_ref,
     o_ref, qm_ref, m_ref, l_ref, acc_ref) = refs[PP:]
    G, R = KV_GROUPS, HEADS_PER_GROUP
    RS = R * S
    steps = n_pages // PP
    b = pl.program_id(0)
    s_id = pl.program_id(1)
    scale = HEAD_DIM ** -0.5
    qi = lax.broadcasted_iota(jnp.int32, (RS, 1), 0) % S
    jl = lax.broadcasted_iota(jnp.int32, (1, TQ), 1)

    def attend(g, k_t, v_t, bias, mask):
        s = lax.dot_general(qm_ref[g], k_t.astype(BF16), _NT, preferred_element_type=F32) * scale + bias
        s = jnp.where(mask, s, NEG_INF)
        m_old = m_ref[g]
        m_new = jnp.maximum(m_old, jnp.max(s, axis=-1, keepdims=True))
        p = jnp.where(mask, jnp.exp(s - m_new), 0.0)
        alpha = jnp.exp(m_old - m_new)
        l_ref[g] = alpha * l_ref[g] + jnp.sum(p, axis=-1, keepdims=True)
        acc_ref[g] = alpha * acc_ref[g] + jnp.dot(p.astype(BF16), v_t.astype(BF16), preferred_element_type=F32)
        m_ref[g] = m_new

    def reset_state():
        m_ref[...] = jnp.full(m_ref.shape, NEG_INF, F32)
        l_ref[...] = jnp.zeros(l_ref.shape, F32)
        acc_ref[...] = jnp.zeros(acc_ref.shape, F32)

    @pl.when(s_id == 0)
    def _():
        reset_state()
        q = q_ref[...]
        for g in range(G):
            qm_ref[g] = _head_rows(q, g).astype(BF16)

    for g in range(G):
        flags = [fl_ref[b * G + g, s_id * PP + pi] for pi in range(PP)]

        def group_pages(g=g, flags=flags):
            selm = jnp.dot(selsc_ref[0, g, s_id].astype(BF16), e_ref[...], preferred_element_type=F32)
            selm = _expand_rows(selm) > 0.5
            for pi in range(PP):
                def one_page(pi=pi):
                    tbl = jnp.minimum(n_pages - (s_id * PP + pi), 2)
                    pg = pages[pi]
                    attend(g, pg[pl.ds(g, PAGE_SIZE, stride=KV_ROWS), :],
                           pg[pl.ds(G + g, PAGE_SIZE, stride=KV_ROWS), :],
                           sb_ref[g, tbl], selm[:, pi * PAGE_SIZE:(pi + 1) * PAGE_SIZE])
                pl.when(flags[pi] > 0)(one_page)

        pl.when(sum(flags) > 0)(group_pages)

    @pl.when(s_id == steps - 1)
    def _():
        dist_new = qi - jl
        o_sel, o_win = [], []
        for g in range(G):
            selm = jnp.dot(selsc_ref[0, g, steps].astype(BF16), e_ref[...], preferred_element_type=F32)
            mask = (_expand_rows(selm[:, :TQ]) > 0.5) & (dist_new >= 0)
            attend(g, nsel_ref[0, :, g * HEAD_DIM:(g + 1) * HEAD_DIM],
                   nsel_ref[0, :, (G + g) * HEAD_DIM:(G + g + 1) * HEAD_DIM], sb_ref[g, 0], mask)
            o_sel.append(acc_ref[g] / l_ref[g])
        reset_state()
        for g in range(G):
            for t in range(lw // TQ):
                base = lw - t * TQ
                dist = base + qi - jl
                attend(g, winb_ref[pl.ds(t * TQ * KV_ROWS + g, TQ, stride=KV_ROWS), :],
                       winb_ref[pl.ds(t * TQ * KV_ROWS + G + g, TQ, stride=KV_ROWS), :],
                       sb_ref[g, min(base // TQ, 2)], (dist >= 0) & (dist <= WINDOW))
            attend(g, nwin_ref[0, :, g * HEAD_DIM:(g + 1) * HEAD_DIM],
                   nwin_ref[0, :, (G + g) * HEAD_DIM:(G + g + 1) * HEAD_DIM], sb_ref[g, 0],
                   (dist_new >= 0) & (dist_new <= WINDOW))
            o_win.append(acc_ref[g] / l_ref[g])
        gates = jax.nn.sigmoid(gl_ref[...])
        for g in range(G):
            oc = ocmp_ref[0, g]
            for r in range(R):
                c = g * 128 + r
                rows = slice(r * S, (r + 1) * S)
                o = (gates[:, c:c + 1] * oc[rows] + gates[:, c + R:c + R + 1] * o_sel[g][rows]
                     + gates[:, c + 2 * R:c + 2 * R + 1] * o_win[g][rows])
                o_ref[:, (g * R + r) * HEAD_DIM:(g * R + r + 1) * HEAD_DIM] = o


def sample_bias_tables(rel_table, S, past_len, nbc):
    G, R = KV_GROUPS, HEADS_PER_GROUP
    d = jnp.arange(3)[:, None, None] * TQ + jnp.arange(S)[None, :, None] - jnp.arange(TQ)[None, None, :]
    b = rel_table[rel_bucket(d)].astype(F32).reshape(3, S, TQ, G, R)
    sb = b.transpose(3, 0, 4, 1, 2).reshape(G, 3, R * S, TQ)
    dc = past_len + jnp.arange(S)[:, None] - (jnp.arange(nbc) * BLOCK + (BLOCK - 1))[None, :]
    cb = head_bias(rel_table, dc).reshape(G, R * S, nbc)
    return sb, cb


def nsa_sample_attention(q, kc, vc, pool_sel, page_table, win_buf, new_sel, new_win, gl, rel_table):
    G, R = KV_GROUPS, HEADS_PER_GROUP
    B, n_pages = page_table.shape
    S = new_sel.shape[1]
    lw = win_buf.shape[0] // (B * KV_ROWS)
    W = 2 * KV_WIDTH
    nbc = kc.shape[2]
    PP = PAGES_PER_STEP
    steps = n_pages // PP
    assert n_pages % PP == 0 and lw % TQ == 0 and S <= BLOCK and PP * PAGE_SIZE // BLOCK == 8
    sb, cb = sample_bias_tables(rel_table, S, n_pages * PAGE_SIZE, nbc)
    nbp = (steps + 1) * 8

    brow = lambda shape: pl.BlockSpec(shape, lambda b: (b,) + (0,) * (len(shape) - 1))
    ocmp, selsc, sel = pl.pallas_call(
        functools.partial(_nsa_sample_select_kernel, S=S, n_pages=n_pages), grid=(B,), name="nsa_sample_select",
        in_specs=[brow((S, G * R * HEAD_DIM)), brow((1, G, nbc, HEAD_DIM)), brow((1, G, nbc, HEAD_DIM)),
                  pl.BlockSpec((G, R * S, nbc), lambda b: (0, 0, 0))],
        out_specs=[brow((1, G, R * S, HEAD_DIM)), brow((1, G, steps + 1, S, 128)), brow((1, G, S, nbp))],
        out_shape=[jax.ShapeDtypeStruct((B, G, R * S, HEAD_DIM), F32),
                   jax.ShapeDtypeStruct((B, G, steps + 1, S, 128), F32),
                   jax.ShapeDtypeStruct((B, G, S, nbp), F32)],
        compiler_params=pltpu.CompilerParams(dimension_semantics=("parallel",),
                                             vmem_limit_bytes=V7X_VMEM_LIMIT_BYTES),
    )(q, kc, vc, cb)
    bpp = PAGE_SIZE // BLOCK
    flags = sel[..., :n_pages * bpp].reshape(B, G, S, n_pages, bpp).max(axis=(2, 4))
    flags = (flags > 0.5).astype(jnp.int32).reshape(B * G, n_pages)

    pad = ((0, 0), (0, NEW_PAD - S), (0, 0))
    new_sel, new_win = jnp.pad(new_sel, pad), jnp.pad(new_win, pad)
    expand = np.zeros((128, PP * PAGE_SIZE), np.float32)
    expand[np.arange(PP * PAGE_SIZE) // BLOCK, np.arange(PP * PAGE_SIZE)] = 1.0
    e = jnp.asarray(expand, BF16)
    page_specs = [pl.BlockSpec((PAGE_SIZE * KV_ROWS, HEAD_DIM), functools.partial(
        lambda b, s, pt, fl, k: (pt[b, s * PP + k], 0), k=k)) for k in range(PP)]
    row = lambda shape: pl.BlockSpec(shape, lambda b, s, pt, fl: (b,) + (0,) * (len(shape) - 1))
    const = lambda shape: pl.BlockSpec(shape, lambda b, s, pt, fl: (0,) * len(shape))
    grid_spec = pltpu.PrefetchScalarGridSpec(
        num_scalar_prefetch=2, grid=(B, steps),
        in_specs=page_specs + [row((S, G * R * HEAD_DIM)), row((1, G, R * S, HEAD_DIM)),
                               row((1, G, steps + 1, S, 128)), row((lw * KV_ROWS, HEAD_DIM)),
                               row((1, NEW_PAD, W)), row((1, NEW_PAD, W)), row((S, G * 128)),
                               const((G, 3, R * S, TQ)), const((128, PP * PAGE_SIZE))],
        out_specs=row((S, G * R * HEAD_DIM)),
        scratch_shapes=[pltpu.VMEM((G, R * S, HEAD_DIM), BF16), pltpu.VMEM((G, R * S, 1), F32),
                        pltpu.VMEM((G, R * S, 1), F32), pltpu.VMEM((G, R * S, HEAD_DIM), F32)])
    return pl.pallas_call(
        functools.partial(_nsa_sample_kernel, S=S, n_pages=n_pages, lw=lw), grid_spec=grid_spec,
        name="nsa_sample_attn",
        out_shape=jax.ShapeDtypeStruct((B * S, G * R * HEAD_DIM), F32),
        compiler_params=pltpu.CompilerParams(dimension_semantics=("parallel", "arbitrary"),
                                             vmem_limit_bytes=V7X_VMEM_LIMIT_BYTES),
    )(page_table, flags, *([pool_sel] * PP), q, ocmp, selsc, win_buf, new_sel, new_win, gl, sb, e)


CONV_HALO = 32


def _rmsnorm_kernel(x_ref, g_ref, o_ref):
    x = x_ref[...]
    y = x * lax.rsqrt(jnp.mean(x * x, axis=-1, keepdims=True) + EPS) * g_ref[...]
    o_ref[...] = y.astype(o_ref.dtype)


def rmsnorm(x, g, out_dtype, tm=256):
    M, D = x.shape
    tm = _pick(M, tm)
    return pl.pallas_call(
        _rmsnorm_kernel, grid=(M // tm,), name="rmsnorm",
        in_specs=[pl.BlockSpec((tm, D), lambda m: (m, 0)), pl.BlockSpec((1, D), lambda m: (0, 0))],
        out_specs=pl.BlockSpec((tm, D), lambda m: (m, 0)),
        out_shape=jax.ShapeDtypeStruct((M, D), out_dtype),
        compiler_params=pltpu.CompilerParams(dimension_semantics=("parallel",),
                                             vmem_limit_bytes=V7X_VMEM_LIMIT_BYTES),
    )(x, g.reshape(1, D))


CONV_LANES = 512
CONV_ROWS = 64


def _conv_kernel(glu_ref, buf_ref, w_ref, b_ref, c_ref, st_ref, x_ref, *, tt):
    t = pl.program_id(2)
    lead = CONV_HALO - CONV_BUF

    @pl.when(t == 0)
    def _():
        x_ref[0:lead, :] = jnp.zeros((lead, x_ref.shape[1]), F32)
        x_ref[lead:CONV_HALO, :] = buf_ref[0]

    x_ref[CONV_HALO:, :] = glu_ref[0]
    rc = min(tt, CONV_ROWS)
    for r0 in range(0, tt, rc):
        acc = x_ref[pl.ds(lead + r0, rc), :] * w_ref[0:1, :] + b_ref[...]
        for k in range(1, CONV_WIDTH):
            acc = acc + x_ref[pl.ds(lead + r0 + k, rc), :] * w_ref[k:k + 1, :]
        c_ref[0, r0:r0 + rc, :] = acc

    @pl.when(t == pl.num_programs(2) - 1)
    def _():
        st_ref[0] = x_ref[pl.ds(tt + lead, CONV_BUF), :]

    x_ref[0:CONV_HALO, :] = x_ref[pl.ds(tt, CONV_HALO), :]


def _ln_silu_kernel(c_ref, par_ref, z_ref):
    conv = c_ref[...]
    mu = jnp.mean(conv, axis=-1, keepdims=True)
    c = conv - mu
    var = jnp.mean(c * c, axis=-1, keepdims=True)
    y = c * lax.rsqrt(var + EPS) * par_ref[0:1, :] + par_ref[1:2, :]
    z_ref[...] = (y * jax.nn.sigmoid(y)).astype(z_ref.dtype)


def conv_ln_silu(glu, buf, dw_w, dw_b, ln_g, ln_b, tt=256, tm=256):
    B, S, D = glu.shape
    tt = _pick(S, tt)
    cw = _pick(D, CONV_LANES)
    assert tt % 8 == 0
    conv, state = pl.pallas_call(
        functools.partial(_conv_kernel, tt=tt), grid=(B, D // cw, S // tt), name="dwconv",
        in_specs=[pl.BlockSpec((1, tt, cw), lambda b, c, t: (b, t, c)),
                  pl.BlockSpec((1, CONV_BUF, cw), lambda b, c, t: (b, 0, c)),
                  pl.BlockSpec((CONV_WIDTH, cw), lambda b, c, t: (0, c)),
                  pl.BlockSpec((1, cw), lambda b, c, t: (0, c))],
        out_specs=[pl.BlockSpec((1, tt, cw), lambda b, c, t: (b, t, c)),
                   pl.BlockSpec((1, CONV_BUF, cw), lambda b, c, t: (b, 0, c))],
        out_shape=[jax.ShapeDtypeStruct((B, S, D), F32), jax.ShapeDtypeStruct((B, CONV_BUF, D), F32)],
        scratch_shapes=[pltpu.VMEM((CONV_HALO + tt, cw), F32)],
        compiler_params=pltpu.CompilerParams(dimension_semantics=("parallel", "parallel", "arbitrary"),
                                             vmem_limit_bytes=V7X_VMEM_LIMIT_BYTES),
    )(glu, buf, dw_w, dw_b.reshape(1, D))
    M = B * S
    tm = _pick(M, tm)
    z = pl.pallas_call(
        _ln_silu_kernel, grid=(M // tm,), name="ln_silu",
        in_specs=[pl.BlockSpec((tm, D), lambda m: (m, 0)), pl.BlockSpec((2, D), lambda m: (0, 0))],
        out_specs=pl.BlockSpec((tm, D), lambda m: (m, 0)),
        out_shape=jax.ShapeDtypeStruct((M, D), BF16),
        compiler_params=pltpu.CompilerParams(dimension_semantics=("parallel",),
                                             vmem_limit_bytes=V7X_VMEM_LIMIT_BYTES),
    )(conv.reshape(M, D), jnp.stack([ln_g, ln_b]))
    return z, state


def rel_bucket(dist):
    n = jnp.maximum(dist, 0)
    max_exact = REL_BUCKETS // 2
    nf = jnp.maximum(n, 1).astype(jnp.float32)
    large = max_exact + (jnp.log(nf / max_exact) / math.log(REL_MAX_DIST / max_exact)
                         * (REL_BUCKETS - max_exact)).astype(jnp.int32)
    large = jnp.minimum(large, REL_BUCKETS - 1)
    return jnp.where(n < max_exact, n, large)


def head_bias(rel_table, dist):
    b = rel_table[rel_bucket(dist)].astype(jnp.float32)
    return b.reshape(dist.shape + (KV_GROUPS, HEADS_PER_GROUP)).transpose(2, 3, 0, 1)


def nsa_project(hb, p, q_dtype):
    q2 = matmul(hb, p["w_q"], out_dtype=q_dtype)
    kvs = [matmul(hb, p["w_kv"][i]) for i in range(N_BRANCHES)]
    gl = matmul(hb, p["w_gate"])
    return q2, kvs, gl


def nsa_prompt(hb, B, T, p, rel_table):
    q2, kvs, gl = nsa_project(hb, p, BF16)
    ident = jnp.arange(B * T // PAGE_SIZE, dtype=jnp.int32).reshape(B, T // PAGE_SIZE)
    kc, vc = compress_paged(kvs[0], ident, p, tall=False)
    nb = T // BLOCK
    dist_c = jnp.arange(T)[:, None] - (jnp.arange(nb) * BLOCK + (BLOCK - 1))[None, :]
    o = nsa_prompt_attention(q2, kvs[1], kvs[2], kc, vc, gl,
                             toeplitz_bias(rel_table), head_bias(rel_table, dist_c), B, T)
    kv_cmp, kv_sel, kv_win = [kv.reshape(B, T, 2, KV_GROUPS, HEAD_DIM) for kv in kvs]
    return o, kv_cmp, kv_sel, kv_win[:, T - min(WINDOW, T):]


def nsa_sample(hb, B, S, pool_cmp, pool_sel, win_buf, page_table, p, rel_table):
    W = 2 * KV_WIDTH
    n_pages = page_table.shape[1]
    q2, kvs, gl = nsa_project(hb, p, F32)
    assert (n_pages * PAGE_SIZE + S) // BLOCK == n_pages * PAGE_SIZE // BLOCK
    kc, vc = compress_paged(pool_cmp.reshape(-1, HEAD_DIM), page_table, p, tall=True)
    lw = win_buf.shape[1]
    o = nsa_sample_attention(q2, kc, vc, pool_sel.reshape(-1, HEAD_DIM), page_table,
                             win_buf.reshape(-1, HEAD_DIM), kvs[1].reshape(B, S, W), kvs[2].reshape(B, S, W),
                             gl, rel_table)
    kv_cmp, kv_sel, kv_win = [kv.reshape(B, S, 2, KV_GROUPS, HEAD_DIM) for kv in kvs]
    all_win = jnp.concatenate([win_buf, kv_win], axis=1)
    return o, kv_cmp, kv_sel, all_win[:, lw + S - min(WINDOW, lw + S):]


def ffn(x, g, w_up, w_down):
    hid = matmul(rmsnorm(x, g, BF16), w_up, act="relu2", out_dtype=BF16)
    return matmul(hid, w_down, res=x, tk=2048)


def kernel(x_prompt, x_sample, state_conv, cache_kv_cmp, cache_kv_sel, state_kv_win, page_table, norm_mix_g, norm_ffn_g, norm_final_g, rel_bias_table, conv_w_in, conv_b_in, conv_dw_w, conv_dw_b, conv_ln_g, conv_ln_b, conv_w_out, conv_b_out, nsa_w_in, nsa_cmp_pos, nsa_cmp_w1, nsa_cmp_b1, nsa_cmp_w2, nsa_cmp_b2, nsa_w_out, ffn_w_up, ffn_w_down):
    Bp, T, D = x_prompt.shape
    Bs, S, _ = x_sample.shape
    xp, xs = x_prompt.reshape(Bp * T, D), x_sample.reshape(Bs * S, D)
    qd = N_HEADS * HEAD_DIM
    kvd = 2 * KV_WIDTH

    wa, wg = conv_w_in[0][:, :D].astype(BF16), conv_w_in[0][:, D:].astype(BF16)
    ba, bg = conv_b_in[0][:D], conv_b_in[0][D:]
    w_out0 = conv_w_out[0].astype(BF16)

    def conv_layer(x, B, L, buf):
        glu = matmul_glu(rmsnorm(x, norm_mix_g[0], BF16), wa, wg, ba, bg)
        z, new_buf = conv_ln_silu(glu.reshape(B, L, D), buf, conv_dw_w[0], conv_dw_b[0], conv_ln_g[0], conv_ln_b[0])
        return matmul(z, w_out0, bias=conv_b_out[0], res=x), new_buf

    xp, conv_p = conv_layer(xp, Bp, T, jnp.zeros((Bp, CONV_BUF, D), F32))
    xs, conv_s = conv_layer(xs, Bs, S, state_conv[0])
    w_up, w_dn = ffn_w_up[0].astype(BF16), ffn_w_down[0].astype(BF16)
    xp = ffn(xp, norm_ffn_g[0], w_up, w_dn)
    xs = ffn(xs, norm_ffn_g[0], w_up, w_dn)

    w_in = nsa_w_in[0]
    nbr = N_BRANCHES * HEADS_PER_GROUP
    gate_w = w_in[:, qd + N_BRANCHES * kvd:].reshape(D, N_BRANCHES, KV_GROUPS, HEADS_PER_GROUP)
    gate_w = gate_w.transpose(0, 2, 1, 3).reshape(D, KV_GROUPS, nbr)
    gate_w = jnp.pad(gate_w, ((0, 0), (0, 0), (0, 128 - nbr))).reshape(D, KV_GROUPS * 128)
    npar = dict(w_q=w_in[:, :qd].astype(BF16),
                w_kv=[w_in[:, qd + i * kvd: qd + (i + 1) * kvd].astype(BF16) for i in range(N_BRANCHES)],
                w_gate=gate_w.astype(BF16), pos=nsa_cmp_pos[0], w1=nsa_cmp_w1[0].astype(BF16),
                b1=nsa_cmp_b1[0], w2=nsa_cmp_w2[0].astype(BF16), b2=nsa_cmp_b2[0])
    w_out1 = nsa_w_out[0].astype(BF16)
    op, kcp, ksp, kwp = nsa_prompt(rmsnorm(xp, norm_mix_g[1], BF16), Bp, T, npar, rel_bias_table)
    os_, kcs, kss, kws = nsa_sample(rmsnorm(xs, norm_mix_g[1], BF16), Bs, S, cache_kv_cmp[0], cache_kv_sel[0],
                                    state_kv_win[0], page_table, npar, rel_bias_table)
    xp = matmul(op, w_out1, res=xp)
    xs = matmul(os_.astype(BF16), w_out1, res=xs)
    w_up, w_dn = ffn_w_up[1].astype(BF16), ffn_w_down[1].astype(BF16)
    xp = ffn(xp, norm_ffn_g[1], w_up, w_dn)
    xs = ffn(xs, norm_ffn_g[1], w_up, w_dn)

    y_prompt = rmsnorm(xp, norm_final_g, F32).reshape(Bp, T, D)
    y_sample = rmsnorm(xs, norm_final_g, F32).reshape(Bs, S, D)
    return (y_prompt, y_sample, conv_p[None], conv_s[None], kcp[None], kcs[None],
            ksp[None], kss[None], kwp[None], kws[None])
```

```python
import functools
import math

import jax
import jax.numpy as jnp
import numpy as np
from jax import lax
from jax.experimental import pallas as pl
from jax.experimental.pallas import tpu as pltpu

D_MODEL = 4096
CONV_WIDTH = 31
CONV_BUF = CONV_WIDTH - 1
N_HEADS = 32
HEAD_DIM = D_MODEL // N_HEADS
KV_GROUPS = 4
HEADS_PER_GROUP = N_HEADS // KV_GROUPS
KV_WIDTH = KV_GROUPS * HEAD_DIM
BLOCK = 64
N_SELECT = 16
WINDOW = 512
WIN_Q_BLOCK = 128
SEL_Q_CHUNK = 16
N_BRANCHES = 3
PAGE_SIZE = 128
REL_BUCKETS = 32
REL_MAX_DIST = 128
EPS = 1e-6
NEG_INF = -1e30

V7X_VMEM_LIMIT_BYTES = 56 * 1024 * 1024
BF16 = jnp.bfloat16
F32 = jnp.float32


def _apply_act(y, act):
    if act == "relu2":
        return jnp.square(jnp.maximum(y, 0.0))
    if act == "silu":
        return y * jax.nn.sigmoid(y)
    assert act is None
    return y


def _mm_kernel(*refs, act, has_bias, has_res, has_tall, nk):
    x_ref, w_ref = refs[0], refs[1]
    pos = 2
    b_ref = r_ref = None
    if has_bias:
        b_ref = refs[pos]
        pos += 1
    if has_res:
        r_ref = refs[pos]
        pos += 1
    o_ref = refs[pos]
    pos += 1
    tall_ref = None
    if has_tall:
        tall_ref = refs[pos]
        pos += 1
    acc_ref = refs[pos] if nk > 1 else None

    def epilogue(y):
        if has_bias:
            y = y + b_ref[...]
        y = _apply_act(y, act)
        if has_res:
            y = y + r_ref[...]
        o_ref[...] = y.astype(o_ref.dtype)
        if has_tall:
            n_slabs = y.shape[1] // HEAD_DIM
            for c in range(n_slabs):
                tall_ref[pl.ds(c, y.shape[0], stride=n_slabs), :] = y[:, c * HEAD_DIM:(c + 1) * HEAD_DIM]

    part = jnp.dot(x_ref[...], w_ref[...], preferred_element_type=F32)
    if nk == 1:
        epilogue(part)
        return
    k = pl.program_id(2)

    @pl.when(k == 0)
    def _():
        acc_ref[...] = part

    @pl.when(jnp.logical_and(k > 0, k < nk - 1))
    def _():
        acc_ref[...] += part

    @pl.when(k == nk - 1)
    def _():
        epilogue(acc_ref[...] + part)


def _pick(dim, pref):
    t = min(dim, pref)
    assert dim % t == 0, (dim, pref)
    return t


def matmul(x, w, bias=None, res=None, act=None, out_dtype=F32, tm=1024, tn=1024, tk=4096, tall=False):
    M, K = x.shape
    K2, N = w.shape
    assert K == K2
    tm, tn, tk = _pick(M, tm), _pick(N, tn), _pick(K, tk)
    nk = K // tk
    in_specs = [pl.BlockSpec((tm, tk), lambda m, n, k: (m, k)),
                pl.BlockSpec((tk, tn), lambda m, n, k: (k, n))]
    args = [x, w]
    if bias is not None:
        in_specs.append(pl.BlockSpec((1, tn), lambda m, n, k: (0, n)))
        args.append(bias.reshape(1, N).astype(F32))
    if res is not None:
        in_specs.append(pl.BlockSpec((tm, tn), lambda m, n, k: (m, n)))
        args.append(res)
    scratch = [pltpu.VMEM((tm, tn), F32)] if nk > 1 else []
    out_specs = pl.BlockSpec((tm, tn), lambda m, n, k: (m, n))
    out_shape = jax.ShapeDtypeStruct((M, N), out_dtype)
    if tall:
        assert tn == N and N % HEAD_DIM == 0
        slabs = N // HEAD_DIM
        out_specs = [out_specs, pl.BlockSpec((tm * slabs, HEAD_DIM), lambda m, n, k: (m, 0))]
        out_shape = [out_shape, jax.ShapeDtypeStruct((M * slabs, HEAD_DIM), F32)]
    return pl.pallas_call(
        functools.partial(_mm_kernel, act=act, has_bias=bias is not None,
                          has_res=res is not None, has_tall=tall, nk=nk),
        grid=(M // tm, N // tn, nk),
        in_specs=in_specs,
        out_specs=out_specs,
        out_shape=out_shape,
        scratch_shapes=scratch,
        compiler_params=pltpu.CompilerParams(
            dimension_semantics=("parallel", "parallel", "arbitrary"),
            vmem_limit_bytes=V7X_VMEM_LIMIT_BYTES),
    )(*args)


def _glu_kernel(x_ref, wa_ref, wg_ref, ba_ref, bg_ref, o_ref):
    x = x_ref[...]
    a = jnp.dot(x, wa_ref[...], preferred_element_type=F32) + ba_ref[...]
    g = jnp.dot(x, wg_ref[...], preferred_element_type=F32) + bg_ref[...]
    o_ref[...] = a * jax.nn.sigmoid(g)


def matmul_glu(x, wa, wg, ba, bg, tm=1024, tn=512):
    M, K = x.shape
    N = wa.shape[1]
    tm, tn = _pick(M, tm), _pick(N, tn)
    wspec = pl.BlockSpec((K, tn), lambda m, n: (0, n))
    bspec = pl.BlockSpec((1, tn), lambda m, n: (0, n))
    return pl.pallas_call(
        _glu_kernel,
        grid=(M // tm, N // tn),
        in_specs=[pl.BlockSpec((tm, K), lambda m, n: (m, 0)), wspec, wspec, bspec, bspec],
        out_specs=pl.BlockSpec((tm, tn), lambda m, n: (m, n)),
        out_shape=jax.ShapeDtypeStruct((M, N), F32),
        compiler_params=pltpu.CompilerParams(
            dimension_semantics=("parallel", "parallel"),
            vmem_limit_bytes=V7X_VMEM_LIMIT_BYTES),
    )(x, wa, wg, ba.reshape(1, N), bg.reshape(1, N))


TQ = 128
SEL_KT = 4 * TQ
IMP_FORCED = 16.0
IMP_FUTURE = -1.0
_NT = (((1,), (1,)), ((), ()))


def _nsa_prompt_kernel(q_ref, ks_ref, vs_ref, kw_ref, vw_ref, kc_ref, vc_ref, gl_ref, toe_ref, cb_ref,
                       e_ref, o_ref, selm_ref, m_ref, l_ref, acc_ref, *, T):
    R = HEADS_PER_GROUP
    nb = T // BLOCK
    nqt = T // TQ
    i = pl.program_id(2)
    scale = HEAD_DIM ** -0.5
    qb = q_ref[...]
    qm = jnp.concatenate([qb[:, r * HEAD_DIM:(r + 1) * HEAD_DIM] for r in range(R)], axis=0)
    t_loc = lax.broadcasted_iota(jnp.int32, (TQ, 1), 0)
    q_pos = i * TQ + t_loc

    kc = kc_ref[0, 0].astype(BF16)
    vc = vc_ref[0, 0].astype(BF16)
    sc = lax.dot_general(qm, kc, _NT, preferred_element_type=F32) * scale
    sc = sc.reshape(R, TQ, nb) + cb_ref[0]
    jn = lax.broadcasted_iota(jnp.int32, (TQ, nb), 1)
    valid = (q_pos - (jn * BLOCK + (BLOCK - 1))) >= 0
    sc = jnp.where(valid[None], sc, NEG_INF)
    mc = jnp.max(sc, axis=-1, keepdims=True)
    ec = jnp.where(valid[None], jnp.exp(sc - mc), 0.0)
    lc = jnp.sum(ec, axis=-1, keepdims=True)
    pc = ec / jnp.where(lc > 0.0, lc, 1.0)
    o_cmp = jnp.dot(pc.reshape(R * TQ, nb).astype(BF16), vc, preferred_element_type=F32).reshape(R, TQ, HEAD_DIM)
    imp = pc[0]
    for r in range(1, R):
        imp = imp + pc[r]

    cur = q_pos // BLOCK
    forced = (jn == 0) | (jn == cur) | (jn == cur - 1)
    impp = jnp.where(jn > cur, IMP_FUTURE, jnp.where(forced, IMP_FORCED, imp))
    rank = jnp.zeros((TQ, nb), jnp.int32)
    for jp in range(nb):
        col = impp[:, jp:jp + 1]
        beats = (col > impp) | ((col == impp) & (jn > jp))
        rank = rank + beats.astype(jnp.int32)
    sel = (rank < min(N_SELECT, nb)).astype(BF16)
    selm = jnp.dot(sel, e_ref[...], preferred_element_type=F32)
    for st in range(T // SEL_KT):
        selm_ref[st] = selm[:, st * SEL_KT:(st + 1) * SEL_KT]

    def scores(k_t, kt0, n_sub):
        s = lax.dot_general(qm, k_t, _NT, preferred_element_type=F32) * scale
        bias = jnp.concatenate([toe_ref[0, jnp.clip(i - kt0 - c, 0, 2)] for c in range(n_sub)], axis=1)
        j_loc = lax.broadcasted_iota(jnp.int32, (1, n_sub * TQ), 1)
        dist = (i - kt0) * TQ + t_loc - j_loc
        return (s + bias).reshape(R, TQ, n_sub * TQ), dist

    m_ref[...] = jnp.full(m_ref.shape, NEG_INF, F32)
    l_ref[...] = jnp.zeros(l_ref.shape, F32)
    acc_ref[...] = jnp.zeros(acc_ref.shape, F32)
    sub = SEL_KT // TQ

    def sel_body(st, carry):
        ks = pl.multiple_of(st * SEL_KT, SEL_KT)
        s, dist = scores(ks_ref[pl.ds(ks, SEL_KT), :].astype(BF16), st * sub, sub)
        mask = (dist >= 0) & (selm_ref[st] > 0.5)
        s = jnp.where(mask[None], s, NEG_INF)
        m_old = m_ref[...]
        m_new = jnp.maximum(m_old, jnp.max(s, axis=-1, keepdims=True))
        p = jnp.exp(s - m_new)
        alpha = jnp.exp(m_old - m_new)
        l_ref[...] = alpha * l_ref[...] + jnp.sum(p, axis=-1, keepdims=True)
        pv = jnp.dot(p.reshape(R * TQ, SEL_KT).astype(BF16), vs_ref[pl.ds(ks, SEL_KT), :].astype(BF16),
                     preferred_element_type=F32)
        acc_ref[...] = alpha * acc_ref[...] + pv.reshape(R, TQ, HEAD_DIM)
        m_ref[...] = m_new
        return carry

    lax.fori_loop(0, i // sub + 1, sel_body, 0)
    o_sel = acc_ref[...] / l_ref[...]

    n_win = WINDOW // TQ + 1
    kt0 = jnp.clip(i - WINDOW // TQ, 0, nqt - n_win)
    ws = pl.multiple_of(kt0 * TQ, TQ)
    s, dist = scores(kw_ref[pl.ds(ws, n_win * TQ), :].astype(BF16), kt0, n_win)
    s = jnp.where(((dist >= 0) & (dist <= WINDOW))[None], s, NEG_INF)
    p = jnp.exp(s - jnp.max(s, axis=-1, keepdims=True))
    lw = jnp.sum(p, axis=-1, keepdims=True)
    pv = jnp.dot(p.reshape(R * TQ, n_win * TQ).astype(BF16), vw_ref[pl.ds(ws, n_win * TQ), :].astype(BF16),
                 preferred_element_type=F32)
    o_win = pv.reshape(R, TQ, HEAD_DIM) / lw

    gates = jax.nn.sigmoid(gl_ref[...])
    for r in range(R):
        o = (gates[:, r:r + 1] * o_cmp[r] + gates[:, R + r:R + r + 1] * o_sel[r]
             + gates[:, 2 * R + r:2 * R + r + 1] * o_win[r])
        o_ref[:, r * HEAD_DIM:(r + 1) * HEAD_DIM] = o.astype(o_ref.dtype)


def nsa_prompt_attention(q2, kv_sel, kv_win, kc, vc, gl, toe, cb, B, T):
    G, R = KV_GROUPS, HEADS_PER_GROUP
    nb, nqt = T // BLOCK, T // TQ
    assert T % SEL_KT == 0 and nqt >= WINDOW // TQ + 1
    expand = (np.arange(T)[None, :] // BLOCK == np.arange(nb)[:, None])
    e = jnp.asarray(expand, BF16)
    qspec = pl.BlockSpec((TQ, R * HEAD_DIM), lambda b, g, i: (b * nqt + i, g))
    kspec = pl.BlockSpec((T, HEAD_DIM), lambda b, g, i: (b, g))
    vspec = pl.BlockSpec((T, HEAD_DIM), lambda b, g, i: (b, G + g))
    cspec = pl.BlockSpec((1, 1, nb, HEAD_DIM), lambda b, g, i: (b, g, 0, 0))
    return pl.pallas_call(
        functools.partial(_nsa_prompt_kernel, T=T), name="nsa_prompt_attn",
        grid=(B, G, nqt),
        in_specs=[qspec, kspec, vspec, kspec, vspec, cspec, cspec,
                  pl.BlockSpec((TQ, 128), lambda b, g, i: (b * nqt + i, g)),
                  pl.BlockSpec((1, 3, R * TQ, TQ), lambda b, g, i: (g, 0, 0, 0)),
                  pl.BlockSpec((1, R, TQ, nb), lambda b, g, i: (g, 0, i, 0)),
                  pl.BlockSpec((nb, T), lambda b, g, i: (0, 0))],
        out_specs=qspec,
        out_shape=jax.ShapeDtypeStruct((B * T, G * R * HEAD_DIM), BF16),
        scratch_shapes=[pltpu.VMEM((T // SEL_KT, TQ, SEL_KT), F32), pltpu.VMEM((R, TQ, 1), F32),
                        pltpu.VMEM((R, TQ, 1), F32), pltpu.VMEM((R, TQ, HEAD_DIM), F32)],
        compiler_params=pltpu.CompilerParams(
            dimension_semantics=("parallel", "parallel", "arbitrary"),
            vmem_limit_bytes=V7X_VMEM_LIMIT_BYTES),
    )(q2, kv_sel, kv_sel, kv_win, kv_win, kc, vc, gl, toe, cb, e)


def toeplitz_bias(rel_table):
    d = jnp.arange(3)[:, None, None] * TQ + jnp.arange(TQ)[None, :, None] - jnp.arange(TQ)[None, None, :]
    b = bucket_bias(rel_table, d)
    b = b.reshape(3, TQ, TQ, KV_GROUPS, HEADS_PER_GROUP).transpose(3, 0, 4, 1, 2)
    return b.reshape(KV_GROUPS, 3, HEADS_PER_GROUP * TQ, TQ)


NEW_PAD = TQ
PAGES_PER_STEP = 4
SEL_DONE = -3.0
SEL_PAD = -2.0


BLOCKIFY_PAGES = 8
KV_ROWS = 2 * KV_GROUPS


def _blockify_kernel(pt_ref, *refs, tall):
    del pt_ref
    NP = BLOCKIFY_PAGES
    pages, pos_ref, o_ref, xs_ref = refs[:NP], refs[NP], refs[NP + 1], refs[NP + 2]
    rows = pages[0].shape[0]
    for pi, pg in enumerate(pages):
        if tall:
            xs_ref[pi * rows:(pi + 1) * rows, :] = pg[...]
        else:
            for c in range(KV_ROWS):
                xs_ref[c, pi * rows:(pi + 1) * rows, :] = pg[:, c * HEAD_DIM:(c + 1) * HEAD_DIM]
    nblk = NP * PAGE_SIZE // BLOCK
    for kv in range(2):
        for g in range(KV_GROUPS):
            c = kv * KV_GROUPS + g
            for j in range(BLOCK):
                if tall:
                    v = xs_ref[pl.ds(j * KV_ROWS + c, nblk, stride=BLOCK * KV_ROWS), :]
                else:
                    v = xs_ref[c, pl.ds(j, nblk, stride=BLOCK), :]
                o_ref[kv, 0, g, :, j * HEAD_DIM:(j + 1) * HEAD_DIM] = (v + pos_ref[kv, j:j + 1, :]).astype(o_ref.dtype)


def blockify(pool, page_table, pos, tall):
    B, n_pages = page_table.shape
    NP = BLOCKIFY_PAGES
    assert n_pages % NP == 0
    blk = (PAGE_SIZE * KV_ROWS, HEAD_DIM) if tall else (PAGE_SIZE, KV_ROWS * HEAD_DIM)
    page_specs = [pl.BlockSpec(blk, functools.partial(lambda b, s, pt, k: (pt[b, s * NP + k], 0), k=k))
                  for k in range(NP)]
    nbs = NP * PAGE_SIZE // BLOCK
    grid_spec = pltpu.PrefetchScalarGridSpec(
        num_scalar_prefetch=1, grid=(B, n_pages // NP),
        in_specs=page_specs + [pl.BlockSpec((2, BLOCK, HEAD_DIM), lambda b, s, pt: (0, 0, 0))],
        out_specs=pl.BlockSpec((2, 1, KV_GROUPS, nbs, BLOCK * HEAD_DIM), lambda b, s, pt: (0, b, 0, s, 0)),
        scratch_shapes=[pltpu.VMEM((NP * PAGE_SIZE * KV_ROWS, HEAD_DIM) if tall else
                                   (KV_ROWS, NP * PAGE_SIZE, HEAD_DIM), F32)])
    return pl.pallas_call(
        functools.partial(_blockify_kernel, tall=tall), grid_spec=grid_spec, name="blockify",
        out_shape=jax.ShapeDtypeStruct((2, B, KV_GROUPS, n_pages * PAGE_SIZE // BLOCK, BLOCK * HEAD_DIM), BF16),
        compiler_params=pltpu.CompilerParams(dimension_semantics=("parallel", "arbitrary"),
                                             vmem_limit_bytes=V7X_VMEM_LIMIT_BYTES),
    )(page_table, *([pool] * NP), pos)


def _compress_mlp_kernel(a_ref, w1_ref, b1_ref, w2_ref, b2_ref, o_ref):
    h = jnp.dot(a_ref[0], w1_ref[0], preferred_element_type=F32) + b1_ref[0]
    h = h * jax.nn.sigmoid(h)
    o_ref[0] = jnp.dot(h.astype(BF16), w2_ref[0], preferred_element_type=F32) + b2_ref[0]


def compress_mlp(a, w1, b1, w2, b2, tm=512):
    _, M, K = a.shape
    H = w1.shape[-1]
    tm = _pick(M, tm)
    return pl.pallas_call(
        _compress_mlp_kernel, grid=(2, M // tm), name="compress_mlp",
        in_specs=[pl.BlockSpec((1, tm, K), lambda k, m: (k, m, 0)),
                  pl.BlockSpec((1, K, H), lambda k, m: (k, 0, 0)),
                  pl.BlockSpec((1, 1, H), lambda k, m: (k, 0, 0)),
                  pl.BlockSpec((1, H, HEAD_DIM), lambda k, m: (k, 0, 0)),
                  pl.BlockSpec((1, 1, HEAD_DIM), lambda k, m: (k, 0, 0))],
        out_specs=pl.BlockSpec((1, tm, HEAD_DIM), lambda k, m: (k, m, 0)),
        out_shape=jax.ShapeDtypeStruct((2, M, HEAD_DIM), F32),
        compiler_params=pltpu.CompilerParams(dimension_semantics=("parallel", "parallel"),
                                             vmem_limit_bytes=V7X_VMEM_LIMIT_BYTES),
    )(a, w1, b1.reshape(2, 1, H), w2, b2.reshape(2, 1, HEAD_DIM))


def compress_paged(pool, page_table, p, tall):
    a = blockify(pool, page_table, p["pos"], tall)
    _, B, G, nb = a.shape[:4]
    out = compress_mlp(a.reshape(2, B * G * nb, BLOCK * HEAD_DIM), p["w1"], p["b1"], p["w2"], p["b2"])
    out = out.reshape(2, B, G, nb, HEAD_DIM)
    return out[0], out[1]


def _expand_rows(x8):
    return jnp.concatenate([x8] * HEADS_PER_GROUP, axis=0)


def _head_rows(q, g):
    R = HEADS_PER_GROUP
    return jnp.concatenate([q[:, (g * R + r) * HEAD_DIM:(g * R + r + 1) * HEAD_DIM] for r in range(R)], axis=0)


def _nsa_sample_select_kernel(q_ref, kc_ref, vc_ref, cb_ref, ocmp_ref, selsc_ref, sel_ref, *, S, n_pages):
    G, R = KV_GROUPS, HEADS_PER_GROUP
    RS = R * S
    nbc = kc_ref.shape[2]
    nbp = sel_ref.shape[3]
    n_blocks = -(-(n_pages * PAGE_SIZE + S) // BLOCK)
    past_len = n_pages * PAGE_SIZE
    scale = HEAD_DIM ** -0.5
    qi = lax.broadcasted_iota(jnp.int32, (RS, 1), 0) % S
    q = q_ref[...]
    jn = lax.broadcasted_iota(jnp.int32, (S, nbp), 1)
    jf = jn.astype(F32)
    cur = (past_len + lax.broadcasted_iota(jnp.int32, (S, 1), 0)) // BLOCK
    forced = (jn == 0) | (jn == cur) | (jn == cur - 1)
    jc = lax.broadcasted_iota(jnp.int32, (RS, nbc), 1)
    valid = (past_len + qi - (jc * BLOCK + (BLOCK - 1))) >= 0
    for g in range(G):
        qg = _head_rows(q, g).astype(BF16)
        sc = lax.dot_general(qg, kc_ref[0, g].astype(BF16), _NT, preferred_element_type=F32) * scale
        sc = jnp.where(valid, sc + cb_ref[g], NEG_INF)
        ec = jnp.where(valid, jnp.exp(sc - jnp.max(sc, axis=-1, keepdims=True)), 0.0)
        lc = jnp.sum(ec, axis=-1, keepdims=True)
        pc = ec / jnp.where(lc > 0.0, lc, 1.0)
        ocmp_ref[0, g] = jnp.dot(pc.astype(BF16), vc_ref[0, g].astype(BF16), preferred_element_type=F32)
        imp = pc[0:S]
        for r in range(1, R):
            imp = imp + pc[r * S:(r + 1) * S]
        imp = jnp.concatenate([imp, jnp.zeros((S, nbp - nbc), F32)], axis=1)
        work = jnp.where(jn > cur, IMP_FUTURE, jnp.where(forced, IMP_FORCED, imp))
        work = jnp.where(jn >= n_blocks, SEL_PAD, work)
        sel = jnp.zeros((S, nbp), F32)
        for _ in range(min(N_SELECT, n_blocks)):
            mx = jnp.max(work, axis=-1, keepdims=True)
            first = jnp.min(jnp.where(work == mx, jf, float(nbp)), axis=-1, keepdims=True)
            hit = jf == first
            sel = jnp.where(hit, 1.0, sel)
            work = jnp.where(hit, SEL_DONE, work)
        sel_ref[0, g] = sel
        for st in range(nbp // 8):
            selsc_ref[0, g, st] = jnp.concatenate(
                [sel[:, st * 8:(st + 1) * 8], jnp.zeros((S, 128 - 8), F32)], axis=1)


def _nsa_sample_kernel(pt_ref, fl_ref, *refs, S, n_pages, lw):
    del pt_ref
    PP = PAGES_PER_STEP
    pages = refs[:PP]
    (q_ref, ocmp_ref, selsc_ref, winb_ref, nsel_ref, nwin_ref, gl_ref, sb_ref, e_ref,
     o_ref, qm_ref, m_ref, l_ref, acc_ref) = refs[PP:]
    G, R = KV_GROUPS, HEADS_PER_GROUP
    RS = R * S
    steps = n_pages // PP
    b = pl.program_id(0)
    s_id = pl.program_id(1)
    scale = HEAD_DIM ** -0.5
    qi = lax.broadcasted_iota(jnp.int32, (RS, 1), 0) % S
    jl = lax.broadcasted_iota(jnp.int32, (1, TQ), 1)

    def attend(g, k_t, v_t, bias, mask):
        s = lax.dot_general(qm_ref[g], k_t.astype(BF16), _NT, preferred_element_type=F32) * scale + bias
        s = jnp.where(mask, s, NEG_INF)
        m_old = m_ref[g]
        m_new = jnp.maximum(m_old, jnp.max(s, axis=-1, keepdims=True))
        p = jnp.where(mask, jnp.exp(s - m_new), 0.0)
        alpha = jnp.exp(m_old - m_new)
        l_ref[g] = alpha * l_ref[g] + jnp.sum(p, axis=-1, keepdims=True)
        acc_ref[g] = alpha * acc_ref[g] + jnp.dot(p.astype(BF16), v_t.astype(BF16), preferred_element_type=F32)
        m_ref[g] = m_new

    def reset_state():
        m_ref[...] = jnp.full(m_ref.shape, NEG_INF, F32)
        l_ref[...] = jnp.zeros(l_ref.shape, F32)
        acc_ref[...] = jnp.zeros(acc_ref.shape, F32)

    @pl.when(s_id == 0)
    def _():
        reset_state()
        q = q_ref[...]
        for g in range(G):
            qm_ref[g] = _head_rows(q, g).astype(BF16)

    flags = [fl_ref[b * G + g, s_id * PP + pi] for g in range(G) for pi in range(PP)]

    @pl.when(sum(flags) > 0)
    def _():
        bias_tbl = [jnp.minimum(n_pages - (s_id * PP + pi), 2) for pi in range(PP)]
        for g in range(G):
            selm = jnp.dot(selsc_ref[0, g, s_id].astype(BF16), e_ref[...], preferred_element_type=F32)
            k_t = jnp.concatenate([pg[pl.ds(g, PAGE_SIZE, stride=KV_ROWS), :] for pg in pages], axis=0)
            v_t = jnp.concatenate([pg[pl.ds(G + g, PAGE_SIZE, stride=KV_ROWS), :] for pg in pages], axis=0)
            bias = jnp.concatenate([sb_ref[g, t] for t in bias_tbl], axis=1)
            attend(g, k_t, v_t, bias, _expand_rows(selm) > 0.5)

    @pl.when(s_id == steps - 1)
    def _():
        dist_new = qi - jl
        o_sel, o_win = [], []
        for g in range(G):
            selm = jnp.dot(selsc_ref[0, g, steps].astype(BF16), e_ref[...], preferred_element_type=F32)
            mask = (_expand_rows(selm[:, :TQ]) > 0.5) & (dist_new >= 0)
            attend(g, nsel_ref[0, :, g * HEAD_DIM:(g + 1) * HEAD_DIM],
                   nsel_ref[0, :, (G + g) * HEAD_DIM:(G + g + 1) * HEAD_DIM], sb_ref[g, 0], mask)
            o_sel.append(acc_ref[g] / l_ref[g])
        reset_state()
        for g in range(G):
            for t in range(lw // TQ):
                base = lw - t * TQ
                dist = base + qi - jl
                attend(g, winb_ref[pl.ds(t * TQ * KV_ROWS + g, TQ, stride=KV_ROWS), :],
                       winb_ref[pl.ds(t * TQ * KV_ROWS + G + g, TQ, stride=KV_ROWS), :],
                       sb_ref[g, min(base // TQ, 2)], (dist >= 0) & (dist <= WINDOW))
            attend(g, nwin_ref[0, :, g * HEAD_DIM:(g + 1) * HEAD_DIM],
                   nwin_ref[0, :, (G + g) * HEAD_DIM:(G + g + 1) * HEAD_DIM], sb_ref[g, 0],
                   (dist_new >= 0) & (dist_new <= WINDOW))
            o_win.append(acc_ref[g] / l_ref[g])
        gates = jax.nn.sigmoid(gl_ref[...])
        for g in range(G):
            oc = ocmp_ref[0, g]
            for r in range(R):
                c = g * 128 + r
                rows = slice(r * S, (r + 1) * S)
                o = (gates[:, c:c + 1] * oc[rows] + gates[:, c + R:c + R + 1] * o_sel[g][rows]
                     + gates[:, c + 2 * R:c + 2 * R + 1] * o_win[g][rows])
                o_ref[:, (g * R + r) * HEAD_DIM:(g * R + r + 1) * HEAD_DIM] = o


def sample_bias_tables(rel_table, S, past_len, nbc):
    G, R = KV_GROUPS, HEADS_PER_GROUP
    d = jnp.arange(3)[:, None, None] * TQ + jnp.arange(S)[None, :, None] - jnp.arange(TQ)[None, None, :]
    b = bucket_bias(rel_table, d).reshape(3, S, TQ, G, R)
    sb = b.transpose(3, 0, 4, 1, 2).reshape(G, 3, R * S, TQ)
    dc = past_len + jnp.arange(S)[:, None] - (jnp.arange(nbc) * BLOCK + (BLOCK - 1))[None, :]
    cb = head_bias(rel_table, dc).reshape(G, R * S, nbc)
    return sb, cb


def nsa_sample_attention(q, kc, vc, pool_sel, page_table, win_buf, new_sel, new_win, gl, rel_table):
    G, R = KV_GROUPS, HEADS_PER_GROUP
    B, n_pages = page_table.shape
    S = new_sel.shape[1]
    lw = win_buf.shape[0] // (B * KV_ROWS)
    W = 2 * KV_WIDTH
    nbc = kc.shape[2]
    PP = PAGES_PER_STEP
    steps = n_pages // PP
    assert n_pages % PP == 0 and lw % TQ == 0 and S <= BLOCK and PP * PAGE_SIZE // BLOCK == 8
    sb, cb = sample_bias_tables(rel_table, S, n_pages * PAGE_SIZE, nbc)
    nbp = (steps + 1) * 8

    brow = lambda shape: pl.BlockSpec(shape, lambda b: (b,) + (0,) * (len(shape) - 1))
    ocmp, selsc, sel = pl.pallas_call(
        functools.partial(_nsa_sample_select_kernel, S=S, n_pages=n_pages), grid=(B,), name="nsa_sample_select",
        in_specs=[brow((S, G * R * HEAD_DIM)), brow((1, G, nbc, HEAD_DIM)), brow((1, G, nbc, HEAD_DIM)),
                  pl.BlockSpec((G, R * S, nbc), lambda b: (0, 0, 0))],
        out_specs=[brow((1, G, R * S, HEAD_DIM)), brow((1, G, steps + 1, S, 128)), brow((1, G, S, nbp))],
        out_shape=[jax.ShapeDtypeStruct((B, G, R * S, HEAD_DIM), F32),
                   jax.ShapeDtypeStruct((B, G, steps + 1, S, 128), F32),
                   jax.ShapeDtypeStruct((B, G, S, nbp), F32)],
        compiler_params=pltpu.CompilerParams(dimension_semantics=("parallel",),
                                             vmem_limit_bytes=V7X_VMEM_LIMIT_BYTES),
    )(q, kc, vc, cb)
    bpp = PAGE_SIZE // BLOCK
    flags = sel[..., :n_pages * bpp].reshape(B, G, S, n_pages, bpp).max(axis=(2, 4))
    flags = (flags > 0.5).astype(jnp.int32).reshape(B * G, n_pages)

    pad = ((0, 0), (0, NEW_PAD - S), (0, 0))
    new_sel, new_win = jnp.pad(new_sel, pad), jnp.pad(new_win, pad)
    expand = np.zeros((128, PP * PAGE_SIZE), np.float32)
    expand[np.arange(PP * PAGE_SIZE) // BLOCK, np.arange(PP * PAGE_SIZE)] = 1.0
    e = jnp.asarray(expand, BF16)
    page_specs = [pl.BlockSpec((PAGE_SIZE * KV_ROWS, HEAD_DIM), functools.partial(
        lambda b, s, pt, fl, k: (pt[b, s * PP + k], 0), k=k)) for k in range(PP)]
    row = lambda shape: pl.BlockSpec(shape, lambda b, s, pt, fl: (b,) + (0,) * (len(shape) - 1))
    const = lambda shape: pl.BlockSpec(shape, lambda b, s, pt, fl: (0,) * len(shape))
    grid_spec = pltpu.PrefetchScalarGridSpec(
        num_scalar_prefetch=2, grid=(B, steps),
        in_specs=page_specs + [row((S, G * R * HEAD_DIM)), row((1, G, R * S, HEAD_DIM)),
                               row((1, G, steps + 1, S, 128)), row((lw * KV_ROWS, HEAD_DIM)),
                               row((1, NEW_PAD, W)), row((1, NEW_PAD, W)), row((S, G * 128)),
                               const((G, 3, R * S, TQ)), const((128, PP * PAGE_SIZE))],
        out_specs=row((S, G * R * HEAD_DIM)),
        scratch_shapes=[pltpu.VMEM((G, R * S, HEAD_DIM), BF16), pltpu.VMEM((G, R * S, 1), F32),
                        pltpu.VMEM((G, R * S, 1), F32), pltpu.VMEM((G, R * S, HEAD_DIM), F32)])
    return pl.pallas_call(
        functools.partial(_nsa_sample_kernel, S=S, n_pages=n_pages, lw=lw), grid_spec=grid_spec,
        name="nsa_sample_attn",
        out_shape=jax.ShapeDtypeStruct((B * S, G * R * HEAD_DIM), F32),
        compiler_params=pltpu.CompilerParams(dimension_semantics=("parallel", "arbitrary"),
                                             vmem_limit_bytes=V7X_VMEM_LIMIT_BYTES),
    )(page_table, flags, *([pool_sel] * PP), q, ocmp, selsc, win_buf, new_sel, new_win, gl, sb, e)


CONV_HALO = 32


def _rmsnorm_kernel(x_ref, g_ref, o_ref):
    x = x_ref[...]
    y = x * lax.rsqrt(jnp.mean(x * x, axis=-1, keepdims=True) + EPS) * g_ref[...]
    o_ref[...] = y.astype(o_ref.dtype)


def rmsnorm(x, g, out_dtype, tm=256):
    M, D = x.shape
    tm = _pick(M, tm)
    return pl.pallas_call(
        _rmsnorm_kernel, grid=(M // tm,), name="rmsnorm",
        in_specs=[pl.BlockSpec((tm, D), lambda m: (m, 0)), pl.BlockSpec((1, D), lambda m: (0, 0))],
        out_specs=pl.BlockSpec((tm, D), lambda m: (m, 0)),
        out_shape=jax.ShapeDtypeStruct((M, D), out_dtype),
        compiler_params=pltpu.CompilerParams(dimension_semantics=("parallel",),
                                             vmem_limit_bytes=V7X_VMEM_LIMIT_BYTES),
    )(x, g.reshape(1, D))


CONV_LANES = 512
CONV_ROWS = 64


def _conv_kernel(glu_ref, buf_ref, w_ref, b_ref, c_ref, st_ref, x_ref, *, tt):
    t = pl.program_id(2)
    lead = CONV_HALO - CONV_BUF

    @pl.when(t == 0)
    def _():
        x_ref[0:lead, :] = jnp.zeros((lead, x_ref.shape[1]), F32)
        x_ref[lead:CONV_HALO, :] = buf_ref[0]

    x_ref[CONV_HALO:, :] = glu_ref[0]
    rc = min(tt, CONV_ROWS)
    for r0 in range(0, tt, rc):
        acc = x_ref[pl.ds(lead + r0, rc), :] * w_ref[0:1, :] + b_ref[...]
        for k in range(1, CONV_WIDTH):
            acc = acc + x_ref[pl.ds(lead + r0 + k, rc), :] * w_ref[k:k + 1, :]
        c_ref[0, r0:r0 + rc, :] = acc

    @pl.when(t == pl.num_programs(2) - 1)
    def _():
        st_ref[0] = x_ref[pl.ds(tt + lead, CONV_BUF), :]

    x_ref[0:CONV_HALO, :] = x_ref[pl.ds(tt, CONV_HALO), :]


def _ln_silu_kernel(c_ref, par_ref, z_ref):
    conv = c_ref[...]
    mu = jnp.mean(conv, axis=-1, keepdims=True)
    c = conv - mu
    var = jnp.mean(c * c, axis=-1, keepdims=True)
    y = c * lax.rsqrt(var + EPS) * par_ref[0:1, :] + par_ref[1:2, :]
    z_ref[...] = (y * jax.nn.sigmoid(y)).astype(z_ref.dtype)


def conv_ln_silu(glu, buf, dw_w, dw_b, ln_g, ln_b, tt=256, tm=256):
    B, S, D = glu.shape
    tt = _pick(S, tt)
    cw = _pick(D, CONV_LANES)
    assert tt % 8 == 0
    conv, state = pl.pallas_call(
        functools.partial(_conv_kernel, tt=tt), grid=(B, D // cw, S // tt), name="dwconv",
        in_specs=[pl.BlockSpec((1, tt, cw), lambda b, c, t: (b, t, c)),
                  pl.BlockSpec((1, CONV_BUF, cw), lambda b, c, t: (b, 0, c)),
                  pl.BlockSpec((CONV_WIDTH, cw), lambda b, c, t: (0, c)),
                  pl.BlockSpec((1, cw), lambda b, c, t: (0, c))],
        out_specs=[pl.BlockSpec((1, tt, cw), lambda b, c, t: (b, t, c)),
                   pl.BlockSpec((1, CONV_BUF, cw), lambda b, c, t: (b, 0, c))],
        out_shape=[jax.ShapeDtypeStruct((B, S, D), F32), jax.ShapeDtypeStruct((B, CONV_BUF, D), F32)],
        scratch_shapes=[pltpu.VMEM((CONV_HALO + tt, cw), F32)],
        compiler_params=pltpu.CompilerParams(dimension_semantics=("parallel", "parallel", "arbitrary"),
                                             vmem_limit_bytes=V7X_VMEM_LIMIT_BYTES),
    )(glu, buf, dw_w, dw_b.reshape(1, D))
    M = B * S
    tm = _pick(M, tm)
    z = pl.pallas_call(
        _ln_silu_kernel, grid=(M // tm,), name="ln_silu",
        in_specs=[pl.BlockSpec((tm, D), lambda m: (m, 0)), pl.BlockSpec((2, D), lambda m: (0, 0))],
        out_specs=pl.BlockSpec((tm, D), lambda m: (m, 0)),
        out_shape=jax.ShapeDtypeStruct((M, D), BF16),
        compiler_params=pltpu.CompilerParams(dimension_semantics=("parallel",),
                                             vmem_limit_bytes=V7X_VMEM_LIMIT_BYTES),
    )(conv.reshape(M, D), jnp.stack([ln_g, ln_b]))
    return z, state


def rel_bucket(dist):
    n = jnp.maximum(dist, 0)
    max_exact = REL_BUCKETS // 2
    nf = jnp.maximum(n, 1).astype(jnp.float32)
    large = max_exact + (jnp.log(nf / max_exact) / math.log(REL_MAX_DIST / max_exact)
                         * (REL_BUCKETS - max_exact)).astype(jnp.int32)
    large = jnp.minimum(large, REL_BUCKETS - 1)
    return jnp.where(n < max_exact, n, large)


def bucket_bias(rel_table, dist):
    onehot = jax.nn.one_hot(rel_bucket(dist), REL_BUCKETS, dtype=F32)
    return jnp.dot(onehot, rel_table.astype(F32), precision=lax.Precision.HIGHEST)


def head_bias(rel_table, dist):
    b = bucket_bias(rel_table, dist)
    return b.reshape(dist.shape + (KV_GROUPS, HEADS_PER_GROUP)).transpose(2, 3, 0, 1)


def nsa_project(hb, p, q_dtype):
    q2 = matmul(hb, p["w_q"], out_dtype=q_dtype)
    kv_pairs = [matmul(hb, p["w_kv"][i], tall=True) for i in range(N_BRANCHES)]
    gl = matmul(hb, p["w_gate"])
    return q2, [w for w, _ in kv_pairs], [t for _, t in kv_pairs], gl


def nsa_prompt(hb, B, T, p, rel_table):
    q2, kvs, kvt, gl = nsa_project(hb, p, BF16)
    ident = jnp.arange(B * T // PAGE_SIZE, dtype=jnp.int32).reshape(B, T // PAGE_SIZE)
    kc, vc = compress_paged(kvs[0], ident, p, tall=False)
    nb = T // BLOCK
    dist_c = jnp.arange(T)[:, None] - (jnp.arange(nb) * BLOCK + (BLOCK - 1))[None, :]
    o = nsa_prompt_attention(q2, kvs[1], kvs[2], kc, vc, gl,
                             toeplitz_bias(rel_table), head_bias(rel_table, dist_c), B, T)
    kv_cmp, kv_sel, kv_win = [kv.reshape(B, T, 2, KV_GROUPS, HEAD_DIM) for kv in kvt]
    return o, kv_cmp, kv_sel, kv_win[:, T - min(WINDOW, T):]


def nsa_sample(hb, B, S, pool_cmp, pool_sel, win_buf, page_table, p, rel_table):
    W = 2 * KV_WIDTH
    n_pages = page_table.shape[1]
    q2, kvs, kvt, gl = nsa_project(hb, p, F32)
    assert (n_pages * PAGE_SIZE + S) // BLOCK == n_pages * PAGE_SIZE // BLOCK
    kc, vc = compress_paged(pool_cmp.reshape(-1, HEAD_DIM), page_table, p, tall=True)
    lw = win_buf.shape[1]
    o = nsa_sample_attention(q2, kc, vc, pool_sel.reshape(-1, HEAD_DIM), page_table,
                             win_buf.reshape(-1, HEAD_DIM), kvs[1].reshape(B, S, W), kvs[2].reshape(B, S, W),
                             gl, rel_table)
    kv_cmp, kv_sel, kv_win = [kv.reshape(B, S, 2, KV_GROUPS, HEAD_DIM) for kv in kvt]
    all_win = jnp.concatenate([win_buf, kv_win], axis=1)
    return o, kv_cmp, kv_sel, all_win[:, lw + S - min(WINDOW, lw + S):]


def ffn(x, g, w_up, w_down):
    hid = matmul(rmsnorm(x, g, BF16), w_up, act="relu2", out_dtype=BF16)
    return matmul(hid, w_down, res=x, tk=2048)


def kernel(x_prompt, x_sample, state_conv, cache_kv_cmp, cache_kv_sel, state_kv_win, page_table, norm_mix_g, norm_ffn_g, norm_final_g, rel_bias_table, conv_w_in, conv_b_in, conv_dw_w, conv_dw_b, conv_ln_g, conv_ln_b, conv_w_out, conv_b_out, nsa_w_in, nsa_cmp_pos, nsa_cmp_w1, nsa_cmp_b1, nsa_cmp_w2, nsa_cmp_b2, nsa_w_out, ffn_w_up, ffn_w_down):
    Bp, T, D = x_prompt.shape
    Bs, S, _ = x_sample.shape
    xp, xs = x_prompt.reshape(Bp * T, D), x_sample.reshape(Bs * S, D)
    qd = N_HEADS * HEAD_DIM
    kvd = 2 * KV_WIDTH

    wa, wg = conv_w_in[0][:, :D].astype(BF16), conv_w_in[0][:, D:].astype(BF16)
    ba, bg = conv_b_in[0][:D], conv_b_in[0][D:]
    w_out0 = conv_w_out[0].astype(BF16)

    def conv_layer(x, B, L, buf):
        glu = matmul_glu(rmsnorm(x, norm_mix_g[0], BF16), wa, wg, ba, bg)
        z, new_buf = conv_ln_silu(glu.reshape(B, L, D), buf, conv_dw_w[0], conv_dw_b[0], conv_ln_g[0], conv_ln_b[0])
        return matmul(z, w_out0, bias=conv_b_out[0], res=x), new_buf

    xp, conv_p = conv_layer(xp, Bp, T, jnp.zeros((Bp, CONV_BUF, D), F32))
    xs, conv_s = conv_layer(xs, Bs, S, state_conv[0])
    w_up, w_dn = ffn_w_up[0].astype(BF16), ffn_w_down[0].astype(BF16)
    xp = ffn(xp, norm_ffn_g[0], w_up, w_dn)
    xs = ffn(xs, norm_ffn_g[0], w_up, w_dn)

    w_in = nsa_w_in[0]
    nbr = N_BRANCHES * HEADS_PER_GROUP
    gate_w = w_in[:, qd + N_BRANCHES * kvd:].reshape(D, N_BRANCHES, KV_GROUPS, HEADS_PER_GROUP)
    gate_w = gate_w.transpose(0, 2, 1, 3).reshape(D, KV_GROUPS, nbr)
    gate_w = jnp.pad(gate_w, ((0, 0), (0, 0), (0, 128 - nbr))).reshape(D, KV_GROUPS * 128)
    npar = dict(w_q=w_in[:, :qd].astype(BF16),
                w_kv=[w_in[:, qd + i * kvd: qd + (i + 1) * kvd].astype(BF16) for i in range(N_BRANCHES)],
                w_gate=gate_w.astype(BF16), pos=nsa_cmp_pos[0], w1=nsa_cmp_w1[0].astype(BF16),
                b1=nsa_cmp_b1[0], w2=nsa_cmp_w2[0].astype(BF16), b2=nsa_cmp_b2[0])
    w_out1 = nsa_w_out[0].astype(BF16)
    op, kcp, ksp, kwp = nsa_prompt(rmsnorm(xp, norm_mix_g[1], BF16), Bp, T, npar, rel_bias_table)
    os_, kcs, kss, kws = nsa_sample(rmsnorm(xs, norm_mix_g[1], BF16), Bs, S, cache_kv_cmp[0], cache_kv_sel[0],
                                    state_kv_win[0], page_table, npar, rel_bias_table)
    xp = matmul(op, w_out1, res=xp)
    xs = matmul(os_.astype(BF16), w_out1, res=xs)
    w_up, w_dn = ffn_w_up[1].astype(BF16), ffn_w_down[1].astype(BF16)
    xp = ffn(xp, norm_ffn_g[1], w_up, w_dn)
    xs = ffn(xs, norm_ffn_g[1], w_up, w_dn)

    y_prompt = rmsnorm(xp, norm_final_g, F32).reshape(Bp, T, D)
    y_sample = rmsnorm(xs, norm_final_g, F32).reshape(Bs, S, D)
    return (y_prompt, y_sample, conv_p[None], conv_s[None], kcp[None], kcs[None],
            ksp[None], kss[None], kwp[None], kws[None])
```

```python
import functools
import math

import jax
import jax.numpy as jnp
import numpy as np
from jax import lax
from jax.experimental import pallas as pl
from jax.experimental.pallas import tpu as pltpu

D_MODEL = 4096
CONV_WIDTH = 31
CONV_BUF = CONV_WIDTH - 1
N_HEADS = 32
HEAD_DIM = D_MODEL // N_HEADS
KV_GROUPS = 4
HEADS_PER_GROUP = N_HEADS // KV_GROUPS
KV_WIDTH = KV_GROUPS * HEAD_DIM
BLOCK = 64
N_SELECT = 16
WINDOW = 512
WIN_Q_BLOCK = 128
SEL_Q_CHUNK = 16
N_BRANCHES = 3
PAGE_SIZE = 128
REL_BUCKETS = 32
REL_MAX_DIST = 128
EPS = 1e-6
NEG_INF = -1e30

V7X_VMEM_LIMIT_BYTES = 56 * 1024 * 1024
BF16 = jnp.bfloat16
F32 = jnp.float32


def _apply_act(y, act):
    if act == "relu2":
        return jnp.square(jnp.maximum(y, 0.0))
    if act == "silu":
        return y * jax.nn.sigmoid(y)
    assert act is None
    return y


def _mm_kernel(*refs, act, has_bias, has_res, has_tall, nk):
    x_ref, w_ref = refs[0], refs[1]
    pos = 2
    b_ref = r_ref = None
    if has_bias:
        b_ref = refs[pos]
        pos += 1
    if has_res:
        r_ref = refs[pos]
        pos += 1
    o_ref = refs[pos]
    pos += 1
    tall_ref = None
    if has_tall:
        tall_ref = refs[pos]
        pos += 1
    acc_ref = refs[pos] if nk > 1 else None

    def epilogue(y):
        if has_bias:
            y = y + b_ref[...]
        y = _apply_act(y, act)
        if has_res:
            y = y + r_ref[...]
        o_ref[...] = y.astype(o_ref.dtype)
        if has_tall:
            n_slabs = y.shape[1] // HEAD_DIM
            for c in range(n_slabs):
                tall_ref[pl.ds(c, y.shape[0], stride=n_slabs), :] = y[:, c * HEAD_DIM:(c + 1) * HEAD_DIM]

    part = jnp.dot(x_ref[...], w_ref[...], preferred_element_type=F32)
    if nk == 1:
        epilogue(part)
        return
    k = pl.program_id(2)

    @pl.when(k == 0)
    def _():
        acc_ref[...] = part

    @pl.when(jnp.logical_and(k > 0, k < nk - 1))
    def _():
        acc_ref[...] += part

    @pl.when(k == nk - 1)
    def _():
        epilogue(acc_ref[...] + part)


def _pick(dim, pref):
    t = min(dim, pref)
    assert dim % t == 0, (dim, pref)
    return t


def matmul(x, w, bias=None, res=None, act=None, out_dtype=F32, tm=1024, tn=1024, tk=4096, tall=False):
    M, K = x.shape
    K2, N = w.shape
    assert K == K2
    tm, tn, tk = _pick(M, tm), _pick(N, tn), _pick(K, tk)
    nk = K // tk
    in_specs = [pl.BlockSpec((tm, tk), lambda m, n, k: (m, k)),
                pl.BlockSpec((tk, tn), lambda m, n, k: (k, n))]
    args = [x, w]
    if bias is not None:
        in_specs.append(pl.BlockSpec((1, tn), lambda m, n, k: (0, n)))
        args.append(bias.reshape(1, N).astype(F32))
    if res is not None:
        in_specs.append(pl.BlockSpec((tm, tn), lambda m, n, k: (m, n)))
        args.append(res)
    scratch = [pltpu.VMEM((tm, tn), F32)] if nk > 1 else []
    out_specs = pl.BlockSpec((tm, tn), lambda m, n, k: (m, n))
    out_shape = jax.ShapeDtypeStruct((M, N), out_dtype)
    if tall:
        assert tn == N and N % HEAD_DIM == 0
        slabs = N // HEAD_DIM
        out_specs = [out_specs, pl.BlockSpec((tm * slabs, HEAD_DIM), lambda m, n, k: (m, 0))]
        out_shape = [out_shape, jax.ShapeDtypeStruct((M * slabs, HEAD_DIM), F32)]
    return pl.pallas_call(
        functools.partial(_mm_kernel, act=act, has_bias=bias is not None,
                          has_res=res is not None, has_tall=tall, nk=nk),
        grid=(M // tm, N // tn, nk),
        in_specs=in_specs,
        out_specs=out_specs,
        out_shape=out_shape,
        scratch_shapes=scratch,
        compiler_params=pltpu.CompilerParams(
            dimension_semantics=("parallel", "parallel", "arbitrary"),
            vmem_limit_bytes=V7X_VMEM_LIMIT_BYTES),
    )(*args)


def _glu_kernel(x_ref, wa_ref, wg_ref, ba_ref, bg_ref, o_ref):
    x = x_ref[...]
    a = jnp.dot(x, wa_ref[...], preferred_element_type=F32) + ba_ref[...]
    g = jnp.dot(x, wg_ref[...], preferred_element_type=F32) + bg_ref[...]
    o_ref[...] = a * jax.nn.sigmoid(g)


def matmul_glu(x, wa, wg, ba, bg, tm=1024, tn=512):
    M, K = x.shape
    N = wa.shape[1]
    tm, tn = _pick(M, tm), _pick(N, tn)
    wspec = pl.BlockSpec((K, tn), lambda m, n: (0, n))
    bspec = pl.BlockSpec((1, tn), lambda m, n: (0, n))
    return pl.pallas_call(
        _glu_kernel,
        grid=(M // tm, N // tn),
        in_specs=[pl.BlockSpec((tm, K), lambda m, n: (m, 0)), wspec, wspec, bspec, bspec],
        out_specs=pl.BlockSpec((tm, tn), lambda m, n: (m, n)),
        out_shape=jax.ShapeDtypeStruct((M, N), F32),
        compiler_params=pltpu.CompilerParams(
            dimension_semantics=("parallel", "parallel"),
            vmem_limit_bytes=V7X_VMEM_LIMIT_BYTES),
    )(x, wa, wg, ba.reshape(1, N), bg.reshape(1, N))


def _mm_wcast_kernel(*refs, act, has_bias, has_res):
    x_ref, w_ref = refs[0], refs[1]
    pos = 2
    b_ref = r_ref = None
    if has_bias:
        b_ref = refs[pos]
        pos += 1
    if has_res:
        r_ref = refs[pos]
        pos += 1
    o_ref, wbf_ref = refs[pos], refs[pos + 1]

    @pl.when(pl.program_id(1) == 0)
    def _():
        wbf_ref[...] = w_ref[...].astype(BF16)

    y = jnp.dot(x_ref[...], wbf_ref[...], preferred_element_type=F32)
    if has_bias:
        y = y + b_ref[...]
    y = _apply_act(y, act)
    if has_res:
        y = y + r_ref[...]
    o_ref[...] = y.astype(o_ref.dtype)


def matmul_wcast(x, w32, layer, n_cols, bias=None, res=None, act=None, out_dtype=F32, tm=1024, tn=512):
    M, K = x.shape
    assert w32.shape[1] == K and n_cols <= w32.shape[2]
    N = n_cols
    tm, tn = _pick(M, tm), _pick(N, tn)
    in_specs = [pl.BlockSpec((tm, K), lambda n, m: (m, 0)),
                pl.BlockSpec((None, K, tn), lambda n, m: (layer, 0, n))]
    args = [x, w32]
    if bias is not None:
        in_specs.append(pl.BlockSpec((1, tn), lambda n, m: (0, n)))
        args.append(bias.reshape(1, N).astype(F32))
    if res is not None:
        in_specs.append(pl.BlockSpec((tm, tn), lambda n, m: (m, n)))
        args.append(res)
    return pl.pallas_call(
        functools.partial(_mm_wcast_kernel, act=act, has_bias=bias is not None, has_res=res is not None),
        grid=(N // tn, M // tm), name="mm_wcast",
        in_specs=in_specs,
        out_specs=[pl.BlockSpec((tm, tn), lambda n, m: (m, n)), pl.BlockSpec((K, tn), lambda n, m: (0, n))],
        out_shape=[jax.ShapeDtypeStruct((M, N), out_dtype), jax.ShapeDtypeStruct((K, N), BF16)],
        compiler_params=pltpu.CompilerParams(
            dimension_semantics=("parallel", "arbitrary"),
            vmem_limit_bytes=V7X_VMEM_LIMIT_BYTES),
    )(*args)


def _glu_wcast_kernel(x_ref, wa_ref, wg_ref, ba_ref, bg_ref, o_ref, wabf_ref, wgbf_ref):
    @pl.when(pl.program_id(1) == 0)
    def _():
        wabf_ref[...] = wa_ref[...].astype(BF16)
        wgbf_ref[...] = wg_ref[...].astype(BF16)

    x = x_ref[...]
    a = jnp.dot(x, wabf_ref[...], preferred_element_type=F32) + ba_ref[...]
    g = jnp.dot(x, wgbf_ref[...], preferred_element_type=F32) + bg_ref[...]
    o_ref[...] = a * jax.nn.sigmoid(g)


def matmul_glu_wcast(x, w32, layer, b, tm=1024, tn=256):
    M, K = x.shape
    N = w32.shape[2] // 2
    tm, tn = _pick(M, tm), _pick(N, tn)
    nt = N // tn
    wspec = pl.BlockSpec((K, tn), lambda n, m: (0, n))
    return pl.pallas_call(
        _glu_wcast_kernel, grid=(nt, M // tm), name="mm_glu_wcast",
        in_specs=[pl.BlockSpec((tm, K), lambda n, m: (m, 0)),
                  pl.BlockSpec((None, K, tn), lambda n, m: (layer, 0, n)),
                  pl.BlockSpec((None, K, tn), lambda n, m: (layer, 0, nt + n)),
                  pl.BlockSpec((1, tn), lambda n, m: (0, n)), pl.BlockSpec((1, tn), lambda n, m: (0, nt + n))],
        out_specs=[pl.BlockSpec((tm, tn), lambda n, m: (m, n)), wspec, wspec],
        out_shape=[jax.ShapeDtypeStruct((M, N), F32), jax.ShapeDtypeStruct((K, N), BF16),
                   jax.ShapeDtypeStruct((K, N), BF16)],
        compiler_params=pltpu.CompilerParams(
            dimension_semantics=("parallel", "arbitrary"),
            vmem_limit_bytes=V7X_VMEM_LIMIT_BYTES),
    )(x, w32, w32, b.reshape(1, 2 * N), b.reshape(1, 2 * N))


TQ = 128
SEL_KT = 4 * TQ
IMP_FORCED = 16.0
IMP_FUTURE = -1.0
_NT = (((1,), (1,)), ((), ()))


def _nsa_prompt_kernel(q_ref, ks_ref, vs_ref, kw_ref, vw_ref, kc_ref, vc_ref, gl_ref, toe_ref, cb_ref,
                       e_ref, o_ref, selm_ref, m_ref, l_ref, acc_ref, *, T):
    R = HEADS_PER_GROUP
    nb = T // BLOCK
    nqt = T // TQ
    i = pl.program_id(2)
    scale = HEAD_DIM ** -0.5
    qb = q_ref[...]
    qm = jnp.concatenate([qb[:, r * HEAD_DIM:(r + 1) * HEAD_DIM] for r in range(R)], axis=0)
    t_loc = lax.broadcasted_iota(jnp.int32, (TQ, 1), 0)
    q_pos = i * TQ + t_loc

    kc = kc_ref[0, 0].astype(BF16)
    vc = vc_ref[0, 0].astype(BF16)
    sc = lax.dot_general(qm, kc, _NT, preferred_element_type=F32) * scale
    sc = sc.reshape(R, TQ, nb) + cb_ref[0]
    jn = lax.broadcasted_iota(jnp.int32, (TQ, nb), 1)
    valid = (q_pos - (jn * BLOCK + (BLOCK - 1))) >= 0
    sc = jnp.where(valid[None], sc, NEG_INF)
    mc = jnp.max(sc, axis=-1, keepdims=True)
    ec = jnp.where(valid[None], jnp.exp(sc - mc), 0.0)
    lc = jnp.sum(ec, axis=-1, keepdims=True)
    pc = ec / jnp.where(lc > 0.0, lc, 1.0)
    o_cmp = jnp.dot(pc.reshape(R * TQ, nb).astype(BF16), vc, preferred_element_type=F32).reshape(R, TQ, HEAD_DIM)
    imp = pc[0]
    for r in range(1, R):
        imp = imp + pc[r]

    cur = q_pos // BLOCK
    forced = (jn == 0) | (jn == cur) | (jn == cur - 1)
    impp = jnp.where(jn > cur, IMP_FUTURE, jnp.where(forced, IMP_FORCED, imp))
    rank = jnp.zeros((TQ, nb), jnp.int32)
    for jp in range(nb):
        col = impp[:, jp:jp + 1]
        beats = (col > impp) | ((col == impp) & (jn > jp))
        rank = rank + beats.astype(jnp.int32)
    sel = (rank < min(N_SELECT, nb)).astype(BF16)
    selm = jnp.dot(sel, e_ref[...], preferred_element_type=F32)
    for st in range(T // SEL_KT):
        selm_ref[st] = selm[:, st * SEL_KT:(st + 1) * SEL_KT]

    def scores(k_t, kt0, n_sub):
        s = lax.dot_general(qm, k_t, _NT, preferred_element_type=F32) * scale
        bias = jnp.concatenate([toe_ref[0, jnp.clip(i - kt0 - c, 0, 2)] for c in range(n_sub)], axis=1)
        j_loc = lax.broadcasted_iota(jnp.int32, (1, n_sub * TQ), 1)
        dist = (i - kt0) * TQ + t_loc - j_loc
        return (s + bias).reshape(R, TQ, n_sub * TQ), dist

    m_ref[...] = jnp.full(m_ref.shape, NEG_INF, F32)
    l_ref[...] = jnp.zeros(l_ref.shape, F32)
    acc_ref[...] = jnp.zeros(acc_ref.shape, F32)
    sub = SEL_KT // TQ

    def sel_body(st, carry):
        ks = pl.multiple_of(st * SEL_KT, SEL_KT)
        s, dist = scores(ks_ref[pl.ds(ks, SEL_KT), :].astype(BF16), st * sub, sub)
        mask = (dist >= 0) & (selm_ref[st] > 0.5)
        s = jnp.where(mask[None], s, NEG_INF)
        m_old = m_ref[...]
        m_new = jnp.maximum(m_old, jnp.max(s, axis=-1, keepdims=True))
        p = jnp.exp(s - m_new)
        alpha = jnp.exp(m_old - m_new)
        l_ref[...] = alpha * l_ref[...] + jnp.sum(p, axis=-1, keepdims=True)
        pv = jnp.dot(p.reshape(R * TQ, SEL_KT).astype(BF16), vs_ref[pl.ds(ks, SEL_KT), :].astype(BF16),
                     preferred_element_type=F32)
        acc_ref[...] = alpha * acc_ref[...] + pv.reshape(R, TQ, HEAD_DIM)
        m_ref[...] = m_new
        return carry

    lax.fori_loop(0, i // sub + 1, sel_body, 0)
    o_sel = acc_ref[...] / l_ref[...]

    n_win = WINDOW // TQ + 1
    kt0 = jnp.clip(i - WINDOW // TQ, 0, nqt - n_win)
    ws = pl.multiple_of(kt0 * TQ, TQ)
    s, dist = scores(kw_ref[pl.ds(ws, n_win * TQ), :].astype(BF16), kt0, n_win)
    s = jnp.where(((dist >= 0) & (dist <= WINDOW))[None], s, NEG_INF)
    p = jnp.exp(s - jnp.max(s, axis=-1, keepdims=True))
    lw = jnp.sum(p, axis=-1, keepdims=True)
    pv = jnp.dot(p.reshape(R * TQ, n_win * TQ).astype(BF16), vw_ref[pl.ds(ws, n_win * TQ), :].astype(BF16),
                 preferred_element_type=F32)
    o_win = pv.reshape(R, TQ, HEAD_DIM) / lw

    gates = jax.nn.sigmoid(gl_ref[...])
    for r in range(R):
        o = (gates[:, r:r + 1] * o_cmp[r] + gates[:, R + r:R + r + 1] * o_sel[r]
             + gates[:, 2 * R + r:2 * R + r + 1] * o_win[r])
        o_ref[:, r * HEAD_DIM:(r + 1) * HEAD_DIM] = o.astype(o_ref.dtype)


def nsa_prompt_attention(q2, kv_sel, kv_win, kc, vc, gl, toe, cb, B, T):
    G, R = KV_GROUPS, HEADS_PER_GROUP
    nb, nqt = T // BLOCK, T // TQ
    assert T % SEL_KT == 0 and nqt >= WINDOW // TQ + 1
    expand = (np.arange(T)[None, :] // BLOCK == np.arange(nb)[:, None])
    e = jnp.asarray(expand, BF16)
    qspec = pl.BlockSpec((TQ, R * HEAD_DIM), lambda b, g, i: (b * nqt + i, g))
    kspec = pl.BlockSpec((T, HEAD_DIM), lambda b, g, i: (b, g))
    vspec = pl.BlockSpec((T, HEAD_DIM), lambda b, g, i: (b, G + g))
    cspec = pl.BlockSpec((1, 1, nb, HEAD_DIM), lambda b, g, i: (b, g, 0, 0))
    return pl.pallas_call(
        functools.partial(_nsa_prompt_kernel, T=T), name="nsa_prompt_attn",
        grid=(B, G, nqt),
        in_specs=[qspec, kspec, vspec, kspec, vspec, cspec, cspec,
                  pl.BlockSpec((TQ, 128), lambda b, g, i: (b * nqt + i, g)),
                  pl.BlockSpec((1, 3, R * TQ, TQ), lambda b, g, i: (g, 0, 0, 0)),
                  pl.BlockSpec((1, R, TQ, nb), lambda b, g, i: (g, 0, i, 0)),
                  pl.BlockSpec((nb, T), lambda b, g, i: (0, 0))],
        out_specs=qspec,
        out_shape=jax.ShapeDtypeStruct((B * T, G * R * HEAD_DIM), BF16),
        scratch_shapes=[pltpu.VMEM((T // SEL_KT, TQ, SEL_KT), F32), pltpu.VMEM((R, TQ, 1), F32),
                        pltpu.VMEM((R, TQ, 1), F32), pltpu.VMEM((R, TQ, HEAD_DIM), F32)],
        compiler_params=pltpu.CompilerParams(
            dimension_semantics=("parallel", "parallel", "arbitrary"),
            vmem_limit_bytes=V7X_VMEM_LIMIT_BYTES),
    )(q2, kv_sel, kv_sel, kv_win, kv_win, kc, vc, gl, toe, cb, e)


def toeplitz_bias(rel_table):
    d = jnp.arange(3)[:, None, None] * TQ + jnp.arange(TQ)[None, :, None] - jnp.arange(TQ)[None, None, :]
    b = bucket_bias(rel_table, d)
    b = b.reshape(3, TQ, TQ, KV_GROUPS, HEADS_PER_GROUP).transpose(3, 0, 4, 1, 2)
    return b.reshape(KV_GROUPS, 3, HEADS_PER_GROUP * TQ, TQ)


NEW_PAD = TQ
PAGES_PER_STEP = 4
SEL_DONE = -3.0
SEL_PAD = -2.0


BLOCKIFY_PAGES = 8
KV_ROWS = 2 * KV_GROUPS


def _blockify_kernel(pt_ref, *refs, tall):
    del pt_ref
    NP = BLOCKIFY_PAGES
    pages, pos_ref, o_ref, xs_ref = refs[:NP], refs[NP], refs[NP + 1], refs[NP + 2]
    rows = pages[0].shape[0]
    for pi, pg in enumerate(pages):
        if tall:
            xs_ref[pi * rows:(pi + 1) * rows, :] = pg[...]
        else:
            for c in range(KV_ROWS):
                xs_ref[c, pi * rows:(pi + 1) * rows, :] = pg[:, c * HEAD_DIM:(c + 1) * HEAD_DIM]
    nblk = NP * PAGE_SIZE // BLOCK
    for kv in range(2):
        for g in range(KV_GROUPS):
            c = kv * KV_GROUPS + g
            for j in range(BLOCK):
                if tall:
                    v = xs_ref[pl.ds(j * KV_ROWS + c, nblk, stride=BLOCK * KV_ROWS), :]
                else:
                    v = xs_ref[c, pl.ds(j, nblk, stride=BLOCK), :]
                o_ref[kv, 0, g, :, j * HEAD_DIM:(j + 1) * HEAD_DIM] = (v + pos_ref[kv, j:j + 1, :]).astype(o_ref.dtype)


def blockify(pool, page_table, pos, tall):
    B, n_pages = page_table.shape
    NP = BLOCKIFY_PAGES
    assert n_pages % NP == 0
    blk = (PAGE_SIZE * KV_ROWS, HEAD_DIM) if tall else (PAGE_SIZE, KV_ROWS * HEAD_DIM)
    page_specs = [pl.BlockSpec(blk, functools.partial(lambda b, s, pt, k: (pt[b, s * NP + k], 0), k=k))
                  for k in range(NP)]
    nbs = NP * PAGE_SIZE // BLOCK
    grid_spec = pltpu.PrefetchScalarGridSpec(
        num_scalar_prefetch=1, grid=(B, n_pages // NP),
        in_specs=page_specs + [pl.BlockSpec((2, BLOCK, HEAD_DIM), lambda b, s, pt: (0, 0, 0))],
        out_specs=pl.BlockSpec((2, 1, KV_GROUPS, nbs, BLOCK * HEAD_DIM), lambda b, s, pt: (0, b, 0, s, 0)),
        scratch_shapes=[pltpu.VMEM((NP * PAGE_SIZE * KV_ROWS, HEAD_DIM) if tall else
                                   (KV_ROWS, NP * PAGE_SIZE, HEAD_DIM), F32)])
    return pl.pallas_call(
        functools.partial(_blockify_kernel, tall=tall), grid_spec=grid_spec, name="blockify",
        out_shape=jax.ShapeDtypeStruct((2, B, KV_GROUPS, n_pages * PAGE_SIZE // BLOCK, BLOCK * HEAD_DIM), BF16),
        compiler_params=pltpu.CompilerParams(dimension_semantics=("parallel", "arbitrary"),
                                             vmem_limit_bytes=V7X_VMEM_LIMIT_BYTES),
    )(page_table, *([pool] * NP), pos)


def _compress_mlp_kernel(a_ref, w1_ref, b1_ref, w2_ref, b2_ref, o_ref):
    h = jnp.dot(a_ref[0], w1_ref[0], preferred_element_type=F32) + b1_ref[0]
    h = h * jax.nn.sigmoid(h)
    o_ref[0] = jnp.dot(h.astype(BF16), w2_ref[0], preferred_element_type=F32) + b2_ref[0]


def compress_mlp(a, w1, b1, w2, b2, tm=512):
    _, M, K = a.shape
    H = w1.shape[-1]
    tm = _pick(M, tm)
    return pl.pallas_call(
        _compress_mlp_kernel, grid=(2, M // tm), name="compress_mlp",
        in_specs=[pl.BlockSpec((1, tm, K), lambda k, m: (k, m, 0)),
                  pl.BlockSpec((1, K, H), lambda k, m: (k, 0, 0)),
                  pl.BlockSpec((1, 1, H), lambda k, m: (k, 0, 0)),
                  pl.BlockSpec((1, H, HEAD_DIM), lambda k, m: (k, 0, 0)),
                  pl.BlockSpec((1, 1, HEAD_DIM), lambda k, m: (k, 0, 0))],
        out_specs=pl.BlockSpec((1, tm, HEAD_DIM), lambda k, m: (k, m, 0)),
        out_shape=jax.ShapeDtypeStruct((2, M, HEAD_DIM), F32),
        compiler_params=pltpu.CompilerParams(dimension_semantics=("parallel", "parallel"),
                                             vmem_limit_bytes=V7X_VMEM_LIMIT_BYTES),
    )(a, w1, b1.reshape(2, 1, H), w2, b2.reshape(2, 1, HEAD_DIM))


def compress_paged(pool, page_table, p, tall):
    a = blockify(pool, page_table, p["pos"], tall)
    _, B, G, nb = a.shape[:4]
    out = compress_mlp(a.reshape(2, B * G * nb, BLOCK * HEAD_DIM), p["w1"], p["b1"], p["w2"], p["b2"])
    out = out.reshape(2, B, G, nb, HEAD_DIM)
    return out[0], out[1]


def _expand_rows(x8):
    return jnp.concatenate([x8] * HEADS_PER_GROUP, axis=0)


def _head_rows(q, g):
    R = HEADS_PER_GROUP
    return jnp.concatenate([q[:, (g * R + r) * HEAD_DIM:(g * R + r + 1) * HEAD_DIM] for r in range(R)], axis=0)


def _nsa_sample_select_kernel(q_ref, kc_ref, vc_ref, cb_ref, ocmp_ref, selsc_ref, sel_ref, *, S, n_pages):
    G, R = KV_GROUPS, HEADS_PER_GROUP
    RS = R * S
    nbc = kc_ref.shape[2]
    nbp = sel_ref.shape[3]
    n_blocks = -(-(n_pages * PAGE_SIZE + S) // BLOCK)
    past_len = n_pages * PAGE_SIZE
    scale = HEAD_DIM ** -0.5
    qi = lax.broadcasted_iota(jnp.int32, (RS, 1), 0) % S
    q = q_ref[...]
    jn = lax.broadcasted_iota(jnp.int32, (S, nbp), 1)
    jf = jn.astype(F32)
    cur = (past_len + lax.broadcasted_iota(jnp.int32, (S, 1), 0)) // BLOCK
    forced = (jn == 0) | (jn == cur) | (jn == cur - 1)
    jc = lax.broadcasted_iota(jnp.int32, (RS, nbc), 1)
    valid = (past_len + qi - (jc * BLOCK + (BLOCK - 1))) >= 0
    for g in range(G):
        qg = _head_rows(q, g).astype(BF16)
        sc = lax.dot_general(qg, kc_ref[0, g].astype(BF16), _NT, preferred_element_type=F32) * scale
        sc = jnp.where(valid, sc + cb_ref[g], NEG_INF)
        ec = jnp.where(valid, jnp.exp(sc - jnp.max(sc, axis=-1, keepdims=True)), 0.0)
        lc = jnp.sum(ec, axis=-1, keepdims=True)
        pc = ec / jnp.where(lc > 0.0, lc, 1.0)
        ocmp_ref[0, g] = jnp.dot(pc.astype(BF16), vc_ref[0, g].astype(BF16), preferred_element_type=F32)
        imp = pc[0:S]
        for r in range(1, R):
            imp = imp + pc[r * S:(r + 1) * S]
        imp = jnp.concatenate([imp, jnp.zeros((S, nbp - nbc), F32)], axis=1)
        work = jnp.where(jn > cur, IMP_FUTURE, jnp.where(forced, IMP_FORCED, imp))
        work = jnp.where(jn >= n_blocks, SEL_PAD, work)
        sel = jnp.zeros((S, nbp), F32)
        for _ in range(min(N_SELECT, n_blocks)):
            mx = jnp.max(work, axis=-1, keepdims=True)
            first = jnp.min(jnp.where(work == mx, jf, float(nbp)), axis=-1, keepdims=True)
            hit = jf == first
            sel = jnp.where(hit, 1.0, sel)
            work = jnp.where(hit, SEL_DONE, work)
        sel_ref[0, g] = sel
        for st in range(nbp // 8):
            selsc_ref[0, g, st] = jnp.concatenate(
                [sel[:, st * 8:(st + 1) * 8], jnp.zeros((S, 128 - 8), F32)], axis=1)


def _nsa_sample_kernel(pt_ref, fl_ref, *refs, S, n_pages, lw):
    del pt_ref
    PP = PAGES_PER_STEP
    pages = refs[:PP]
    (q_ref, ocmp_ref, selsc_ref, winb_ref, nsel_ref, nwin_ref, gl_ref, sb_ref, e_ref,
     o_ref, qm_ref, m_ref, l_ref, acc_ref) = refs[PP:]
    G, R = KV_GROUPS, HEADS_PER_GROUP
    RS = R * S
    steps = n_pages // PP
    b = pl.program_id(0)
    s_id = pl.program_id(1)
    scale = HEAD_DIM ** -0.5
    qi = lax.broadcasted_iota(jnp.int32, (RS, 1), 0) % S
    jl = lax.broadcasted_iota(jnp.int32, (1, TQ), 1)

    def attend(g, k_t, v_t, bias, mask):
        s = lax.dot_general(qm_ref[g], k_t.astype(BF16), _NT, preferred_element_type=F32) * scale + bias
        s = jnp.where(mask, s, NEG_INF)
        m_old = m_ref[g]
        m_new = jnp.maximum(m_old, jnp.max(s, axis=-1, keepdims=True))
        p = jnp.where(mask, jnp.exp(s - m_new), 0.0)
        alpha = jnp.exp(m_old - m_new)
        l_ref[g] = alpha * l_ref[g] + jnp.sum(p, axis=-1, keepdims=True)
        acc_ref[g] = alpha * acc_ref[g] + jnp.dot(p.astype(BF16), v_t.astype(BF16), preferred_element_type=F32)
        m_ref[g] = m_new

    def reset_state():
        m_ref[...] = jnp.full(m_ref.shape, NEG_INF, F32)
        l_ref[...] = jnp.zeros(l_ref.shape, F32)
        acc_ref[...] = jnp.zeros(acc_ref.shape, F32)

    @pl.when(s_id == 0)
    def _():
        reset_state()
        q = q_ref[...]
        for g in range(G):
            qm_ref[g] = _head_rows(q, g).astype(BF16)

    flags = [fl_ref[b * G + g, s_id * PP + pi] for g in range(G) for pi in range(PP)]

    @pl.when(sum(flags) > 0)
    def _():
        bias_tbl = [jnp.minimum(n_pages - (s_id * PP + pi), 2) for pi in range(PP)]
        for g in range(G):
            selm = jnp.dot(selsc_ref[0, g, s_id].astype(BF16), e_ref[...], preferred_element_type=F32)
            k_t = jnp.concatenate([pg[pl.ds(g, PAGE_SIZE, stride=KV_ROWS), :] for pg in pages], axis=0)
            v_t = jnp.concatenate([pg[pl.ds(G + g, PAGE_SIZE, stride=KV_ROWS), :] for pg in pages], axis=0)
            bias = jnp.concatenate([sb_ref[g, t] for t in bias_tbl], axis=1)
            attend(g, k_t, v_t, bias, _expand_rows(selm) > 0.5)

    @pl.when(s_id == steps - 1)
    def _():
        dist_new = qi - jl
        o_sel, o_win = [], []
        for g in range(G):
            selm = jnp.dot(selsc_ref[0, g, steps].astype(BF16), e_ref[...], preferred_element_type=F32)
            mask = (_expand_rows(selm[:, :TQ]) > 0.5) & (dist_new >= 0)
            attend(g, nsel_ref[0, :, g * HEAD_DIM:(g + 1) * HEAD_DIM],
                   nsel_ref[0, :, (G + g) * HEAD_DIM:(G + g + 1) * HEAD_DIM], sb_ref[g, 0], mask)
            o_sel.append(acc_ref[g] / l_ref[g])
        reset_state()
        for g in range(G):
            for t in range(lw // TQ):
                base = lw - t * TQ
                dist = base + qi - jl
                attend(g, winb_ref[pl.ds(t * TQ * KV_ROWS + g, TQ, stride=KV_ROWS), :],
                       winb_ref[pl.ds(t * TQ * KV_ROWS + G + g, TQ, stride=KV_ROWS), :],
                       sb_ref[g, min(base // TQ, 2)], (dist >= 0) & (dist <= WINDOW))
            attend(g, nwin_ref[0, :, g * HEAD_DIM:(g + 1) * HEAD_DIM],
                   nwin_ref[0, :, (G + g) * HEAD_DIM:(G + g + 1) * HEAD_DIM], sb_ref[g, 0],
                   (dist_new >= 0) & (dist_new <= WINDOW))
            o_win.append(acc_ref[g] / l_ref[g])
        gates = jax.nn.sigmoid(gl_ref[...])
        for g in range(G):
            oc = ocmp_ref[0, g]
            for r in range(R):
                c = g * 128 + r
                rows = slice(r * S, (r + 1) * S)
                o = (gates[:, c:c + 1] * oc[rows] + gates[:, c + R:c + R + 1] * o_sel[g][rows]
                     + gates[:, c + 2 * R:c + 2 * R + 1] * o_win[g][rows])
                o_ref[:, (g * R + r) * HEAD_DIM:(g * R + r + 1) * HEAD_DIM] = o


def sample_bias_tables(rel_table, S, past_len, nbc):
    G, R = KV_GROUPS, HEADS_PER_GROUP
    d = jnp.arange(3)[:, None, None] * TQ + jnp.arange(S)[None, :, None] - jnp.arange(TQ)[None, None, :]
    b = bucket_bias(rel_table, d).reshape(3, S, TQ, G, R)
    sb = b.transpose(3, 0, 4, 1, 2).reshape(G, 3, R * S, TQ)
    dc = past_len + jnp.arange(S)[:, None] - (jnp.arange(nbc) * BLOCK + (BLOCK - 1))[None, :]
    cb = head_bias(rel_table, dc).reshape(G, R * S, nbc)
    return sb, cb


def nsa_sample_attention(q, kc, vc, pool_sel, page_table, win_buf, new_sel, new_win, gl, rel_table):
    G, R = KV_GROUPS, HEADS_PER_GROUP
    B, n_pages = page_table.shape
    S = new_sel.shape[1]
    lw = win_buf.shape[0] // (B * KV_ROWS)
    W = 2 * KV_WIDTH
    nbc = kc.shape[2]
    PP = PAGES_PER_STEP
    steps = n_pages // PP
    assert n_pages % PP == 0 and lw % TQ == 0 and S <= BLOCK and PP * PAGE_SIZE // BLOCK == 8
    sb, cb = sample_bias_tables(rel_table, S, n_pages * PAGE_SIZE, nbc)
    nbp = (steps + 1) * 8

    brow = lambda shape: pl.BlockSpec(shape, lambda b: (b,) + (0,) * (len(shape) - 1))
    ocmp, selsc, sel = pl.pallas_call(
        functools.partial(_nsa_sample_select_kernel, S=S, n_pages=n_pages), grid=(B,), name="nsa_sample_select",
        in_specs=[brow((S, G * R * HEAD_DIM)), brow((1, G, nbc, HEAD_DIM)), brow((1, G, nbc, HEAD_DIM)),
                  pl.BlockSpec((G, R * S, nbc), lambda b: (0, 0, 0))],
        out_specs=[brow((1, G, R * S, HEAD_DIM)), brow((1, G, steps + 1, S, 128)), brow((1, G, S, nbp))],
        out_shape=[jax.ShapeDtypeStruct((B, G, R * S, HEAD_DIM), F32),
                   jax.ShapeDtypeStruct((B, G, steps + 1, S, 128), F32),
                   jax.ShapeDtypeStruct((B, G, S, nbp), F32)],
        compiler_params=pltpu.CompilerParams(dimension_semantics=("parallel",),
                                             vmem_limit_bytes=V7X_VMEM_LIMIT_BYTES),
    )(q, kc, vc, cb)
    bpp = PAGE_SIZE // BLOCK
    flags = sel[..., :n_pages * bpp].reshape(B, G, S, n_pages, bpp).max(axis=(2, 4))
    flags = (flags > 0.5).astype(jnp.int32).reshape(B * G, n_pages)

    pad = ((0, 0), (0, NEW_PAD - S), (0, 0))
    new_sel, new_win = jnp.pad(new_sel, pad), jnp.pad(new_win, pad)
    expand = np.zeros((128, PP * PAGE_SIZE), np.float32)
    expand[np.arange(PP * PAGE_SIZE) // BLOCK, np.arange(PP * PAGE_SIZE)] = 1.0
    e = jnp.asarray(expand, BF16)
    page_specs = [pl.BlockSpec((PAGE_SIZE * KV_ROWS, HEAD_DIM), functools.partial(
        lambda b, s, pt, fl, k: (pt[b, s * PP + k], 0), k=k)) for k in range(PP)]
    row = lambda shape: pl.BlockSpec(shape, lambda b, s, pt, fl: (b,) + (0,) * (len(shape) - 1))
    const = lambda shape: pl.BlockSpec(shape, lambda b, s, pt, fl: (0,) * len(shape))
    grid_spec = pltpu.PrefetchScalarGridSpec(
        num_scalar_prefetch=2, grid=(B, steps),
        in_specs=page_specs + [row((S, G * R * HEAD_DIM)), row((1, G, R * S, HEAD_DIM)),
                               row((1, G, steps + 1, S, 128)), row((lw * KV_ROWS, HEAD_DIM)),
                               row((1, NEW_PAD, W)), row((1, NEW_PAD, W)), row((S, G * 128)),
                               const((G, 3, R * S, TQ)), const((128, PP * PAGE_SIZE))],
        out_specs=row((S, G * R * HEAD_DIM)),
        scratch_shapes=[pltpu.VMEM((G, R * S, HEAD_DIM), BF16), pltpu.VMEM((G, R * S, 1), F32),
                        pltpu.VMEM((G, R * S, 1), F32), pltpu.VMEM((G, R * S, HEAD_DIM), F32)])
    return pl.pallas_call(
        functools.partial(_nsa_sample_kernel, S=S, n_pages=n_pages, lw=lw), grid_spec=grid_spec,
        name="nsa_sample_attn",
        out_shape=jax.ShapeDtypeStruct((B * S, G * R * HEAD_DIM), F32),
        compiler_params=pltpu.CompilerParams(dimension_semantics=("parallel", "arbitrary"),
                                             vmem_limit_bytes=V7X_VMEM_LIMIT_BYTES),
    )(page_table, flags, *([pool_sel] * PP), q, ocmp, selsc, win_buf, new_sel, new_win, gl, sb, e)


CONV_HALO = 32


def _rmsnorm_kernel(x_ref, g_ref, o_ref):
    x = x_ref[...]
    y = x * lax.rsqrt(jnp.mean(x * x, axis=-1, keepdims=True) + EPS) * g_ref[...]
    o_ref[...] = y.astype(o_ref.dtype)


def rmsnorm(x, g, out_dtype, tm=256):
    M, D = x.shape
    tm = _pick(M, tm)
    return pl.pallas_call(
        _rmsnorm_kernel, grid=(M // tm,), name="rmsnorm",
        in_specs=[pl.BlockSpec((tm, D), lambda m: (m, 0)), pl.BlockSpec((1, D), lambda m: (0, 0))],
        out_specs=pl.BlockSpec((tm, D), lambda m: (m, 0)),
        out_shape=jax.ShapeDtypeStruct((M, D), out_dtype),
        compiler_params=pltpu.CompilerParams(dimension_semantics=("parallel",),
                                             vmem_limit_bytes=V7X_VMEM_LIMIT_BYTES),
    )(x, g.reshape(1, D))


CONV_LANES = 512
CONV_ROWS = 64
SUBLANES = 8


def _conv_kernel(glu_ref, buf_ref, w_ref, b_ref, c_ref, st_ref, x_ref, *, tt):
    t = pl.program_id(2)
    lead = CONV_HALO - CONV_BUF

    @pl.when(t == 0)
    def _():
        x_ref[0:lead, :] = jnp.zeros((lead, x_ref.shape[1]), F32)
        x_ref[lead:CONV_HALO, :] = buf_ref[0]
        x_ref[CONV_HALO + tt:, :] = jnp.zeros((SUBLANES, x_ref.shape[1]), F32)

    x_ref[CONV_HALO:CONV_HALO + tt, :] = glu_ref[0]
    rc = min(tt, CONV_ROWS)
    for r0 in range(0, tt, rc):
        acc = jnp.zeros((rc, x_ref.shape[1]), F32) + b_ref[...]
        for phase in range(SUBLANES):
            taps = [k for k in range(CONV_WIDTH) if (lead + k) % SUBLANES == phase]
            part = None
            for k in taps:
                term = x_ref[pl.ds(r0 + lead + k - phase, rc + SUBLANES), :] * w_ref[k:k + 1, :]
                part = term if part is None else part + term
            acc = acc + part[phase:phase + rc, :]
        c_ref[0, r0:r0 + rc, :] = acc

    @pl.when(t == pl.num_programs(2) - 1)
    def _():
        st_ref[0] = x_ref[pl.ds(tt + lead, CONV_BUF), :]

    x_ref[0:CONV_HALO, :] = x_ref[pl.ds(tt, CONV_HALO), :]


def _ln_silu_kernel(c_ref, par_ref, z_ref):
    conv = c_ref[...]
    mu = jnp.mean(conv, axis=-1, keepdims=True)
    c = conv - mu
    var = jnp.mean(c * c, axis=-1, keepdims=True)
    y = c * lax.rsqrt(var + EPS) * par_ref[0:1, :] + par_ref[1:2, :]
    z_ref[...] = (y * jax.nn.sigmoid(y)).astype(z_ref.dtype)


def conv_ln_silu(glu, buf, dw_w, dw_b, ln_g, ln_b, tt=256, tm=256):
    B, S, D = glu.shape
    tt = _pick(S, tt)
    cw = _pick(D, CONV_LANES)
    assert tt % 8 == 0
    conv, state = pl.pallas_call(
        functools.partial(_conv_kernel, tt=tt), grid=(B, D // cw, S // tt), name="dwconv",
        in_specs=[pl.BlockSpec((1, tt, cw), lambda b, c, t: (b, t, c)),
                  pl.BlockSpec((1, CONV_BUF, cw), lambda b, c, t: (b, 0, c)),
                  pl.BlockSpec((CONV_WIDTH, cw), lambda b, c, t: (0, c)),
                  pl.BlockSpec((1, cw), lambda b, c, t: (0, c))],
        out_specs=[pl.BlockSpec((1, tt, cw), lambda b, c, t: (b, t, c)),
                   pl.BlockSpec((1, CONV_BUF, cw), lambda b, c, t: (b, 0, c))],
        out_shape=[jax.ShapeDtypeStruct((B, S, D), F32), jax.ShapeDtypeStruct((B, CONV_BUF, D), F32)],
        scratch_shapes=[pltpu.VMEM((CONV_HALO + tt + SUBLANES, cw), F32)],
        compiler_params=pltpu.CompilerParams(dimension_semantics=("parallel", "parallel", "arbitrary"),
                                             vmem_limit_bytes=V7X_VMEM_LIMIT_BYTES),
    )(glu, buf, dw_w, dw_b.reshape(1, D))
    M = B * S
    tm = _pick(M, tm)
    z = pl.pallas_call(
        _ln_silu_kernel, grid=(M // tm,), name="ln_silu",
        in_specs=[pl.BlockSpec((tm, D), lambda m: (m, 0)), pl.BlockSpec((2, D), lambda m: (0, 0))],
        out_specs=pl.BlockSpec((tm, D), lambda m: (m, 0)),
        out_shape=jax.ShapeDtypeStruct((M, D), BF16),
        compiler_params=pltpu.CompilerParams(dimension_semantics=("parallel",),
                                             vmem_limit_bytes=V7X_VMEM_LIMIT_BYTES),
    )(conv.reshape(M, D), jnp.stack([ln_g, ln_b]))
    return z, state


def rel_bucket(dist):
    n = jnp.maximum(dist, 0)
    max_exact = REL_BUCKETS // 2
    nf = jnp.maximum(n, 1).astype(jnp.float32)
    large = max_exact + (jnp.log(nf / max_exact) / math.log(REL_MAX_DIST / max_exact)
                         * (REL_BUCKETS - max_exact)).astype(jnp.int32)
    large = jnp.minimum(large, REL_BUCKETS - 1)
    return jnp.where(n < max_exact, n, large)


def bucket_bias(rel_table, dist):
    onehot = jax.nn.one_hot(rel_bucket(dist), REL_BUCKETS, dtype=F32)
    return jnp.dot(onehot, rel_table.astype(F32), precision=lax.Precision.HIGHEST)


def head_bias(rel_table, dist):
    b = bucket_bias(rel_table, dist)
    return b.reshape(dist.shape + (KV_GROUPS, HEADS_PER_GROUP)).transpose(2, 3, 0, 1)


def nsa_project(hb, p):
    kv_pairs = [matmul(hb, p["w_kv"][i], tall=True) for i in range(N_BRANCHES)]
    gl = matmul(hb, p["w_gate"])
    return [w for w, _ in kv_pairs], [t for _, t in kv_pairs], gl


def nsa_prompt(hb, q2, B, T, p, rel_table):
    kvs, kvt, gl = nsa_project(hb, p)
    ident = jnp.arange(B * T // PAGE_SIZE, dtype=jnp.int32).reshape(B, T // PAGE_SIZE)
    kc, vc = compress_paged(kvs[0], ident, p, tall=False)
    nb = T // BLOCK
    dist_c = jnp.arange(T)[:, None] - (jnp.arange(nb) * BLOCK + (BLOCK - 1))[None, :]
    o = nsa_prompt_attention(q2, kvs[1], kvs[2], kc, vc, gl,
                             toeplitz_bias(rel_table), head_bias(rel_table, dist_c), B, T)
    kv_cmp, kv_sel, kv_win = [kv.reshape(B, T, 2, KV_GROUPS, HEAD_DIM) for kv in kvt]
    return o, kv_cmp, kv_sel, kv_win[:, T - min(WINDOW, T):]


def nsa_sample(hb, q2, B, S, pool_cmp, pool_sel, win_buf, page_table, p, rel_table):
    W = 2 * KV_WIDTH
    n_pages = page_table.shape[1]
    kvs, kvt, gl = nsa_project(hb, p)
    assert (n_pages * PAGE_SIZE + S) // BLOCK == n_pages * PAGE_SIZE // BLOCK
    kc, vc = compress_paged(pool_cmp.reshape(-1, HEAD_DIM), page_table, p, tall=True)
    lw = win_buf.shape[1]
    o = nsa_sample_attention(q2, kc, vc, pool_sel.reshape(-1, HEAD_DIM), page_table,
                             win_buf.reshape(-1, HEAD_DIM), kvs[1].reshape(B, S, W), kvs[2].reshape(B, S, W),
                             gl, rel_table)
    kv_cmp, kv_sel, kv_win = [kv.reshape(B, S, 2, KV_GROUPS, HEAD_DIM) for kv in kvt]
    all_win = jnp.concatenate([win_buf, kv_win], axis=1)
    return o, kv_cmp, kv_sel, all_win[:, lw + S - min(WINDOW, lw + S):]


def ffn(xp, xs, g, w_up32, w_down32, layer):
    D_ff = w_up32.shape[2]
    hid_p, w_up = matmul_wcast(rmsnorm(xp, g, BF16), w_up32, layer, D_ff, act="relu2", out_dtype=BF16)
    w_down = w_down32[layer].astype(BF16)
    xp = matmul(hid_p, w_down, res=xp, tk=2048)
    hid_s = matmul(rmsnorm(xs, g, BF16), w_up, act="relu2", out_dtype=BF16)
    return xp, matmul(hid_s, w_down, res=xs, tk=2048)


def kernel(x_prompt, x_sample, state_conv, cache_kv_cmp, cache_kv_sel, state_kv_win, page_table, norm_mix_g, norm_ffn_g, norm_final_g, rel_bias_table, conv_w_in, conv_b_in, conv_dw_w, conv_dw_b, conv_ln_g, conv_ln_b, conv_w_out, conv_b_out, nsa_w_in, nsa_cmp_pos, nsa_cmp_w1, nsa_cmp_b1, nsa_cmp_w2, nsa_cmp_b2, nsa_w_out, ffn_w_up, ffn_w_down):
    Bp, T, D = x_prompt.shape
    Bs, S, _ = x_sample.shape
    xp, xs = x_prompt.reshape(Bp * T, D), x_sample.reshape(Bs * S, D)
    qd = N_HEADS * HEAD_DIM
    kvd = 2 * KV_WIDTH

    def conv_tail(glu, B, L, buf):
        return conv_ln_silu(glu.reshape(B, L, D), buf, conv_dw_w[0], conv_dw_b[0], conv_ln_g[0], conv_ln_b[0])

    glu_p, wa, wg = matmul_glu_wcast(rmsnorm(xp, norm_mix_g[0], BF16), conv_w_in, 0, conv_b_in[0])
    zp, conv_p = conv_tail(glu_p, Bp, T, jnp.zeros((Bp, CONV_BUF, D), F32))
    xp, w_out0 = matmul_wcast(zp, conv_w_out, 0, D, bias=conv_b_out[0], res=xp)
    glu_s = matmul_glu(rmsnorm(xs, norm_mix_g[0], BF16), wa, wg, conv_b_in[0][:D], conv_b_in[0][D:])
    zs, conv_s = conv_tail(glu_s, Bs, S, state_conv[0])
    xs = matmul(zs, w_out0, bias=conv_b_out[0], res=xs)
    xp, xs = ffn(xp, xs, norm_ffn_g[0], ffn_w_up, ffn_w_down, 0)

    w_in = nsa_w_in[0]
    nbr = N_BRANCHES * HEADS_PER_GROUP
    gate_w = w_in[:, qd + N_BRANCHES * kvd:].reshape(D, N_BRANCHES, KV_GROUPS, HEADS_PER_GROUP)
    gate_w = gate_w.transpose(0, 2, 1, 3).reshape(D, KV_GROUPS, nbr)
    gate_w = jnp.pad(gate_w, ((0, 0), (0, 0), (0, 128 - nbr))).reshape(D, KV_GROUPS * 128)
    npar = dict(w_kv=[w_in[:, qd + i * kvd: qd + (i + 1) * kvd].astype(BF16) for i in range(N_BRANCHES)],
                w_gate=gate_w.astype(BF16), pos=nsa_cmp_pos[0], w1=nsa_cmp_w1[0].astype(BF16),
                b1=nsa_cmp_b1[0], w2=nsa_cmp_w2[0].astype(BF16), b2=nsa_cmp_b2[0])
    hp = rmsnorm(xp, norm_mix_g[1], BF16)
    q2p, w_q = matmul_wcast(hp, nsa_w_in, 0, qd, out_dtype=BF16)
    op, kcp, ksp, kwp = nsa_prompt(hp, q2p, Bp, T, npar, rel_bias_table)
    xp, w_out1 = matmul_wcast(op, nsa_w_out, 0, D, res=xp)
    hs = rmsnorm(xs, norm_mix_g[1], BF16)
    os_, kcs, kss, kws = nsa_sample(hs, matmul(hs, w_q), Bs, S, cache_kv_cmp[0], cache_kv_sel[0],
                                    state_kv_win[0], page_table, npar, rel_bias_table)
    xs = matmul(os_.astype(BF16), w_out1, res=xs)
    xp, xs = ffn(xp, xs, norm_ffn_g[1], ffn_w_up, ffn_w_down, 1)

    y_prompt = rmsnorm(xp, norm_final_g, F32).reshape(Bp, T, D)
    y_sample = rmsnorm(xs, norm_final_g, F32).reshape(Bs, S, D)
    return (y_prompt, y_sample, conv_p[None], conv_s[None], kcp[None], kcs[None],
            ksp[None], kss[None], kwp[None], kws[None])
```

```python
import functools
import math

import jax
import jax.numpy as jnp
import numpy as np
from jax import lax
from jax.experimental import pallas as pl
from jax.experimental.pallas import tpu as pltpu

D_MODEL = 4096
CONV_WIDTH = 31
CONV_BUF = CONV_WIDTH - 1
N_HEADS = 32
HEAD_DIM = D_MODEL // N_HEADS
KV_GROUPS = 4
HEADS_PER_GROUP = N_HEADS // KV_GROUPS
KV_WIDTH = KV_GROUPS * HEAD_DIM
BLOCK = 64
N_SELECT = 16
WINDOW = 512
WIN_Q_BLOCK = 128
SEL_Q_CHUNK = 16
N_BRANCHES = 3
PAGE_SIZE = 128
REL_BUCKETS = 32
REL_MAX_DIST = 128
EPS = 1e-6
NEG_INF = -1e30

V7X_VMEM_LIMIT_BYTES = 56 * 1024 * 1024
BF16 = jnp.bfloat16
F32 = jnp.float32


def _apply_act(y, act):
    if act == "relu2":
        return jnp.square(jnp.maximum(y, 0.0))
    if act == "silu":
        return y * jax.nn.sigmoid(y)
    assert act is None
    return y


def _mm_kernel(*refs, act, has_bias, has_res, has_tall, nk):
    x_ref, w_ref = refs[0], refs[1]
    pos = 2
    b_ref = r_ref = None
    if has_bias:
        b_ref = refs[pos]
        pos += 1
    if has_res:
        r_ref = refs[pos]
        pos += 1
    o_ref = refs[pos]
    pos += 1
    tall_ref = None
    if has_tall:
        tall_ref = refs[pos]
        pos += 1
    acc_ref = refs[pos] if nk > 1 else None

    def epilogue(y):
        if has_bias:
            y = y + b_ref[...]
        y = _apply_act(y, act)
        if has_res:
            y = y + r_ref[...]
        o_ref[...] = y.astype(o_ref.dtype)
        if has_tall:
            n_slabs = y.shape[1] // HEAD_DIM
            for c in range(n_slabs):
                tall_ref[pl.ds(c, y.shape[0], stride=n_slabs), :] = y[:, c * HEAD_DIM:(c + 1) * HEAD_DIM]

    part = jnp.dot(x_ref[...], w_ref[...], preferred_element_type=F32)
    if nk == 1:
        epilogue(part)
        return
    k = pl.program_id(2)

    @pl.when(k == 0)
    def _():
        acc_ref[...] = part

    @pl.when(jnp.logical_and(k > 0, k < nk - 1))
    def _():
        acc_ref[...] += part

    @pl.when(k == nk - 1)
    def _():
        epilogue(acc_ref[...] + part)


def _pick(dim, pref):
    t = min(dim, pref)
    assert dim % t == 0, (dim, pref)
    return t


def matmul(x, w, bias=None, res=None, act=None, out_dtype=F32, tm=1024, tn=1024, tk=4096, tall=False):
    M, K = x.shape
    K2, N = w.shape
    assert K == K2
    tm, tn, tk = _pick(M, tm), _pick(N, tn), _pick(K, tk)
    nk = K // tk
    in_specs = [pl.BlockSpec((tm, tk), lambda m, n, k: (m, k)),
                pl.BlockSpec((tk, tn), lambda m, n, k: (k, n))]
    args = [x, w]
    if bias is not None:
        in_specs.append(pl.BlockSpec((1, tn), lambda m, n, k: (0, n)))
        args.append(bias.reshape(1, N).astype(F32))
    if res is not None:
        in_specs.append(pl.BlockSpec((tm, tn), lambda m, n, k: (m, n)))
        args.append(res)
    scratch = [pltpu.VMEM((tm, tn), F32)] if nk > 1 else []
    out_specs = pl.BlockSpec((tm, tn), lambda m, n, k: (m, n))
    out_shape = jax.ShapeDtypeStruct((M, N), out_dtype)
    if tall:
        assert tn == N and N % HEAD_DIM == 0
        slabs = N // HEAD_DIM
        out_specs = [out_specs, pl.BlockSpec((tm * slabs, HEAD_DIM), lambda m, n, k: (m, 0))]
        out_shape = [out_shape, jax.ShapeDtypeStruct((M * slabs, HEAD_DIM), F32)]
    return pl.pallas_call(
        functools.partial(_mm_kernel, act=act, has_bias=bias is not None,
                          has_res=res is not None, has_tall=tall, nk=nk),
        grid=(M // tm, N // tn, nk),
        in_specs=in_specs,
        out_specs=out_specs,
        out_shape=out_shape,
        scratch_shapes=scratch,
        compiler_params=pltpu.CompilerParams(
            dimension_semantics=("parallel", "parallel", "arbitrary"),
            vmem_limit_bytes=V7X_VMEM_LIMIT_BYTES),
    )(*args)


def _glu_kernel(x_ref, wa_ref, wg_ref, ba_ref, bg_ref, o_ref):
    x = x_ref[...]
    a = jnp.dot(x, wa_ref[...], preferred_element_type=F32) + ba_ref[...]
    g = jnp.dot(x, wg_ref[...], preferred_element_type=F32) + bg_ref[...]
    o_ref[...] = a * jax.nn.sigmoid(g)


def matmul_glu(x, wa, wg, ba, bg, tm=1024, tn=512):
    M, K = x.shape
    N = wa.shape[1]
    tm, tn = _pick(M, tm), _pick(N, tn)
    wspec = pl.BlockSpec((K, tn), lambda m, n: (0, n))
    bspec = pl.BlockSpec((1, tn), lambda m, n: (0, n))
    return pl.pallas_call(
        _glu_kernel,
        grid=(M // tm, N // tn),
        in_specs=[pl.BlockSpec((tm, K), lambda m, n: (m, 0)), wspec, wspec, bspec, bspec],
        out_specs=pl.BlockSpec((tm, tn), lambda m, n: (m, n)),
        out_shape=jax.ShapeDtypeStruct((M, N), F32),
        compiler_params=pltpu.CompilerParams(
            dimension_semantics=("parallel", "parallel"),
            vmem_limit_bytes=V7X_VMEM_LIMIT_BYTES),
    )(x, wa, wg, ba.reshape(1, N), bg.reshape(1, N))


def _mm_wcast_kernel(*refs, act, has_bias, has_res):
    x_ref, w_ref = refs[0], refs[1]
    pos = 2
    b_ref = r_ref = None
    if has_bias:
        b_ref = refs[pos]
        pos += 1
    if has_res:
        r_ref = refs[pos]
        pos += 1
    o_ref, wbf_ref = refs[pos], refs[pos + 1]

    @pl.when(pl.program_id(1) == 0)
    def _():
        wbf_ref[...] = w_ref[...].astype(BF16)

    y = jnp.dot(x_ref[...], wbf_ref[...], preferred_element_type=F32)
    if has_bias:
        y = y + b_ref[...]
    y = _apply_act(y, act)
    if has_res:
        y = y + r_ref[...]
    o_ref[...] = y.astype(o_ref.dtype)


def matmul_wcast(x, w32, layer, n_cols, bias=None, res=None, act=None, out_dtype=F32, tm=1024, tn=512):
    M, K = x.shape
    assert w32.shape[1] == K and n_cols <= w32.shape[2]
    N = n_cols
    tm, tn = _pick(M, tm), _pick(N, tn)
    in_specs = [pl.BlockSpec((tm, K), lambda n, m: (m, 0)),
                pl.BlockSpec((None, K, tn), lambda n, m: (layer, 0, n))]
    args = [x, w32]
    if bias is not None:
        in_specs.append(pl.BlockSpec((1, tn), lambda n, m: (0, n)))
        args.append(bias.reshape(1, N).astype(F32))
    if res is not None:
        in_specs.append(pl.BlockSpec((tm, tn), lambda n, m: (m, n)))
        args.append(res)
    return pl.pallas_call(
        functools.partial(_mm_wcast_kernel, act=act, has_bias=bias is not None, has_res=res is not None),
        grid=(N // tn, M // tm), name="mm_wcast",
        in_specs=in_specs,
        out_specs=[pl.BlockSpec((tm, tn), lambda n, m: (m, n)), pl.BlockSpec((K, tn), lambda n, m: (0, n))],
        out_shape=[jax.ShapeDtypeStruct((M, N), out_dtype), jax.ShapeDtypeStruct((K, N), BF16)],
        compiler_params=pltpu.CompilerParams(
            dimension_semantics=("parallel", "arbitrary"),
            vmem_limit_bytes=V7X_VMEM_LIMIT_BYTES),
    )(*args)


def _mm_wcast_ktiled_kernel(x_ref, w_ref, r_ref, o_ref, wbf_ref, acc_ref, *, nk, tm):
    k = pl.program_id(1)
    m = pl.program_id(2)

    @pl.when(m == 0)
    def _():
        wbf_ref[...] = w_ref[...].astype(BF16)

    part = jnp.dot(x_ref[...], wbf_ref[...], preferred_element_type=F32)
    rows = pl.ds(pl.multiple_of(m * tm, tm), tm)

    @pl.when(k == 0)
    def _():
        acc_ref[rows, :] = part

    @pl.when(jnp.logical_and(k > 0, k < nk - 1))
    def _():
        acc_ref[rows, :] += part

    @pl.when(k == nk - 1)
    def _():
        o_ref[...] = acc_ref[rows, :] + part + r_ref[...]


def matmul_wcast_ktiled(x, w32, layer, res, tm=1024, tn=512, tk=2048):
    M, K = x.shape
    N = w32.shape[2]
    tm, tn, tk = _pick(M, tm), _pick(N, tn), _pick(K, tk)
    nk = K // tk
    assert nk >= 2
    last_rows = lambda n, k, m: (jnp.where(k == nk - 1, m, 0), n)
    return pl.pallas_call(
        functools.partial(_mm_wcast_ktiled_kernel, nk=nk, tm=tm),
        grid=(N // tn, nk, M // tm), name="mm_wcast_ktiled",
        in_specs=[pl.BlockSpec((tm, tk), lambda n, k, m: (m, k)),
                  pl.BlockSpec((None, tk, tn), lambda n, k, m: (layer, k, n)),
                  pl.BlockSpec((tm, tn), last_rows)],
        out_specs=[pl.BlockSpec((tm, tn), last_rows), pl.BlockSpec((tk, tn), lambda n, k, m: (k, n))],
        out_shape=[jax.ShapeDtypeStruct((M, N), F32), jax.ShapeDtypeStruct((K, N), BF16)],
        scratch_shapes=[pltpu.VMEM((M, tn), F32)],
        compiler_params=pltpu.CompilerParams(
            dimension_semantics=("parallel", "arbitrary", "arbitrary"),
            vmem_limit_bytes=V7X_VMEM_LIMIT_BYTES),
    )(x, w32, res)


def _glu_wcast_kernel(x_ref, wa_ref, wg_ref, ba_ref, bg_ref, o_ref, wabf_ref, wgbf_ref):
    @pl.when(pl.program_id(1) == 0)
    def _():
        wabf_ref[...] = wa_ref[...].astype(BF16)
        wgbf_ref[...] = wg_ref[...].astype(BF16)

    x = x_ref[...]
    a = jnp.dot(x, wabf_ref[...], preferred_element_type=F32) + ba_ref[...]
    g = jnp.dot(x, wgbf_ref[...], preferred_element_type=F32) + bg_ref[...]
    o_ref[...] = a * jax.nn.sigmoid(g)


def matmul_glu_wcast(x, w32, layer, b, tm=1024, tn=256):
    M, K = x.shape
    N = w32.shape[2] // 2
    tm, tn = _pick(M, tm), _pick(N, tn)
    nt = N // tn
    wspec = pl.BlockSpec((K, tn), lambda n, m: (0, n))
    return pl.pallas_call(
        _glu_wcast_kernel, grid=(nt, M // tm), name="mm_glu_wcast",
        in_specs=[pl.BlockSpec((tm, K), lambda n, m: (m, 0)),
                  pl.BlockSpec((None, K, tn), lambda n, m: (layer, 0, n)),
                  pl.BlockSpec((None, K, tn), lambda n, m: (layer, 0, nt + n)),
                  pl.BlockSpec((1, tn), lambda n, m: (0, n)), pl.BlockSpec((1, tn), lambda n, m: (0, nt + n))],
        out_specs=[pl.BlockSpec((tm, tn), lambda n, m: (m, n)), wspec, wspec],
        out_shape=[jax.ShapeDtypeStruct((M, N), F32), jax.ShapeDtypeStruct((K, N), BF16),
                   jax.ShapeDtypeStruct((K, N), BF16)],
        compiler_params=pltpu.CompilerParams(
            dimension_semantics=("parallel", "arbitrary"),
            vmem_limit_bytes=V7X_VMEM_LIMIT_BYTES),
    )(x, w32, w32, b.reshape(1, 2 * N), b.reshape(1, 2 * N))


TQ = 128
SEL_KT = 4 * TQ
IMP_FORCED = 16.0
IMP_FUTURE = -1.0
_NT = (((1,), (1,)), ((), ()))


def _nsa_prompt_kernel(q_ref, ks_ref, vs_ref, kw_ref, vw_ref, kc_ref, vc_ref, gl_ref, toe_ref, cb_ref,
                       e_ref, o_ref, selm_ref, m_ref, l_ref, acc_ref, *, T):
    R = HEADS_PER_GROUP
    nb = T // BLOCK
    nqt = T // TQ
    i = pl.program_id(2)
    scale = HEAD_DIM ** -0.5
    qb = q_ref[...]
    qm = jnp.concatenate([qb[:, r * HEAD_DIM:(r + 1) * HEAD_DIM] for r in range(R)], axis=0)
    t_loc = lax.broadcasted_iota(jnp.int32, (TQ, 1), 0)
    q_pos = i * TQ + t_loc

    kc = kc_ref[0, 0].astype(BF16)
    vc = vc_ref[0, 0].astype(BF16)
    sc = lax.dot_general(qm, kc, _NT, preferred_element_type=F32) * scale
    sc = sc.reshape(R, TQ, nb) + cb_ref[0]
    jn = lax.broadcasted_iota(jnp.int32, (TQ, nb), 1)
    valid = (q_pos - (jn * BLOCK + (BLOCK - 1))) >= 0
    sc = jnp.where(valid[None], sc, NEG_INF)
    mc = jnp.max(sc, axis=-1, keepdims=True)
    ec = jnp.where(valid[None], jnp.exp(sc - mc), 0.0)
    lc = jnp.sum(ec, axis=-1, keepdims=True)
    pc = ec / jnp.where(lc > 0.0, lc, 1.0)
    o_cmp = jnp.dot(pc.reshape(R * TQ, nb).astype(BF16), vc, preferred_element_type=F32).reshape(R, TQ, HEAD_DIM)
    imp = pc[0]
    for r in range(1, R):
        imp = imp + pc[r]

    cur = q_pos // BLOCK
    forced = (jn == 0) | (jn == cur) | (jn == cur - 1)
    impp = jnp.where(jn > cur, IMP_FUTURE, jnp.where(forced, IMP_FORCED, imp))
    rank = jnp.zeros((TQ, nb), jnp.int32)
    for jp in range(nb):
        col = impp[:, jp:jp + 1]
        beats = (col > impp) | ((col == impp) & (jn > jp))
        rank = rank + beats.astype(jnp.int32)
    sel = (rank < min(N_SELECT, nb)).astype(BF16)
    selm = jnp.dot(sel, e_ref[...], preferred_element_type=F32)
    for st in range(T // SEL_KT):
        selm_ref[st] = selm[:, st * SEL_KT:(st + 1) * SEL_KT]

    def with_ones(v):
        return jnp.concatenate([v.astype(BF16), jnp.ones(v.shape, BF16)], axis=1)

    def scores(k_t, kt0, n_sub):
        s = lax.dot_general(qm, k_t, _NT, preferred_element_type=F32) * scale
        bias = jnp.concatenate([toe_ref[0, jnp.clip(i - kt0 - c, 0, 2)] for c in range(n_sub)], axis=1)
        j_loc = lax.broadcasted_iota(jnp.int32, (1, n_sub * TQ), 1)
        dist = (i - kt0) * TQ + t_loc - j_loc
        return (s + bias).reshape(R, TQ, n_sub * TQ), dist

    m_ref[...] = jnp.full(m_ref.shape, NEG_INF, F32)
    l_ref[...] = jnp.zeros(l_ref.shape, F32)
    acc_ref[...] = jnp.zeros(acc_ref.shape, F32)
    sub = SEL_KT // TQ

    def sel_body(st, carry):
        ks = pl.multiple_of(st * SEL_KT, SEL_KT)
        s, dist = scores(ks_ref[pl.ds(ks, SEL_KT), :].astype(BF16), st * sub, sub)
        mask = (dist >= 0) & (selm_ref[st] > 0.5)
        s = jnp.where(mask[None], s, NEG_INF)
        m_old = m_ref[...]
        m_new = jnp.maximum(m_old, jnp.max(s, axis=-1, keepdims=True))
        p = jnp.exp(s - m_new)
        alpha = jnp.exp(m_old - m_new)
        pv = jnp.dot(p.reshape(R * TQ, SEL_KT).astype(BF16), with_ones(vs_ref[pl.ds(ks, SEL_KT), :]),
                     preferred_element_type=F32)
        acc_ref[...] = alpha * acc_ref[...] + pv[:, :HEAD_DIM].reshape(R, TQ, HEAD_DIM)
        l_ref[...] = alpha * l_ref[...] + pv[:, HEAD_DIM:].reshape(R, TQ, HEAD_DIM)
        m_ref[...] = m_new
        return carry

    lax.fori_loop(0, i // sub + 1, sel_body, 0)
    o_sel = acc_ref[...] / l_ref[...]

    n_win = WINDOW // TQ + 1
    kt0 = jnp.clip(i - WINDOW // TQ, 0, nqt - n_win)
    ws = pl.multiple_of(kt0 * TQ, TQ)
    s, dist = scores(kw_ref[pl.ds(ws, n_win * TQ), :].astype(BF16), kt0, n_win)
    s = jnp.where(((dist >= 0) & (dist <= WINDOW))[None], s, NEG_INF)
    p = jnp.exp(s - jnp.max(s, axis=-1, keepdims=True))
    pv = jnp.dot(p.reshape(R * TQ, n_win * TQ).astype(BF16), with_ones(vw_ref[pl.ds(ws, n_win * TQ), :]),
                 preferred_element_type=F32)
    o_win = (pv[:, :HEAD_DIM] / pv[:, HEAD_DIM:]).reshape(R, TQ, HEAD_DIM)

    gates = jax.nn.sigmoid(gl_ref[...])
    for r in range(R):
        o = (gates[:, r:r + 1] * o_cmp[r] + gates[:, R + r:R + r + 1] * o_sel[r]
             + gates[:, 2 * R + r:2 * R + r + 1] * o_win[r])
        o_ref[:, r * HEAD_DIM:(r + 1) * HEAD_DIM] = o.astype(o_ref.dtype)


def nsa_prompt_attention(q2, kv_sel, kv_win, kc, vc, gl, toe, cb, B, T):
    G, R = KV_GROUPS, HEADS_PER_GROUP
    nb, nqt = T // BLOCK, T // TQ
    assert T % SEL_KT == 0 and nqt >= WINDOW // TQ + 1
    expand = (np.arange(T)[None, :] // BLOCK == np.arange(nb)[:, None])
    e = jnp.asarray(expand, BF16)
    qspec = pl.BlockSpec((TQ, R * HEAD_DIM), lambda b, g, i: (b * nqt + i, g))
    kspec = pl.BlockSpec((T, HEAD_DIM), lambda b, g, i: (b, g))
    vspec = pl.BlockSpec((T, HEAD_DIM), lambda b, g, i: (b, G + g))
    cspec = pl.BlockSpec((1, 1, nb, HEAD_DIM), lambda b, g, i: (b, g, 0, 0))
    return pl.pallas_call(
        functools.partial(_nsa_prompt_kernel, T=T), name="nsa_prompt_attn",
        grid=(B, G, nqt),
        in_specs=[qspec, kspec, vspec, kspec, vspec, cspec, cspec,
                  pl.BlockSpec((TQ, 128), lambda b, g, i: (b * nqt + i, g)),
                  pl.BlockSpec((1, 3, R * TQ, TQ), lambda b, g, i: (g, 0, 0, 0)),
                  pl.BlockSpec((1, R, TQ, nb), lambda b, g, i: (g, 0, i, 0)),
                  pl.BlockSpec((nb, T), lambda b, g, i: (0, 0))],
        out_specs=qspec,
        out_shape=jax.ShapeDtypeStruct((B * T, G * R * HEAD_DIM), BF16),
        scratch_shapes=[pltpu.VMEM((T // SEL_KT, TQ, SEL_KT), F32), pltpu.VMEM((R, TQ, 1), F32),
                        pltpu.VMEM((R, TQ, HEAD_DIM), F32), pltpu.VMEM((R, TQ, HEAD_DIM), F32)],
        compiler_params=pltpu.CompilerParams(
            dimension_semantics=("parallel", "parallel", "arbitrary"),
            vmem_limit_bytes=V7X_VMEM_LIMIT_BYTES),
    )(q2, kv_sel, kv_sel, kv_win, kv_win, kc, vc, gl, toe, cb, e)


def toeplitz_bias(rel_table):
    d = jnp.arange(3)[:, None, None] * TQ + jnp.arange(TQ)[None, :, None] - jnp.arange(TQ)[None, None, :]
    b = bucket_bias(rel_table, d)
    b = b.reshape(3, TQ, TQ, KV_GROUPS, HEADS_PER_GROUP).transpose(3, 0, 4, 1, 2)
    return b.reshape(KV_GROUPS, 3, HEADS_PER_GROUP * TQ, TQ)


NEW_PAD = TQ
PAGES_PER_STEP = 4
SEL_DONE = -3.0
SEL_PAD = -2.0


BLOCKIFY_PAGES = 8
KV_ROWS = 2 * KV_GROUPS


def _blockify_kernel(pt_ref, *refs, tall):
    del pt_ref
    NP = BLOCKIFY_PAGES
    pages, pos_ref, o_ref, xs_ref = refs[:NP], refs[NP], refs[NP + 1], refs[NP + 2]
    rows = pages[0].shape[0]
    for pi, pg in enumerate(pages):
        if tall:
            xs_ref[pi * rows:(pi + 1) * rows, :] = pg[...]
        else:
            for c in range(KV_ROWS):
                xs_ref[c, pi * rows:(pi + 1) * rows, :] = pg[:, c * HEAD_DIM:(c + 1) * HEAD_DIM]
    nblk = NP * PAGE_SIZE // BLOCK
    for kv in range(2):
        for g in range(KV_GROUPS):
            c = kv * KV_GROUPS + g
            for j in range(BLOCK):
                if tall:
                    v = xs_ref[pl.ds(j * KV_ROWS + c, nblk, stride=BLOCK * KV_ROWS), :]
                else:
                    v = xs_ref[c, pl.ds(j, nblk, stride=BLOCK), :]
                o_ref[kv, 0, g, :, j * HEAD_DIM:(j + 1) * HEAD_DIM] = (v + pos_ref[kv, j:j + 1, :]).astype(o_ref.dtype)


def blockify(pool, page_table, pos, tall):
    B, n_pages = page_table.shape
    NP = BLOCKIFY_PAGES
    assert n_pages % NP == 0
    blk = (PAGE_SIZE * KV_ROWS, HEAD_DIM) if tall else (PAGE_SIZE, KV_ROWS * HEAD_DIM)
    page_specs = [pl.BlockSpec(blk, functools.partial(lambda b, s, pt, k: (pt[b, s * NP + k], 0), k=k))
                  for k in range(NP)]
    nbs = NP * PAGE_SIZE // BLOCK
    grid_spec = pltpu.PrefetchScalarGridSpec(
        num_scalar_prefetch=1, grid=(B, n_pages // NP),
        in_specs=page_specs + [pl.BlockSpec((2, BLOCK, HEAD_DIM), lambda b, s, pt: (0, 0, 0))],
        out_specs=pl.BlockSpec((2, 1, KV_GROUPS, nbs, BLOCK * HEAD_DIM), lambda b, s, pt: (0, b, 0, s, 0)),
        scratch_shapes=[pltpu.VMEM((NP * PAGE_SIZE * KV_ROWS, HEAD_DIM) if tall else
                                   (KV_ROWS, NP * PAGE_SIZE, HEAD_DIM), F32)])
    return pl.pallas_call(
        functools.partial(_blockify_kernel, tall=tall), grid_spec=grid_spec, name="blockify",
        out_shape=jax.ShapeDtypeStruct((2, B, KV_GROUPS, n_pages * PAGE_SIZE // BLOCK, BLOCK * HEAD_DIM), BF16),
        compiler_params=pltpu.CompilerParams(dimension_semantics=("parallel", "arbitrary"),
                                             vmem_limit_bytes=V7X_VMEM_LIMIT_BYTES),
    )(page_table, *([pool] * NP), pos)


def _compress_mlp_kernel(a_ref, w1_ref, b1_ref, w2_ref, b2_ref, o_ref):
    h = jnp.dot(a_ref[0], w1_ref[0], preferred_element_type=F32) + b1_ref[0]
    h = h * jax.nn.sigmoid(h)
    o_ref[0] = jnp.dot(h.astype(BF16), w2_ref[0], preferred_element_type=F32) + b2_ref[0]


def compress_mlp(a, w1, b1, w2, b2, tm=512):
    _, M, K = a.shape
    H = w1.shape[-1]
    tm = _pick(M, tm)
    return pl.pallas_call(
        _compress_mlp_kernel, grid=(2, M // tm), name="compress_mlp",
        in_specs=[pl.BlockSpec((1, tm, K), lambda k, m: (k, m, 0)),
                  pl.BlockSpec((1, K, H), lambda k, m: (k, 0, 0)),
                  pl.BlockSpec((1, 1, H), lambda k, m: (k, 0, 0)),
                  pl.BlockSpec((1, H, HEAD_DIM), lambda k, m: (k, 0, 0)),
                  pl.BlockSpec((1, 1, HEAD_DIM), lambda k, m: (k, 0, 0))],
        out_specs=pl.BlockSpec((1, tm, HEAD_DIM), lambda k, m: (k, m, 0)),
        out_shape=jax.ShapeDtypeStruct((2, M, HEAD_DIM), F32),
        compiler_params=pltpu.CompilerParams(dimension_semantics=("parallel", "parallel"),
                                             vmem_limit_bytes=V7X_VMEM_LIMIT_BYTES),
    )(a, w1, b1.reshape(2, 1, H), w2, b2.reshape(2, 1, HEAD_DIM))


def compress_paged(pool, page_table, p, tall):
    a = blockify(pool, page_table, p["pos"], tall)
    _, B, G, nb = a.shape[:4]
    out = compress_mlp(a.reshape(2, B * G * nb, BLOCK * HEAD_DIM), p["w1"], p["b1"], p["w2"], p["b2"])
    out = out.reshape(2, B, G, nb, HEAD_DIM)
    return out[0], out[1]


def _expand_rows(x8):
    return jnp.concatenate([x8] * HEADS_PER_GROUP, axis=0)


def _head_rows(q, g):
    R = HEADS_PER_GROUP
    return jnp.concatenate([q[:, (g * R + r) * HEAD_DIM:(g * R + r + 1) * HEAD_DIM] for r in range(R)], axis=0)


def _nsa_sample_select_kernel(q_ref, kc_ref, vc_ref, cb_ref, ocmp_ref, selsc_ref, sel_ref, *, S, n_pages):
    G, R = KV_GROUPS, HEADS_PER_GROUP
    RS = R * S
    nbc = kc_ref.shape[2]
    nbp = sel_ref.shape[3]
    n_blocks = -(-(n_pages * PAGE_SIZE + S) // BLOCK)
    past_len = n_pages * PAGE_SIZE
    scale = HEAD_DIM ** -0.5
    qi = lax.broadcasted_iota(jnp.int32, (RS, 1), 0) % S
    q = q_ref[...]
    jn = lax.broadcasted_iota(jnp.int32, (S, nbp), 1)
    jf = jn.astype(F32)
    cur = (past_len + lax.broadcasted_iota(jnp.int32, (S, 1), 0)) // BLOCK
    forced = (jn == 0) | (jn == cur) | (jn == cur - 1)
    jc = lax.broadcasted_iota(jnp.int32, (RS, nbc), 1)
    valid = (past_len + qi - (jc * BLOCK + (BLOCK - 1))) >= 0
    for g in range(G):
        qg = _head_rows(q, g).astype(BF16)
        sc = lax.dot_general(qg, kc_ref[0, g].astype(BF16), _NT, preferred_element_type=F32) * scale
        sc = jnp.where(valid, sc + cb_ref[g], NEG_INF)
        ec = jnp.where(valid, jnp.exp(sc - jnp.max(sc, axis=-1, keepdims=True)), 0.0)
        lc = jnp.sum(ec, axis=-1, keepdims=True)
        pc = ec / jnp.where(lc > 0.0, lc, 1.0)
        ocmp_ref[0, g] = jnp.dot(pc.astype(BF16), vc_ref[0, g].astype(BF16), preferred_element_type=F32)
        imp = pc[0:S]
        for r in range(1, R):
            imp = imp + pc[r * S:(r + 1) * S]
        imp = jnp.concatenate([imp, jnp.zeros((S, nbp - nbc), F32)], axis=1)
        work = jnp.where(jn > cur, IMP_FUTURE, jnp.where(forced, IMP_FORCED, imp))
        work = jnp.where(jn >= n_blocks, SEL_PAD, work)
        sel = jnp.zeros((S, nbp), F32)
        for _ in range(min(N_SELECT, n_blocks)):
            mx = jnp.max(work, axis=-1, keepdims=True)
            first = jnp.min(jnp.where(work == mx, jf, float(nbp)), axis=-1, keepdims=True)
            hit = jf == first
            sel = jnp.where(hit, 1.0, sel)
            work = jnp.where(hit, SEL_DONE, work)
        sel_ref[0, g] = sel
        for st in range(nbp // 8):
            selsc_ref[0, g, st] = jnp.concatenate(
                [sel[:, st * 8:(st + 1) * 8], jnp.zeros((S, 128 - 8), F32)], axis=1)


def _nsa_sample_kernel(pt_ref, fl_ref, *refs, S, n_pages, lw):
    del pt_ref
    PP = PAGES_PER_STEP
    pages = refs[:PP]
    (q_ref, ocmp_ref, selsc_ref, winb_ref, nsel_ref, nwin_ref, gl_ref, sb_ref, e_ref,
     o_ref, qm_ref, m_ref, l_ref, acc_ref) = refs[PP:]
    G, R = KV_GROUPS, HEADS_PER_GROUP
    RS = R * S
    steps = n_pages // PP
    b = pl.program_id(0)
    s_id = pl.program_id(1)
    scale = HEAD_DIM ** -0.5
    qi = lax.broadcasted_iota(jnp.int32, (RS, 1), 0) % S
    jl = lax.broadcasted_iota(jnp.int32, (1, TQ), 1)

    def attend(g, k_t, v_t, bias, mask):
        s = lax.dot_general(qm_ref[g], k_t.astype(BF16), _NT, preferred_element_type=F32) * scale + bias
        s = jnp.where(mask, s, NEG_INF)
        m_old = m_ref[g]
        m_new = jnp.maximum(m_old, jnp.max(s, axis=-1, keepdims=True))
        p = jnp.where(mask, jnp.exp(s - m_new), 0.0)
        alpha = jnp.exp(m_old - m_new)
        l_ref[g] = alpha * l_ref[g] + jnp.sum(p, axis=-1, keepdims=True)
        acc_ref[g] = alpha * acc_ref[g] + jnp.dot(p.astype(BF16), v_t.astype(BF16), preferred_element_type=F32)
        m_ref[g] = m_new

    def reset_state():
        m_ref[...] = jnp.full(m_ref.shape, NEG_INF, F32)
        l_ref[...] = jnp.zeros(l_ref.shape, F32)
        acc_ref[...] = jnp.zeros(acc_ref.shape, F32)

    @pl.when(s_id == 0)
    def _():
        reset_state()
        q = q_ref[...]
        for g in range(G):
            qm_ref[g] = _head_rows(q, g).astype(BF16)

    flags = [fl_ref[b * G + g, s_id * PP + pi] for g in range(G) for pi in range(PP)]

    @pl.when(sum(flags) > 0)
    def _():
        bias_tbl = [jnp.minimum(n_pages - (s_id * PP + pi), 2) for pi in range(PP)]
        for g in range(G):
            selm = jnp.dot(selsc_ref[0, g, s_id].astype(BF16), e_ref[...], preferred_element_type=F32)
            k_t = jnp.concatenate([pg[pl.ds(g, PAGE_SIZE, stride=KV_ROWS), :] for pg in pages], axis=0)
            v_t = jnp.concatenate([pg[pl.ds(G + g, PAGE_SIZE, stride=KV_ROWS), :] for pg in pages], axis=0)
            bias = jnp.concatenate([sb_ref[g, t] for t in bias_tbl], axis=1)
            attend(g, k_t, v_t, bias, _expand_rows(selm) > 0.5)

    @pl.when(s_id == steps - 1)
    def _():
        dist_new = qi - jl
        o_sel, o_win = [], []
        for g in range(G):
            selm = jnp.dot(selsc_ref[0, g, steps].astype(BF16), e_ref[...], preferred_element_type=F32)
            mask = (_expand_rows(selm[:, :TQ]) > 0.5) & (dist_new >= 0)
            attend(g, nsel_ref[0, :, g * HEAD_DIM:(g + 1) * HEAD_DIM],
                   nsel_ref[0, :, (G + g) * HEAD_DIM:(G + g + 1) * HEAD_DIM], sb_ref[g, 0], mask)
            o_sel.append(acc_ref[g] / l_ref[g])
        reset_state()
        for g in range(G):
            for t in range(lw // TQ):
                base = lw - t * TQ
                dist = base + qi - jl
                attend(g, winb_ref[pl.ds(t * TQ * KV_ROWS + g, TQ, stride=KV_ROWS), :],
                       winb_ref[pl.ds(t * TQ * KV_ROWS + G + g, TQ, stride=KV_ROWS), :],
                       sb_ref[g, min(base // TQ, 2)], (dist >= 0) & (dist <= WINDOW))
            attend(g, nwin_ref[0, :, g * HEAD_DIM:(g + 1) * HEAD_DIM],
                   nwin_ref[0, :, (G + g) * HEAD_DIM:(G + g + 1) * HEAD_DIM], sb_ref[g, 0],
                   (dist_new >= 0) & (dist_new <= WINDOW))
            o_win.append(acc_ref[g] / l_ref[g])
        gates = jax.nn.sigmoid(gl_ref[...])
        for g in range(G):
            oc = ocmp_ref[0, g]
            for r in range(R):
                c = g * 128 + r
                rows = slice(r * S, (r + 1) * S)
                o = (gates[:, c:c + 1] * oc[rows] + gates[:, c + R:c + R + 1] * o_sel[g][rows]
                     + gates[:, c + 2 * R:c + 2 * R + 1] * o_win[g][rows])
                o_ref[:, (g * R + r) * HEAD_DIM:(g * R + r + 1) * HEAD_DIM] = o


def sample_bias_tables(rel_table, S, past_len, nbc):
    G, R = KV_GROUPS, HEADS_PER_GROUP
    d = jnp.arange(3)[:, None, None] * TQ + jnp.arange(S)[None, :, None] - jnp.arange(TQ)[None, None, :]
    b = bucket_bias(rel_table, d).reshape(3, S, TQ, G, R)
    sb = b.transpose(3, 0, 4, 1, 2).reshape(G, 3, R * S, TQ)
    dc = past_len + jnp.arange(S)[:, None] - (jnp.arange(nbc) * BLOCK + (BLOCK - 1))[None, :]
    cb = head_bias(rel_table, dc).reshape(G, R * S, nbc)
    return sb, cb


def nsa_sample_attention(q, kc, vc, pool_sel, page_table, win_buf, new_sel, new_win, gl, rel_table):
    G, R = KV_GROUPS, HEADS_PER_GROUP
    B, n_pages = page_table.shape
    S = new_sel.shape[1]
    lw = win_buf.shape[0] // (B * KV_ROWS)
    W = 2 * KV_WIDTH
    nbc = kc.shape[2]
    PP = PAGES_PER_STEP
    steps = n_pages // PP
    assert n_pages % PP == 0 and lw % TQ == 0 and S <= BLOCK and PP * PAGE_SIZE // BLOCK == 8
    sb, cb = sample_bias_tables(rel_table, S, n_pages * PAGE_SIZE, nbc)
    nbp = (steps + 1) * 8

    brow = lambda shape: pl.BlockSpec(shape, lambda b: (b,) + (0,) * (len(shape) - 1))
    ocmp, selsc, sel = pl.pallas_call(
        functools.partial(_nsa_sample_select_kernel, S=S, n_pages=n_pages), grid=(B,), name="nsa_sample_select",
        in_specs=[brow((S, G * R * HEAD_DIM)), brow((1, G, nbc, HEAD_DIM)), brow((1, G, nbc, HEAD_DIM)),
                  pl.BlockSpec((G, R * S, nbc), lambda b: (0, 0, 0))],
        out_specs=[brow((1, G, R * S, HEAD_DIM)), brow((1, G, steps + 1, S, 128)), brow((1, G, S, nbp))],
        out_shape=[jax.ShapeDtypeStruct((B, G, R * S, HEAD_DIM), F32),
                   jax.ShapeDtypeStruct((B, G, steps + 1, S, 128), F32),
                   jax.ShapeDtypeStruct((B, G, S, nbp), F32)],
        compiler_params=pltpu.CompilerParams(dimension_semantics=("parallel",),
                                             vmem_limit_bytes=V7X_VMEM_LIMIT_BYTES),
    )(q, kc, vc, cb)
    bpp = PAGE_SIZE // BLOCK
    flags = sel[..., :n_pages * bpp].reshape(B, G, S, n_pages, bpp).max(axis=(2, 4))
    flags = (flags > 0.5).astype(jnp.int32).reshape(B * G, n_pages)

    pad = ((0, 0), (0, NEW_PAD - S), (0, 0))
    new_sel, new_win = jnp.pad(new_sel, pad), jnp.pad(new_win, pad)
    expand = np.zeros((128, PP * PAGE_SIZE), np.float32)
    expand[np.arange(PP * PAGE_SIZE) // BLOCK, np.arange(PP * PAGE_SIZE)] = 1.0
    e = jnp.asarray(expand, BF16)
    page_specs = [pl.BlockSpec((PAGE_SIZE * KV_ROWS, HEAD_DIM), functools.partial(
        lambda b, s, pt, fl, k: (pt[b, s * PP + k], 0), k=k)) for k in range(PP)]
    row = lambda shape: pl.BlockSpec(shape, lambda b, s, pt, fl: (b,) + (0,) * (len(shape) - 1))
    const = lambda shape: pl.BlockSpec(shape, lambda b, s, pt, fl: (0,) * len(shape))
    grid_spec = pltpu.PrefetchScalarGridSpec(
        num_scalar_prefetch=2, grid=(B, steps),
        in_specs=page_specs + [row((S, G * R * HEAD_DIM)), row((1, G, R * S, HEAD_DIM)),
                               row((1, G, steps + 1, S, 128)), row((lw * KV_ROWS, HEAD_DIM)),
                               row((1, NEW_PAD, W)), row((1, NEW_PAD, W)), row((S, G * 128)),
                               const((G, 3, R * S, TQ)), const((128, PP * PAGE_SIZE))],
        out_specs=row((S, G * R * HEAD_DIM)),
        scratch_shapes=[pltpu.VMEM((G, R * S, HEAD_DIM), BF16), pltpu.VMEM((G, R * S, 1), F32),
                        pltpu.VMEM((G, R * S, 1), F32), pltpu.VMEM((G, R * S, HEAD_DIM), F32)])
    return pl.pallas_call(
        functools.partial(_nsa_sample_kernel, S=S, n_pages=n_pages, lw=lw), grid_spec=grid_spec,
        name="nsa_sample_attn",
        out_shape=jax.ShapeDtypeStruct((B * S, G * R * HEAD_DIM), F32),
        compiler_params=pltpu.CompilerParams(dimension_semantics=("parallel", "arbitrary"),
                                             vmem_limit_bytes=V7X_VMEM_LIMIT_BYTES),
    )(page_table, flags, *([pool_sel] * PP), q, ocmp, selsc, win_buf, new_sel, new_win, gl, sb, e)


CONV_HALO = 32


def _rmsnorm_kernel(x_ref, g_ref, o_ref):
    x = x_ref[...]
    y = x * lax.rsqrt(jnp.mean(x * x, axis=-1, keepdims=True) + EPS) * g_ref[...]
    o_ref[...] = y.astype(o_ref.dtype)


def rmsnorm(x, g, out_dtype, tm=256):
    M, D = x.shape
    tm = _pick(M, tm)
    return pl.pallas_call(
        _rmsnorm_kernel, grid=(M // tm,), name="rmsnorm",
        in_specs=[pl.BlockSpec((tm, D), lambda m: (m, 0)), pl.BlockSpec((1, D), lambda m: (0, 0))],
        out_specs=pl.BlockSpec((tm, D), lambda m: (m, 0)),
        out_shape=jax.ShapeDtypeStruct((M, D), out_dtype),
        compiler_params=pltpu.CompilerParams(dimension_semantics=("parallel",),
                                             vmem_limit_bytes=V7X_VMEM_LIMIT_BYTES),
    )(x, g.reshape(1, D))


CONV_LANES = 512
CONV_ROWS = 64
SUBLANES = 8


def _conv_kernel(glu_ref, buf_ref, w_ref, b_ref, c_ref, st_ref, x_ref, *, tt):
    t = pl.program_id(2)
    lead = CONV_HALO - CONV_BUF

    @pl.when(t == 0)
    def _():
        x_ref[0:lead, :] = jnp.zeros((lead, x_ref.shape[1]), F32)
        x_ref[lead:CONV_HALO, :] = buf_ref[0]
        x_ref[CONV_HALO + tt:, :] = jnp.zeros((SUBLANES, x_ref.shape[1]), F32)

    x_ref[CONV_HALO:CONV_HALO + tt, :] = glu_ref[0]
    rc = min(tt, CONV_ROWS)
    for r0 in range(0, tt, rc):
        acc = jnp.zeros((rc, x_ref.shape[1]), F32) + b_ref[...]
        for phase in range(SUBLANES):
            taps = [k for k in range(CONV_WIDTH) if (lead + k) % SUBLANES == phase]
            part = None
            for k in taps:
                term = x_ref[pl.ds(r0 + lead + k - phase, rc + SUBLANES), :] * w_ref[k:k + 1, :]
                part = term if part is None else part + term
            acc = acc + part[phase:phase + rc, :]
        c_ref[0, r0:r0 + rc, :] = acc

    @pl.when(t == pl.num_programs(2) - 1)
    def _():
        st_ref[0] = x_ref[pl.ds(tt + lead, CONV_BUF), :]

    x_ref[0:CONV_HALO, :] = x_ref[pl.ds(tt, CONV_HALO), :]


def _ln_silu_kernel(c_ref, par_ref, z_ref):
    conv = c_ref[...]
    mu = jnp.mean(conv, axis=-1, keepdims=True)
    c = conv - mu
    var = jnp.mean(c * c, axis=-1, keepdims=True)
    y = c * lax.rsqrt(var + EPS) * par_ref[0:1, :] + par_ref[1:2, :]
    z_ref[...] = (y * jax.nn.sigmoid(y)).astype(z_ref.dtype)


def conv_ln_silu(glu, buf, dw_w, dw_b, ln_g, ln_b, tt=256, tm=256):
    B, S, D = glu.shape
    tt = _pick(S, tt)
    cw = _pick(D, CONV_LANES)
    assert tt % 8 == 0
    conv, state = pl.pallas_call(
        functools.partial(_conv_kernel, tt=tt), grid=(B, D // cw, S // tt), name="dwconv",
        in_specs=[pl.BlockSpec((1, tt, cw), lambda b, c, t: (b, t, c)),
                  pl.BlockSpec((1, CONV_BUF, cw), lambda b, c, t: (b, 0, c)),
                  pl.BlockSpec((CONV_WIDTH, cw), lambda b, c, t: (0, c)),
                  pl.BlockSpec((1, cw), lambda b, c, t: (0, c))],
        out_specs=[pl.BlockSpec((1, tt, cw), lambda b, c, t: (b, t, c)),
                   pl.BlockSpec((1, CONV_BUF, cw), lambda b, c, t: (b, 0, c))],
        out_shape=[jax.ShapeDtypeStruct((B, S, D), F32), jax.ShapeDtypeStruct((B, CONV_BUF, D), F32)],
        scratch_shapes=[pltpu.VMEM((CONV_HALO + tt + SUBLANES, cw), F32)],
        compiler_params=pltpu.CompilerParams(dimension_semantics=("parallel", "parallel", "arbitrary"),
                                             vmem_limit_bytes=V7X_VMEM_LIMIT_BYTES),
    )(glu, buf, dw_w, dw_b.reshape(1, D))
    M = B * S
    tm = _pick(M, tm)
    z = pl.pallas_call(
        _ln_silu_kernel, grid=(M // tm,), name="ln_silu",
        in_specs=[pl.BlockSpec((tm, D), lambda m: (m, 0)), pl.BlockSpec((2, D), lambda m: (0, 0))],
        out_specs=pl.BlockSpec((tm, D), lambda m: (m, 0)),
        out_shape=jax.ShapeDtypeStruct((M, D), BF16),
        compiler_params=pltpu.CompilerParams(dimension_semantics=("parallel",),
                                             vmem_limit_bytes=V7X_VMEM_LIMIT_BYTES),
    )(conv.reshape(M, D), jnp.stack([ln_g, ln_b]))
    return z, state


def rel_bucket(dist):
    n = jnp.maximum(dist, 0)
    max_exact = REL_BUCKETS // 2
    nf = jnp.maximum(n, 1).astype(jnp.float32)
    large = max_exact + (jnp.log(nf / max_exact) / math.log(REL_MAX_DIST / max_exact)
                         * (REL_BUCKETS - max_exact)).astype(jnp.int32)
    large = jnp.minimum(large, REL_BUCKETS - 1)
    return jnp.where(n < max_exact, n, large)


def bucket_bias(rel_table, dist):
    onehot = jax.nn.one_hot(rel_bucket(dist), REL_BUCKETS, dtype=F32)
    return jnp.dot(onehot, rel_table.astype(F32), precision=lax.Precision.HIGHEST)


def head_bias(rel_table, dist):
    b = bucket_bias(rel_table, dist)
    return b.reshape(dist.shape + (KV_GROUPS, HEADS_PER_GROUP)).transpose(2, 3, 0, 1)


def nsa_project(hb, p):
    kv_pairs = [matmul(hb, p["w_kv"][i], tall=True) for i in range(N_BRANCHES)]
    gl = matmul(hb, p["w_gate"])
    return [w for w, _ in kv_pairs], [t for _, t in kv_pairs], gl


def nsa_prompt(hb, q2, B, T, p, rel_table):
    kvs, kvt, gl = nsa_project(hb, p)
    ident = jnp.arange(B * T // PAGE_SIZE, dtype=jnp.int32).reshape(B, T // PAGE_SIZE)
    kc, vc = compress_paged(kvs[0], ident, p, tall=False)
    nb = T // BLOCK
    dist_c = jnp.arange(T)[:, None] - (jnp.arange(nb) * BLOCK + (BLOCK - 1))[None, :]
    o = nsa_prompt_attention(q2, kvs[1], kvs[2], kc, vc, gl,
                             toeplitz_bias(rel_table), head_bias(rel_table, dist_c), B, T)
    kv_cmp, kv_sel, kv_win = [kv.reshape(B, T, 2, KV_GROUPS, HEAD_DIM) for kv in kvt]
    return o, kv_cmp, kv_sel, kv_win[:, T - min(WINDOW, T):]


def nsa_sample(hb, q2, B, S, pool_cmp, pool_sel, win_buf, page_table, p, rel_table):
    W = 2 * KV_WIDTH
    n_pages = page_table.shape[1]
    kvs, kvt, gl = nsa_project(hb, p)
    assert (n_pages * PAGE_SIZE + S) // BLOCK == n_pages * PAGE_SIZE // BLOCK
    kc, vc = compress_paged(pool_cmp.reshape(-1, HEAD_DIM), page_table, p, tall=True)
    lw = win_buf.shape[1]
    o = nsa_sample_attention(q2, kc, vc, pool_sel.reshape(-1, HEAD_DIM), page_table,
                             win_buf.reshape(-1, HEAD_DIM), kvs[1].reshape(B, S, W), kvs[2].reshape(B, S, W),
                             gl, rel_table)
    kv_cmp, kv_sel, kv_win = [kv.reshape(B, S, 2, KV_GROUPS, HEAD_DIM) for kv in kvt]
    all_win = jnp.concatenate([win_buf, kv_win], axis=1)
    return o, kv_cmp, kv_sel, all_win[:, lw + S - min(WINDOW, lw + S):]


def ffn(xp, xs, g, w_up32, w_down32, layer):
    D_ff = w_up32.shape[2]
    hid_p, w_up = matmul_wcast(rmsnorm(xp, g, BF16), w_up32, layer, D_ff, act="relu2", out_dtype=BF16)
    xp, w_down = matmul_wcast_ktiled(hid_p, w_down32, layer, xp)
    hid_s = matmul(rmsnorm(xs, g, BF16), w_up, act="relu2", out_dtype=BF16)
    return xp, matmul(hid_s, w_down, res=xs, tk=2048)


def kernel(x_prompt, x_sample, state_conv, cache_kv_cmp, cache_kv_sel, state_kv_win, page_table, norm_mix_g, norm_ffn_g, norm_final_g, rel_bias_table, conv_w_in, conv_b_in, conv_dw_w, conv_dw_b, conv_ln_g, conv_ln_b, conv_w_out, conv_b_out, nsa_w_in, nsa_cmp_pos, nsa_cmp_w1, nsa_cmp_b1, nsa_cmp_w2, nsa_cmp_b2, nsa_w_out, ffn_w_up, ffn_w_down):
    Bp, T, D = x_prompt.shape
    Bs, S, _ = x_sample.shape
    xp, xs = x_prompt.reshape(Bp * T, D), x_sample.reshape(Bs * S, D)
    qd = N_HEADS * HEAD_DIM
    kvd = 2 * KV_WIDTH

    def conv_tail(glu, B, L, buf):
        return conv_ln_silu(glu.reshape(B, L, D), buf, conv_dw_w[0], conv_dw_b[0], conv_ln_g[0], conv_ln_b[0])

    glu_p, wa, wg = matmul_glu_wcast(rmsnorm(xp, norm_mix_g[0], BF16), conv_w_in, 0, conv_b_in[0])
    zp, conv_p = conv_tail(glu_p, Bp, T, jnp.zeros((Bp, CONV_BUF, D), F32))
    xp, w_out0 = matmul_wcast(zp, conv_w_out, 0, D, bias=conv_b_out[0], res=xp)
    glu_s = matmul_glu(rmsnorm(xs, norm_mix_g[0], BF16), wa, wg, conv_b_in[0][:D], conv_b_in[0][D:])
    zs, conv_s = conv_tail(glu_s, Bs, S, state_conv[0])
    xs = matmul(zs, w_out0, bias=conv_b_out[0], res=xs)
    xp, xs = ffn(xp, xs, norm_ffn_g[0], ffn_w_up, ffn_w_down, 0)

    w_in = nsa_w_in[0]
    nbr = N_BRANCHES * HEADS_PER_GROUP
    gate_w = w_in[:, qd + N_BRANCHES * kvd:].reshape(D, N_BRANCHES, KV_GROUPS, HEADS_PER_GROUP)
    gate_w = gate_w.transpose(0, 2, 1, 3).reshape(D, KV_GROUPS, nbr)
    gate_w = jnp.pad(gate_w, ((0, 0), (0, 0), (0, 128 - nbr))).reshape(D, KV_GROUPS * 128)
    npar = dict(w_kv=[w_in[:, qd + i * kvd: qd + (i + 1) * kvd].astype(BF16) for i in range(N_BRANCHES)],
                w_gate=gate_w.astype(BF16), pos=nsa_cmp_pos[0], w1=nsa_cmp_w1[0].astype(BF16),
                b1=nsa_cmp_b1[0], w2=nsa_cmp_w2[0].astype(BF16), b2=nsa_cmp_b2[0])
    hp = rmsnorm(xp, norm_mix_g[1], BF16)
    q2p, w_q = matmul_wcast(hp, nsa_w_in, 0, qd, out_dtype=BF16)
    op, kcp, ksp, kwp = nsa_prompt(hp, q2p, Bp, T, npar, rel_bias_table)
    xp, w_out1 = matmul_wcast(op, nsa_w_out, 0, D, res=xp)
    hs = rmsnorm(xs, norm_mix_g[1], BF16)
    os_, kcs, kss, kws = nsa_sample(hs, matmul(hs, w_q), Bs, S, cache_kv_cmp[0], cache_kv_sel[0],
                                    state_kv_win[0], page_table, npar, rel_bias_table)
    xs = matmul(os_.astype(BF16), w_out1, res=xs)
    xp, xs = ffn(xp, xs, norm_ffn_g[1], ffn_w_up, ffn_w_down, 1)

    y_prompt = rmsnorm(xp, norm_final_g, F32).reshape(Bp, T, D)
    y_sample = rmsnorm(xs, norm_final_g, F32).reshape(Bs, S, D)
    return (y_prompt, y_sample, conv_p[None], conv_s[None], kcp[None], kcs[None],
            ksp[None], kss[None], kwp[None], kws[None])
```

```python
import functools
import math

import jax
import jax.numpy as jnp
import numpy as np
from jax import lax
from jax.experimental import pallas as pl
from jax.experimental.pallas import tpu as pltpu

D_MODEL = 4096
CONV_WIDTH = 31
CONV_BUF = CONV_WIDTH - 1
N_HEADS = 32
HEAD_DIM = D_MODEL // N_HEADS
KV_GROUPS = 4
HEADS_PER_GROUP = N_HEADS // KV_GROUPS
KV_WIDTH = KV_GROUPS * HEAD_DIM
BLOCK = 64
N_SELECT = 16
WINDOW = 512
WIN_Q_BLOCK = 128
SEL_Q_CHUNK = 16
N_BRANCHES = 3
PAGE_SIZE = 128
REL_BUCKETS = 32
REL_MAX_DIST = 128
EPS = 1e-6
NEG_INF = -1e30

V7X_VMEM_LIMIT_BYTES = 56 * 1024 * 1024
BF16 = jnp.bfloat16
F32 = jnp.float32


def _apply_act(y, act):
    if act == "relu2":
        return jnp.square(jnp.maximum(y, 0.0))
    if act == "silu":
        return y * jax.nn.sigmoid(y)
    assert act is None
    return y


def _mm_kernel(*refs, act, has_bias, has_res, has_tall, nk):
    x_ref, w_ref = refs[0], refs[1]
    pos = 2
    b_ref = r_ref = None
    if has_bias:
        b_ref = refs[pos]
        pos += 1
    if has_res:
        r_ref = refs[pos]
        pos += 1
    o_ref = refs[pos]
    pos += 1
    tall_ref = None
    if has_tall:
        tall_ref = refs[pos]
        pos += 1
    acc_ref = refs[pos] if nk > 1 else None

    def epilogue(y):
        if has_bias:
            y = y + b_ref[...]
        y = _apply_act(y, act)
        if has_res:
            y = y + r_ref[...]
        o_ref[...] = y.astype(o_ref.dtype)
        if has_tall:
            n_slabs = y.shape[1] // HEAD_DIM
            for c in range(n_slabs):
                tall_ref[pl.ds(c, y.shape[0], stride=n_slabs), :] = y[:, c * HEAD_DIM:(c + 1) * HEAD_DIM]

    part = jnp.dot(x_ref[...], w_ref[...], preferred_element_type=F32)
    if nk == 1:
        epilogue(part)
        return
    k = pl.program_id(2)

    @pl.when(k == 0)
    def _():
        acc_ref[...] = part

    @pl.when(jnp.logical_and(k > 0, k < nk - 1))
    def _():
        acc_ref[...] += part

    @pl.when(k == nk - 1)
    def _():
        epilogue(acc_ref[...] + part)


def _pick(dim, pref):
    t = min(dim, pref)
    assert dim % t == 0, (dim, pref)
    return t


def matmul(x, w, bias=None, res=None, act=None, out_dtype=F32, tm=1024, tn=1024, tk=4096, tall=False):
    M, K = x.shape
    K2, N = w.shape
    assert K == K2
    tm, tn, tk = _pick(M, tm), _pick(N, tn), _pick(K, tk)
    nk = K // tk
    in_specs = [pl.BlockSpec((tm, tk), lambda m, n, k: (m, k)),
                pl.BlockSpec((tk, tn), lambda m, n, k: (k, n))]
    args = [x, w]
    if bias is not None:
        in_specs.append(pl.BlockSpec((1, tn), lambda m, n, k: (0, n)))
        args.append(bias.reshape(1, N).astype(F32))
    if res is not None:
        in_specs.append(pl.BlockSpec((tm, tn), lambda m, n, k: (m, n)))
        args.append(res)
    scratch = [pltpu.VMEM((tm, tn), F32)] if nk > 1 else []
    out_specs = pl.BlockSpec((tm, tn), lambda m, n, k: (m, n))
    out_shape = jax.ShapeDtypeStruct((M, N), out_dtype)
    if tall:
        assert tn == N and N % HEAD_DIM == 0
        slabs = N // HEAD_DIM
        out_specs = [out_specs, pl.BlockSpec((tm * slabs, HEAD_DIM), lambda m, n, k: (m, 0))]
        out_shape = [out_shape, jax.ShapeDtypeStruct((M * slabs, HEAD_DIM), F32)]
    return pl.pallas_call(
        functools.partial(_mm_kernel, act=act, has_bias=bias is not None,
                          has_res=res is not None, has_tall=tall, nk=nk),
        grid=(M // tm, N // tn, nk),
        in_specs=in_specs,
        out_specs=out_specs,
        out_shape=out_shape,
        scratch_shapes=scratch,
        compiler_params=pltpu.CompilerParams(
            dimension_semantics=("parallel", "parallel", "arbitrary"),
            vmem_limit_bytes=V7X_VMEM_LIMIT_BYTES),
    )(*args)


def _glu_kernel(x_ref, wa_ref, wg_ref, ba_ref, bg_ref, o_ref):
    x = x_ref[...]
    a = jnp.dot(x, wa_ref[...], preferred_element_type=F32) + ba_ref[...]
    g = jnp.dot(x, wg_ref[...], preferred_element_type=F32) + bg_ref[...]
    o_ref[...] = a * jax.nn.sigmoid(g)


def matmul_glu(x, wa, wg, ba, bg, tm=1024, tn=512):
    M, K = x.shape
    N = wa.shape[1]
    tm, tn = _pick(M, tm), _pick(N, tn)
    wspec = pl.BlockSpec((K, tn), lambda m, n: (0, n))
    bspec = pl.BlockSpec((1, tn), lambda m, n: (0, n))
    return pl.pallas_call(
        _glu_kernel,
        grid=(M // tm, N // tn),
        in_specs=[pl.BlockSpec((tm, K), lambda m, n: (m, 0)), wspec, wspec, bspec, bspec],
        out_specs=pl.BlockSpec((tm, tn), lambda m, n: (m, n)),
        out_shape=jax.ShapeDtypeStruct((M, N), F32),
        compiler_params=pltpu.CompilerParams(
            dimension_semantics=("parallel", "parallel"),
            vmem_limit_bytes=V7X_VMEM_LIMIT_BYTES),
    )(x, wa, wg, ba.reshape(1, N), bg.reshape(1, N))


def _mm_wcast_kernel(*refs, act, has_bias, has_res):
    x_ref, w_ref = refs[0], refs[1]
    pos = 2
    b_ref = r_ref = None
    if has_bias:
        b_ref = refs[pos]
        pos += 1
    if has_res:
        r_ref = refs[pos]
        pos += 1
    o_ref, wbf_ref = refs[pos], refs[pos + 1]

    @pl.when(pl.program_id(1) == 0)
    def _():
        wbf_ref[...] = w_ref[...].astype(BF16)

    y = jnp.dot(x_ref[...], wbf_ref[...], preferred_element_type=F32)
    if has_bias:
        y = y + b_ref[...]
    y = _apply_act(y, act)
    if has_res:
        y = y + r_ref[...]
    o_ref[...] = y.astype(o_ref.dtype)


def matmul_wcast(x, w32, layer, n_cols, bias=None, res=None, act=None, out_dtype=F32, tm=1024, tn=512):
    M, K = x.shape
    assert w32.shape[1] == K and n_cols <= w32.shape[2]
    N = n_cols
    tm, tn = _pick(M, tm), _pick(N, tn)
    in_specs = [pl.BlockSpec((tm, K), lambda n, m: (m, 0)),
                pl.BlockSpec((None, K, tn), lambda n, m: (layer, 0, n))]
    args = [x, w32]
    if bias is not None:
        in_specs.append(pl.BlockSpec((1, tn), lambda n, m: (0, n)))
        args.append(bias.reshape(1, N).astype(F32))
    if res is not None:
        in_specs.append(pl.BlockSpec((tm, tn), lambda n, m: (m, n)))
        args.append(res)
    return pl.pallas_call(
        functools.partial(_mm_wcast_kernel, act=act, has_bias=bias is not None, has_res=res is not None),
        grid=(N // tn, M // tm), name="mm_wcast",
        in_specs=in_specs,
        out_specs=[pl.BlockSpec((tm, tn), lambda n, m: (m, n)), pl.BlockSpec((K, tn), lambda n, m: (0, n))],
        out_shape=[jax.ShapeDtypeStruct((M, N), out_dtype), jax.ShapeDtypeStruct((K, N), BF16)],
        compiler_params=pltpu.CompilerParams(
            dimension_semantics=("parallel", "arbitrary"),
            vmem_limit_bytes=V7X_VMEM_LIMIT_BYTES),
    )(*args)


def _mm_wcast_ktiled_kernel(x_ref, w_ref, r_ref, o_ref, wbf_ref, acc_ref, *, nk, tm):
    k = pl.program_id(1)
    m = pl.program_id(2)

    @pl.when(m == 0)
    def _():
        wbf_ref[...] = w_ref[...].astype(BF16)

    part = jnp.dot(x_ref[...], wbf_ref[...], preferred_element_type=F32)
    rows = pl.ds(pl.multiple_of(m * tm, tm), tm)

    @pl.when(k == 0)
    def _():
        acc_ref[rows, :] = part

    @pl.when(jnp.logical_and(k > 0, k < nk - 1))
    def _():
        acc_ref[rows, :] += part

    @pl.when(k == nk - 1)
    def _():
        o_ref[...] = acc_ref[rows, :] + part + r_ref[...]


def matmul_wcast_ktiled(x, w32, layer, res, tm=1024, tn=512, tk=2048):
    M, K = x.shape
    N = w32.shape[2]
    tm, tn, tk = _pick(M, tm), _pick(N, tn), _pick(K, tk)
    nk = K // tk
    assert nk >= 2
    last_rows = lambda n, k, m: (jnp.where(k == nk - 1, m, 0), n)
    return pl.pallas_call(
        functools.partial(_mm_wcast_ktiled_kernel, nk=nk, tm=tm),
        grid=(N // tn, nk, M // tm), name="mm_wcast_ktiled",
        in_specs=[pl.BlockSpec((tm, tk), lambda n, k, m: (m, k)),
                  pl.BlockSpec((None, tk, tn), lambda n, k, m: (layer, k, n)),
                  pl.BlockSpec((tm, tn), last_rows)],
        out_specs=[pl.BlockSpec((tm, tn), last_rows), pl.BlockSpec((tk, tn), lambda n, k, m: (k, n))],
        out_shape=[jax.ShapeDtypeStruct((M, N), F32), jax.ShapeDtypeStruct((K, N), BF16)],
        scratch_shapes=[pltpu.VMEM((M, tn), F32)],
        compiler_params=pltpu.CompilerParams(
            dimension_semantics=("parallel", "arbitrary", "arbitrary"),
            vmem_limit_bytes=V7X_VMEM_LIMIT_BYTES),
    )(x, w32, res)


def _glu_wcast_kernel(x_ref, wa_ref, wg_ref, ba_ref, bg_ref, o_ref, wabf_ref, wgbf_ref):
    @pl.when(pl.program_id(1) == 0)
    def _():
        wabf_ref[...] = wa_ref[...].astype(BF16)
        wgbf_ref[...] = wg_ref[...].astype(BF16)

    x = x_ref[...]
    a = jnp.dot(x, wabf_ref[...], preferred_element_type=F32) + ba_ref[...]
    g = jnp.dot(x, wgbf_ref[...], preferred_element_type=F32) + bg_ref[...]
    o_ref[...] = a * jax.nn.sigmoid(g)


def matmul_glu_wcast(x, w32, layer, b, tm=1024, tn=256):
    M, K = x.shape
    N = w32.shape[2] // 2
    tm, tn = _pick(M, tm), _pick(N, tn)
    nt = N // tn
    wspec = pl.BlockSpec((K, tn), lambda n, m: (0, n))
    return pl.pallas_call(
        _glu_wcast_kernel, grid=(nt, M // tm), name="mm_glu_wcast",
        in_specs=[pl.BlockSpec((tm, K), lambda n, m: (m, 0)),
                  pl.BlockSpec((None, K, tn), lambda n, m: (layer, 0, n)),
                  pl.BlockSpec((None, K, tn), lambda n, m: (layer, 0, nt + n)),
                  pl.BlockSpec((1, tn), lambda n, m: (0, n)), pl.BlockSpec((1, tn), lambda n, m: (0, nt + n))],
        out_specs=[pl.BlockSpec((tm, tn), lambda n, m: (m, n)), wspec, wspec],
        out_shape=[jax.ShapeDtypeStruct((M, N), F32), jax.ShapeDtypeStruct((K, N), BF16),
                   jax.ShapeDtypeStruct((K, N), BF16)],
        compiler_params=pltpu.CompilerParams(
            dimension_semantics=("parallel", "arbitrary"),
            vmem_limit_bytes=V7X_VMEM_LIMIT_BYTES),
    )(x, w32, w32, b.reshape(1, 2 * N), b.reshape(1, 2 * N))


TQ = 128
SEL_KT = 4 * TQ
IMP_FORCED = 16.0
IMP_FUTURE = -1.0
_NT = (((1,), (1,)), ((), ()))
_TN = (((0,), (0,)), ((), ()))


def _nsa_prompt_kernel(q_ref, ks_ref, vs_ref, kw_ref, vw_ref, kc_ref, vc_ref, gl_ref, toe_ref, cb_ref,
                       e_ref, o_ref, selm_ref, m_ref, l_ref, acc_ref, *, T):
    R = HEADS_PER_GROUP
    nb = T // BLOCK
    nqt = T // TQ
    i = pl.program_id(2)
    scale = HEAD_DIM ** -0.5
    qb = q_ref[...]
    qm = jnp.concatenate([qb[:, r * HEAD_DIM:(r + 1) * HEAD_DIM] for r in range(R)], axis=0)
    t_loc = lax.broadcasted_iota(jnp.int32, (TQ, 1), 0)
    q_pos = i * TQ + t_loc

    kc = kc_ref[0, 0].astype(BF16)
    vc = vc_ref[0, 0].astype(BF16)
    sc = lax.dot_general(kc, qm, _NT, preferred_element_type=F32) * scale + cb_ref[0, 0]
    jb = lax.broadcasted_iota(jnp.int32, (nb, R * TQ), 0)
    q_lane = i * TQ + lax.broadcasted_iota(jnp.int32, (1, R * TQ), 1) % TQ
    valid = (q_lane - (jb * BLOCK + (BLOCK - 1))) >= 0
    sc = jnp.where(valid, sc, NEG_INF)
    ec = jnp.where(valid, jnp.exp(sc - jnp.max(sc, axis=0, keepdims=True)), 0.0)
    lc = jnp.sum(ec, axis=0, keepdims=True)
    pc = ec / jnp.where(lc > 0.0, lc, 1.0)
    o_cmp = lax.dot_general(pc.astype(BF16), vc, _TN, preferred_element_type=F32).reshape(R, TQ, HEAD_DIM)
    imp = pc[:, 0:TQ]
    for r in range(1, R):
        imp = imp + pc[:, r * TQ:(r + 1) * TQ]

    jn = lax.broadcasted_iota(jnp.int32, (nb, TQ), 0)
    cur = (i * TQ + lax.broadcasted_iota(jnp.int32, (1, TQ), 1)) // BLOCK
    forced = (jn == 0) | (jn == cur) | (jn == cur - 1)
    impp = jnp.where(jn > cur, IMP_FUTURE, jnp.where(forced, IMP_FORCED, imp))
    rank = jnp.zeros((nb, TQ), jnp.int32)
    for jp in range(nb):
        row = impp[jp:jp + 1, :]
        beats = (row > impp) | ((row == impp) & (jn > jp))
        rank = rank + beats.astype(jnp.int32)
    sel = (rank < min(N_SELECT, nb)).astype(BF16)
    selm = lax.dot_general(sel, e_ref[...], _TN, preferred_element_type=F32)
    for st in range(T // SEL_KT):
        selm_ref[st] = selm[:, st * SEL_KT:(st + 1) * SEL_KT]

    def with_ones(v):
        return jnp.concatenate([v.astype(BF16), jnp.ones(v.shape, BF16)], axis=1)

    def scores(k_t, kt0, n_sub):
        s = lax.dot_general(qm, k_t, _NT, preferred_element_type=F32) * scale
        bias = jnp.concatenate([toe_ref[0, jnp.clip(i - kt0 - c, 0, 2)] for c in range(n_sub)], axis=1)
        j_loc = lax.broadcasted_iota(jnp.int32, (1, n_sub * TQ), 1)
        dist = (i - kt0) * TQ + t_loc - j_loc
        return (s + bias).reshape(R, TQ, n_sub * TQ), dist

    m_ref[...] = jnp.full(m_ref.shape, NEG_INF, F32)
    l_ref[...] = jnp.zeros(l_ref.shape, F32)
    acc_ref[...] = jnp.zeros(acc_ref.shape, F32)
    sub = SEL_KT // TQ

    def sel_body(st, carry):
        ks = pl.multiple_of(st * SEL_KT, SEL_KT)
        s, dist = scores(ks_ref[pl.ds(ks, SEL_KT), :].astype(BF16), st * sub, sub)
        mask = (dist >= 0) & (selm_ref[st] > 0.5)
        s = jnp.where(mask[None], s, NEG_INF)
        m_old = m_ref[...]
        m_new = jnp.maximum(m_old, jnp.max(s, axis=-1, keepdims=True))
        p = jnp.exp(s - m_new)
        alpha = jnp.exp(m_old - m_new)
        pv = jnp.dot(p.reshape(R * TQ, SEL_KT).astype(BF16), with_ones(vs_ref[pl.ds(ks, SEL_KT), :]),
                     preferred_element_type=F32)
        acc_ref[...] = alpha * acc_ref[...] + pv[:, :HEAD_DIM].reshape(R, TQ, HEAD_DIM)
        l_ref[...] = alpha * l_ref[...] + pv[:, HEAD_DIM:].reshape(R, TQ, HEAD_DIM)
        m_ref[...] = m_new
        return carry

    lax.fori_loop(0, i // sub + 1, sel_body, 0)
    o_sel = acc_ref[...] / l_ref[...]

    n_win = WINDOW // TQ + 1
    kt0 = jnp.clip(i - WINDOW // TQ, 0, nqt - n_win)
    ws = pl.multiple_of(kt0 * TQ, TQ)
    s, dist = scores(kw_ref[pl.ds(ws, n_win * TQ), :].astype(BF16), kt0, n_win)
    s = jnp.where(((dist >= 0) & (dist <= WINDOW))[None], s, NEG_INF)
    p = jnp.exp(s - jnp.max(s, axis=-1, keepdims=True))
    pv = jnp.dot(p.reshape(R * TQ, n_win * TQ).astype(BF16), with_ones(vw_ref[pl.ds(ws, n_win * TQ), :]),
                 preferred_element_type=F32)
    o_win = (pv[:, :HEAD_DIM] / pv[:, HEAD_DIM:]).reshape(R, TQ, HEAD_DIM)

    gates = jax.nn.sigmoid(gl_ref[...])
    for r in range(R):
        o = (gates[:, r:r + 1] * o_cmp[r] + gates[:, R + r:R + r + 1] * o_sel[r]
             + gates[:, 2 * R + r:2 * R + r + 1] * o_win[r])
        o_ref[:, r * HEAD_DIM:(r + 1) * HEAD_DIM] = o.astype(o_ref.dtype)


def nsa_prompt_attention(q2, kv_sel, kv_win, kc, vc, gl, toe, cb, B, T):
    G, R = KV_GROUPS, HEADS_PER_GROUP
    nb, nqt = T // BLOCK, T // TQ
    assert T % SEL_KT == 0 and nqt >= WINDOW // TQ + 1
    expand = (np.arange(T)[None, :] // BLOCK == np.arange(nb)[:, None])
    e = jnp.asarray(expand, BF16)
    qspec = pl.BlockSpec((TQ, R * HEAD_DIM), lambda b, g, i: (b * nqt + i, g))
    kspec = pl.BlockSpec((T, HEAD_DIM), lambda b, g, i: (b, g))
    vspec = pl.BlockSpec((T, HEAD_DIM), lambda b, g, i: (b, G + g))
    cspec = pl.BlockSpec((1, 1, nb, HEAD_DIM), lambda b, g, i: (b, g, 0, 0))
    return pl.pallas_call(
        functools.partial(_nsa_prompt_kernel, T=T), name="nsa_prompt_attn",
        grid=(B, G, nqt),
        in_specs=[qspec, kspec, vspec, kspec, vspec, cspec, cspec,
                  pl.BlockSpec((TQ, 128), lambda b, g, i: (b * nqt + i, g)),
                  pl.BlockSpec((1, 3, R * TQ, TQ), lambda b, g, i: (g, 0, 0, 0)),
                  pl.BlockSpec((1, 1, nb, R * TQ), lambda b, g, i: (g, i, 0, 0)),
                  pl.BlockSpec((nb, T), lambda b, g, i: (0, 0))],
        out_specs=qspec,
        out_shape=jax.ShapeDtypeStruct((B * T, G * R * HEAD_DIM), BF16),
        scratch_shapes=[pltpu.VMEM((T // SEL_KT, TQ, SEL_KT), F32), pltpu.VMEM((R, TQ, 1), F32),
                        pltpu.VMEM((R, TQ, HEAD_DIM), F32), pltpu.VMEM((R, TQ, HEAD_DIM), F32)],
        compiler_params=pltpu.CompilerParams(
            dimension_semantics=("parallel", "parallel", "arbitrary"),
            vmem_limit_bytes=V7X_VMEM_LIMIT_BYTES),
    )(q2, kv_sel, kv_sel, kv_win, kv_win, kc, vc, gl, toe, cb, e)


def toeplitz_bias(rel_table):
    d = jnp.arange(3)[:, None, None] * TQ + jnp.arange(TQ)[None, :, None] - jnp.arange(TQ)[None, None, :]
    b = bucket_bias(rel_table, d)
    b = b.reshape(3, TQ, TQ, KV_GROUPS, HEADS_PER_GROUP).transpose(3, 0, 4, 1, 2)
    return b.reshape(KV_GROUPS, 3, HEADS_PER_GROUP * TQ, TQ)


NEW_PAD = TQ
PAGES_PER_STEP = 4
SEL_DONE = -3.0
SEL_PAD = -2.0


BLOCKIFY_PAGES = 8
KV_ROWS = 2 * KV_GROUPS


def _blockify_kernel(pt_ref, *refs, tall):
    del pt_ref
    NP = BLOCKIFY_PAGES
    pages, pos_ref, o_ref, xs_ref = refs[:NP], refs[NP], refs[NP + 1], refs[NP + 2]
    rows = pages[0].shape[0]
    for pi, pg in enumerate(pages):
        if tall:
            xs_ref[pi * rows:(pi + 1) * rows, :] = pg[...]
        else:
            for c in range(KV_ROWS):
                xs_ref[c, pi * rows:(pi + 1) * rows, :] = pg[:, c * HEAD_DIM:(c + 1) * HEAD_DIM]
    nblk = NP * PAGE_SIZE // BLOCK
    for kv in range(2):
        for g in range(KV_GROUPS):
            c = kv * KV_GROUPS + g
            for j in range(BLOCK):
                if tall:
                    v = xs_ref[pl.ds(j * KV_ROWS + c, nblk, stride=BLOCK * KV_ROWS), :]
                else:
                    v = xs_ref[c, pl.ds(j, nblk, stride=BLOCK), :]
                o_ref[kv, 0, g, :, j * HEAD_DIM:(j + 1) * HEAD_DIM] = (v + pos_ref[kv, j:j + 1, :]).astype(o_ref.dtype)


def blockify(pool, page_table, pos, tall):
    B, n_pages = page_table.shape
    NP = BLOCKIFY_PAGES
    assert n_pages % NP == 0
    blk = (PAGE_SIZE * KV_ROWS, HEAD_DIM) if tall else (PAGE_SIZE, KV_ROWS * HEAD_DIM)
    page_specs = [pl.BlockSpec(blk, functools.partial(lambda b, s, pt, k: (pt[b, s * NP + k], 0), k=k))
                  for k in range(NP)]
    nbs = NP * PAGE_SIZE // BLOCK
    grid_spec = pltpu.PrefetchScalarGridSpec(
        num_scalar_prefetch=1, grid=(B, n_pages // NP),
        in_specs=page_specs + [pl.BlockSpec((2, BLOCK, HEAD_DIM), lambda b, s, pt: (0, 0, 0))],
        out_specs=pl.BlockSpec((2, 1, KV_GROUPS, nbs, BLOCK * HEAD_DIM), lambda b, s, pt: (0, b, 0, s, 0)),
        scratch_shapes=[pltpu.VMEM((NP * PAGE_SIZE * KV_ROWS, HEAD_DIM) if tall else
                                   (KV_ROWS, NP * PAGE_SIZE, HEAD_DIM), F32)])
    return pl.pallas_call(
        functools.partial(_blockify_kernel, tall=tall), grid_spec=grid_spec, name="blockify",
        out_shape=jax.ShapeDtypeStruct((2, B, KV_GROUPS, n_pages * PAGE_SIZE // BLOCK, BLOCK * HEAD_DIM), BF16),
        compiler_params=pltpu.CompilerParams(dimension_semantics=("parallel", "arbitrary"),
                                             vmem_limit_bytes=V7X_VMEM_LIMIT_BYTES),
    )(page_table, *([pool] * NP), pos)


def _compress_mlp_kernel(a_ref, w1_ref, b1_ref, w2_ref, b2_ref, o_ref):
    h = jnp.dot(a_ref[0], w1_ref[0], preferred_element_type=F32) + b1_ref[0]
    h = h * jax.nn.sigmoid(h)
    o_ref[0] = jnp.dot(h.astype(BF16), w2_ref[0], preferred_element_type=F32) + b2_ref[0]


def compress_mlp(a, w1, b1, w2, b2, tm=512):
    _, M, K = a.shape
    H = w1.shape[-1]
    tm = _pick(M, tm)
    return pl.pallas_call(
        _compress_mlp_kernel, grid=(2, M // tm), name="compress_mlp",
        in_specs=[pl.BlockSpec((1, tm, K), lambda k, m: (k, m, 0)),
                  pl.BlockSpec((1, K, H), lambda k, m: (k, 0, 0)),
                  pl.BlockSpec((1, 1, H), lambda k, m: (k, 0, 0)),
                  pl.BlockSpec((1, H, HEAD_DIM), lambda k, m: (k, 0, 0)),
                  pl.BlockSpec((1, 1, HEAD_DIM), lambda k, m: (k, 0, 0))],
        out_specs=pl.BlockSpec((1, tm, HEAD_DIM), lambda k, m: (k, m, 0)),
        out_shape=jax.ShapeDtypeStruct((2, M, HEAD_DIM), F32),
        compiler_params=pltpu.CompilerParams(dimension_semantics=("parallel", "parallel"),
                                             vmem_limit_bytes=V7X_VMEM_LIMIT_BYTES),
    )(a, w1, b1.reshape(2, 1, H), w2, b2.reshape(2, 1, HEAD_DIM))


def compress_paged(pool, page_table, p, tall):
    a = blockify(pool, page_table, p["pos"], tall)
    _, B, G, nb = a.shape[:4]
    out = compress_mlp(a.reshape(2, B * G * nb, BLOCK * HEAD_DIM), p["w1"], p["b1"], p["w2"], p["b2"])
    out = out.reshape(2, B, G, nb, HEAD_DIM)
    return out[0], out[1]


def _expand_rows(x8):
    return jnp.concatenate([x8] * HEADS_PER_GROUP, axis=0)


def _head_rows(q, g):
    R = HEADS_PER_GROUP
    return jnp.concatenate([q[:, (g * R + r) * HEAD_DIM:(g * R + r + 1) * HEAD_DIM] for r in range(R)], axis=0)


def _nsa_sample_select_kernel(q_ref, kc_ref, vc_ref, cb_ref, ocmp_ref, selsc_ref, sel_ref, *, S, n_pages):
    G, R = KV_GROUPS, HEADS_PER_GROUP
    RS = R * S
    nbc = kc_ref.shape[2]
    nbp = sel_ref.shape[3]
    n_blocks = -(-(n_pages * PAGE_SIZE + S) // BLOCK)
    past_len = n_pages * PAGE_SIZE
    scale = HEAD_DIM ** -0.5
    qi = lax.broadcasted_iota(jnp.int32, (RS, 1), 0) % S
    q = q_ref[...]
    jn = lax.broadcasted_iota(jnp.int32, (S, nbp), 1)
    jf = jn.astype(F32)
    cur = (past_len + lax.broadcasted_iota(jnp.int32, (S, 1), 0)) // BLOCK
    forced = (jn == 0) | (jn == cur) | (jn == cur - 1)
    jc = lax.broadcasted_iota(jnp.int32, (RS, nbc), 1)
    valid = (past_len + qi - (jc * BLOCK + (BLOCK - 1))) >= 0
    for g in range(G):
        qg = _head_rows(q, g).astype(BF16)
        sc = lax.dot_general(qg, kc_ref[0, g].astype(BF16), _NT, preferred_element_type=F32) * scale
        sc = jnp.where(valid, sc + cb_ref[g], NEG_INF)
        ec = jnp.where(valid, jnp.exp(sc - jnp.max(sc, axis=-1, keepdims=True)), 0.0)
        lc = jnp.sum(ec, axis=-1, keepdims=True)
        pc = ec / jnp.where(lc > 0.0, lc, 1.0)
        ocmp_ref[0, g] = jnp.dot(pc.astype(BF16), vc_ref[0, g].astype(BF16), preferred_element_type=F32)
        imp = pc[0:S]
        for r in range(1, R):
            imp = imp + pc[r * S:(r + 1) * S]
        imp = jnp.concatenate([imp, jnp.zeros((S, nbp - nbc), F32)], axis=1)
        work = jnp.where(jn > cur, IMP_FUTURE, jnp.where(forced, IMP_FORCED, imp))
        work = jnp.where(jn >= n_blocks, SEL_PAD, work)
        sel = jnp.zeros((S, nbp), F32)
        for _ in range(min(N_SELECT, n_blocks)):
            mx = jnp.max(work, axis=-1, keepdims=True)
            first = jnp.min(jnp.where(work == mx, jf, float(nbp)), axis=-1, keepdims=True)
            hit = jf == first
            sel = jnp.where(hit, 1.0, sel)
            work = jnp.where(hit, SEL_DONE, work)
        sel_ref[0, g] = sel
        for st in range(nbp // 8):
            selsc_ref[0, g, st] = jnp.concatenate(
                [sel[:, st * 8:(st + 1) * 8], jnp.zeros((S, 128 - 8), F32)], axis=1)


def _nsa_sample_kernel(pt_ref, fl_ref, *refs, S, n_pages, lw):
    del pt_ref
    PP = PAGES_PER_STEP
    pages = refs[:PP]
    (q_ref, ocmp_ref, selsc_ref, winb_ref, nsel_ref, nwin_ref, gl_ref, sb_ref, e_ref,
     o_ref, qm_ref, m_ref, l_ref, acc_ref) = refs[PP:]
    G, R = KV_GROUPS, HEADS_PER_GROUP
    RS = R * S
    steps = n_pages // PP
    b = pl.program_id(0)
    s_id = pl.program_id(1)
    scale = HEAD_DIM ** -0.5
    qi = lax.broadcasted_iota(jnp.int32, (RS, 1), 0) % S
    jl = lax.broadcasted_iota(jnp.int32, (1, TQ), 1)

    def attend(g, k_t, v_t, bias, mask):
        s = lax.dot_general(qm_ref[g], k_t.astype(BF16), _NT, preferred_element_type=F32) * scale + bias
        s = jnp.where(mask, s, NEG_INF)
        m_old = m_ref[g]
        m_new = jnp.maximum(m_old, jnp.max(s, axis=-1, keepdims=True))
        p = jnp.where(mask, jnp.exp(s - m_new), 0.0)
        alpha = jnp.exp(m_old - m_new)
        v_aug = jnp.concatenate([v_t.astype(BF16), jnp.ones(v_t.shape, BF16)], axis=1)
        pv = jnp.dot(p.astype(BF16), v_aug, preferred_element_type=F32)
        l_ref[g] = alpha * l_ref[g] + pv[:, HEAD_DIM:]
        acc_ref[g] = alpha * acc_ref[g] + pv[:, :HEAD_DIM]
        m_ref[g] = m_new

    def reset_state():
        m_ref[...] = jnp.full(m_ref.shape, NEG_INF, F32)
        l_ref[...] = jnp.zeros(l_ref.shape, F32)
        acc_ref[...] = jnp.zeros(acc_ref.shape, F32)

    @pl.when(s_id == 0)
    def _():
        reset_state()
        q = q_ref[...]
        for g in range(G):
            qm_ref[g] = _head_rows(q, g).astype(BF16)

    flags = [fl_ref[b * G + g, s_id * PP + pi] for g in range(G) for pi in range(PP)]

    @pl.when(sum(flags) > 0)
    def _():
        bias_tbl = [jnp.minimum(n_pages - (s_id * PP + pi), 2) for pi in range(PP)]
        for g in range(G):
            selm = jnp.dot(selsc_ref[0, g, s_id].astype(BF16), e_ref[...], preferred_element_type=F32)
            k_t = jnp.concatenate([pg[pl.ds(g, PAGE_SIZE, stride=KV_ROWS), :] for pg in pages], axis=0)
            v_t = jnp.concatenate([pg[pl.ds(G + g, PAGE_SIZE, stride=KV_ROWS), :] for pg in pages], axis=0)
            bias = jnp.concatenate([sb_ref[g, t] for t in bias_tbl], axis=1)
            attend(g, k_t, v_t, bias, _expand_rows(selm) > 0.5)

    @pl.when(s_id == steps - 1)
    def _():
        dist_new = qi - jl
        o_sel, o_win = [], []
        for g in range(G):
            selm = jnp.dot(selsc_ref[0, g, steps].astype(BF16), e_ref[...], preferred_element_type=F32)
            mask = (_expand_rows(selm[:, :TQ]) > 0.5) & (dist_new >= 0)
            attend(g, nsel_ref[0, :, g * HEAD_DIM:(g + 1) * HEAD_DIM],
                   nsel_ref[0, :, (G + g) * HEAD_DIM:(G + g + 1) * HEAD_DIM], sb_ref[g, 0], mask)
            o_sel.append(acc_ref[g] / l_ref[g])
        reset_state()
        for g in range(G):
            for t in range(lw // TQ):
                base = lw - t * TQ
                dist = base + qi - jl
                attend(g, winb_ref[pl.ds(t * TQ * KV_ROWS + g, TQ, stride=KV_ROWS), :],
                       winb_ref[pl.ds(t * TQ * KV_ROWS + G + g, TQ, stride=KV_ROWS), :],
                       sb_ref[g, min(base // TQ, 2)], (dist >= 0) & (dist <= WINDOW))
            attend(g, nwin_ref[0, :, g * HEAD_DIM:(g + 1) * HEAD_DIM],
                   nwin_ref[0, :, (G + g) * HEAD_DIM:(G + g + 1) * HEAD_DIM], sb_ref[g, 0],
                   (dist_new >= 0) & (dist_new <= WINDOW))
            o_win.append(acc_ref[g] / l_ref[g])
        gates = jax.nn.sigmoid(gl_ref[...])
        for g in range(G):
            oc = ocmp_ref[0, g]
            for r in range(R):
                c = g * 128 + r
                rows = slice(r * S, (r + 1) * S)
                o = (gates[:, c:c + 1] * oc[rows] + gates[:, c + R:c + R + 1] * o_sel[g][rows]
                     + gates[:, c + 2 * R:c + 2 * R + 1] * o_win[g][rows])
                o_ref[:, (g * R + r) * HEAD_DIM:(g * R + r + 1) * HEAD_DIM] = o


def sample_bias_tables(rel_table, S, past_len, nbc):
    G, R = KV_GROUPS, HEADS_PER_GROUP
    d = jnp.arange(3)[:, None, None] * TQ + jnp.arange(S)[None, :, None] - jnp.arange(TQ)[None, None, :]
    b = bucket_bias(rel_table, d).reshape(3, S, TQ, G, R)
    sb = b.transpose(3, 0, 4, 1, 2).reshape(G, 3, R * S, TQ)
    dc = past_len + jnp.arange(S)[:, None] - (jnp.arange(nbc) * BLOCK + (BLOCK - 1))[None, :]
    cb = head_bias(rel_table, dc).reshape(G, R * S, nbc)
    return sb, cb


def nsa_sample_attention(q, kc, vc, pool_sel, page_table, win_buf, new_sel, new_win, gl, rel_table):
    G, R = KV_GROUPS, HEADS_PER_GROUP
    B, n_pages = page_table.shape
    S = new_sel.shape[1]
    lw = win_buf.shape[0] // (B * KV_ROWS)
    W = 2 * KV_WIDTH
    nbc = kc.shape[2]
    PP = PAGES_PER_STEP
    steps = n_pages // PP
    assert n_pages % PP == 0 and lw % TQ == 0 and S <= BLOCK and PP * PAGE_SIZE // BLOCK == 8
    sb, cb = sample_bias_tables(rel_table, S, n_pages * PAGE_SIZE, nbc)
    nbp = (steps + 1) * 8

    brow = lambda shape: pl.BlockSpec(shape, lambda b: (b,) + (0,) * (len(shape) - 1))
    ocmp, selsc, sel = pl.pallas_call(
        functools.partial(_nsa_sample_select_kernel, S=S, n_pages=n_pages), grid=(B,), name="nsa_sample_select",
        in_specs=[brow((S, G * R * HEAD_DIM)), brow((1, G, nbc, HEAD_DIM)), brow((1, G, nbc, HEAD_DIM)),
                  pl.BlockSpec((G, R * S, nbc), lambda b: (0, 0, 0))],
        out_specs=[brow((1, G, R * S, HEAD_DIM)), brow((1, G, steps + 1, S, 128)), brow((1, G, S, nbp))],
        out_shape=[jax.ShapeDtypeStruct((B, G, R * S, HEAD_DIM), F32),
                   jax.ShapeDtypeStruct((B, G, steps + 1, S, 128), F32),
                   jax.ShapeDtypeStruct((B, G, S, nbp), F32)],
        compiler_params=pltpu.CompilerParams(dimension_semantics=("parallel",),
                                             vmem_limit_bytes=V7X_VMEM_LIMIT_BYTES),
    )(q, kc, vc, cb)
    bpp = PAGE_SIZE // BLOCK
    flags = sel[..., :n_pages * bpp].reshape(B, G, S, n_pages, bpp).max(axis=(2, 4))
    flags = (flags > 0.5).astype(jnp.int32).reshape(B * G, n_pages)

    pad = ((0, 0), (0, NEW_PAD - S), (0, 0))
    new_sel, new_win = jnp.pad(new_sel, pad), jnp.pad(new_win, pad)
    expand = np.zeros((128, PP * PAGE_SIZE), np.float32)
    expand[np.arange(PP * PAGE_SIZE) // BLOCK, np.arange(PP * PAGE_SIZE)] = 1.0
    e = jnp.asarray(expand, BF16)
    page_specs = [pl.BlockSpec((PAGE_SIZE * KV_ROWS, HEAD_DIM), functools.partial(
        lambda b, s, pt, fl, k: (pt[b, s * PP + k], 0), k=k)) for k in range(PP)]
    row = lambda shape: pl.BlockSpec(shape, lambda b, s, pt, fl: (b,) + (0,) * (len(shape) - 1))
    const = lambda shape: pl.BlockSpec(shape, lambda b, s, pt, fl: (0,) * len(shape))
    grid_spec = pltpu.PrefetchScalarGridSpec(
        num_scalar_prefetch=2, grid=(B, steps),
        in_specs=page_specs + [row((S, G * R * HEAD_DIM)), row((1, G, R * S, HEAD_DIM)),
                               row((1, G, steps + 1, S, 128)), row((lw * KV_ROWS, HEAD_DIM)),
                               row((1, NEW_PAD, W)), row((1, NEW_PAD, W)), row((S, G * 128)),
                               const((G, 3, R * S, TQ)), const((128, PP * PAGE_SIZE))],
        out_specs=row((S, G * R * HEAD_DIM)),
        scratch_shapes=[pltpu.VMEM((G, R * S, HEAD_DIM), BF16), pltpu.VMEM((G, R * S, 1), F32),
                        pltpu.VMEM((G, R * S, HEAD_DIM), F32), pltpu.VMEM((G, R * S, HEAD_DIM), F32)])
    return pl.pallas_call(
        functools.partial(_nsa_sample_kernel, S=S, n_pages=n_pages, lw=lw), grid_spec=grid_spec,
        name="nsa_sample_attn",
        out_shape=jax.ShapeDtypeStruct((B * S, G * R * HEAD_DIM), F32),
        compiler_params=pltpu.CompilerParams(dimension_semantics=("parallel", "arbitrary"),
                                             vmem_limit_bytes=V7X_VMEM_LIMIT_BYTES),
    )(page_table, flags, *([pool_sel] * PP), q, ocmp, selsc, win_buf, new_sel, new_win, gl, sb, e)


CONV_HALO = 32


def _rmsnorm_kernel(x_ref, g_ref, o_ref):
    x = x_ref[...]
    y = x * lax.rsqrt(jnp.mean(x * x, axis=-1, keepdims=True) + EPS) * g_ref[...]
    o_ref[...] = y.astype(o_ref.dtype)


def rmsnorm(x, g, out_dtype, tm=256):
    M, D = x.shape
    tm = _pick(M, tm)
    return pl.pallas_call(
        _rmsnorm_kernel, grid=(M // tm,), name="rmsnorm",
        in_specs=[pl.BlockSpec((tm, D), lambda m: (m, 0)), pl.BlockSpec((1, D), lambda m: (0, 0))],
        out_specs=pl.BlockSpec((tm, D), lambda m: (m, 0)),
        out_shape=jax.ShapeDtypeStruct((M, D), out_dtype),
        compiler_params=pltpu.CompilerParams(dimension_semantics=("parallel",),
                                             vmem_limit_bytes=V7X_VMEM_LIMIT_BYTES),
    )(x, g.reshape(1, D))


CONV_LANES = 512
CONV_ROWS = 64
SUBLANES = 8


def _conv_kernel(glu_ref, buf_ref, w_ref, b_ref, c_ref, st_ref, x_ref, *, tt):
    t = pl.program_id(2)
    lead = CONV_HALO - CONV_BUF

    @pl.when(t == 0)
    def _():
        x_ref[0:lead, :] = jnp.zeros((lead, x_ref.shape[1]), F32)
        x_ref[lead:CONV_HALO, :] = buf_ref[0]
        x_ref[CONV_HALO + tt:, :] = jnp.zeros((SUBLANES, x_ref.shape[1]), F32)

    x_ref[CONV_HALO:CONV_HALO + tt, :] = glu_ref[0]
    rc = min(tt, CONV_ROWS)
    for r0 in range(0, tt, rc):
        acc = jnp.zeros((rc, x_ref.shape[1]), F32) + b_ref[...]
        for phase in range(SUBLANES):
            taps = [k for k in range(CONV_WIDTH) if (lead + k) % SUBLANES == phase]
            part = None
            for k in taps:
                term = x_ref[pl.ds(r0 + lead + k - phase, rc + SUBLANES), :] * w_ref[k:k + 1, :]
                part = term if part is None else part + term
            acc = acc + part[phase:phase + rc, :]
        c_ref[0, r0:r0 + rc, :] = acc

    @pl.when(t == pl.num_programs(2) - 1)
    def _():
        st_ref[0] = x_ref[pl.ds(tt + lead, CONV_BUF), :]

    x_ref[0:CONV_HALO, :] = x_ref[pl.ds(tt, CONV_HALO), :]


def _ln_silu_kernel(c_ref, par_ref, z_ref):
    conv = c_ref[...]
    mu = jnp.mean(conv, axis=-1, keepdims=True)
    c = conv - mu
    var = jnp.mean(c * c, axis=-1, keepdims=True)
    y = c * lax.rsqrt(var + EPS) * par_ref[0:1, :] + par_ref[1:2, :]
    z_ref[...] = (y * jax.nn.sigmoid(y)).astype(z_ref.dtype)


def conv_ln_silu(glu, buf, dw_w, dw_b, ln_g, ln_b, tt=256, tm=256):
    B, S, D = glu.shape
    tt = _pick(S, tt)
    cw = _pick(D, CONV_LANES)
    assert tt % 8 == 0
    conv, state = pl.pallas_call(
        functools.partial(_conv_kernel, tt=tt), grid=(B, D // cw, S // tt), name="dwconv",
        in_specs=[pl.BlockSpec((1, tt, cw), lambda b, c, t: (b, t, c)),
                  pl.BlockSpec((1, CONV_BUF, cw), lambda b, c, t: (b, 0, c)),
                  pl.BlockSpec((CONV_WIDTH, cw), lambda b, c, t: (0, c)),
                  pl.BlockSpec((1, cw), lambda b, c, t: (0, c))],
        out_specs=[pl.BlockSpec((1, tt, cw), lambda b, c, t: (b, t, c)),
                   pl.BlockSpec((1, CONV_BUF, cw), lambda b, c, t: (b, 0, c))],
        out_shape=[jax.ShapeDtypeStruct((B, S, D), F32), jax.ShapeDtypeStruct((B, CONV_BUF, D), F32)],
        scratch_shapes=[pltpu.VMEM((CONV_HALO + tt + SUBLANES, cw), F32)],
        compiler_params=pltpu.CompilerParams(dimension_semantics=("parallel", "parallel", "arbitrary"),
                                             vmem_limit_bytes=V7X_VMEM_LIMIT_BYTES),
    )(glu, buf, dw_w, dw_b.reshape(1, D))
    M = B * S
    tm = _pick(M, tm)
    z = pl.pallas_call(
        _ln_silu_kernel, grid=(M // tm,), name="ln_silu",
        in_specs=[pl.BlockSpec((tm, D), lambda m: (m, 0)), pl.BlockSpec((2, D), lambda m: (0, 0))],
        out_specs=pl.BlockSpec((tm, D), lambda m: (m, 0)),
        out_shape=jax.ShapeDtypeStruct((M, D), BF16),
        compiler_params=pltpu.CompilerParams(dimension_semantics=("parallel",),
                                             vmem_limit_bytes=V7X_VMEM_LIMIT_BYTES),
    )(conv.reshape(M, D), jnp.stack([ln_g, ln_b]))
    return z, state


def rel_bucket(dist):
    n = jnp.maximum(dist, 0)
    max_exact = REL_BUCKETS // 2
    nf = jnp.maximum(n, 1).astype(jnp.float32)
    large = max_exact + (jnp.log(nf / max_exact) / math.log(REL_MAX_DIST / max_exact)
                         * (REL_BUCKETS - max_exact)).astype(jnp.int32)
    large = jnp.minimum(large, REL_BUCKETS - 1)
    return jnp.where(n < max_exact, n, large)


def bucket_bias(rel_table, dist):
    onehot = jax.nn.one_hot(rel_bucket(dist), REL_BUCKETS, dtype=F32)
    return jnp.dot(onehot, rel_table.astype(F32), precision=lax.Precision.HIGHEST)


def head_bias(rel_table, dist):
    b = bucket_bias(rel_table, dist)
    return b.reshape(dist.shape + (KV_GROUPS, HEADS_PER_GROUP)).transpose(2, 3, 0, 1)


def nsa_project(hb, p):
    kv_pairs = [matmul(hb, p["w_kv"][i], tall=True) for i in range(N_BRANCHES)]
    gl = matmul(hb, p["w_gate"])
    return [w for w, _ in kv_pairs], [t for _, t in kv_pairs], gl


def nsa_prompt(hb, q2, B, T, p, rel_table):
    kvs, kvt, gl = nsa_project(hb, p)
    ident = jnp.arange(B * T // PAGE_SIZE, dtype=jnp.int32).reshape(B, T // PAGE_SIZE)
    kc, vc = compress_paged(kvs[0], ident, p, tall=False)
    nb = T // BLOCK
    dist_c = jnp.arange(T)[:, None] - (jnp.arange(nb) * BLOCK + (BLOCK - 1))[None, :]
    cb = head_bias(rel_table, dist_c.T).reshape(KV_GROUPS, HEADS_PER_GROUP, nb, T // TQ, TQ)
    cb = cb.transpose(0, 3, 2, 1, 4).reshape(KV_GROUPS, T // TQ, nb, HEADS_PER_GROUP * TQ)
    o = nsa_prompt_attention(q2, kvs[1], kvs[2], kc, vc, gl, toeplitz_bias(rel_table), cb, B, T)
    kv_cmp, kv_sel, kv_win = [kv.reshape(B, T, 2, KV_GROUPS, HEAD_DIM) for kv in kvt]
    return o, kv_cmp, kv_sel, kv_win[:, T - min(WINDOW, T):]


def nsa_sample(hb, q2, B, S, pool_cmp, pool_sel, win_buf, page_table, p, rel_table):
    W = 2 * KV_WIDTH
    n_pages = page_table.shape[1]
    kvs, kvt, gl = nsa_project(hb, p)
    assert (n_pages * PAGE_SIZE + S) // BLOCK == n_pages * PAGE_SIZE // BLOCK
    kc, vc = compress_paged(pool_cmp.reshape(-1, HEAD_DIM), page_table, p, tall=True)
    lw = win_buf.shape[1]
    o = nsa_sample_attention(q2, kc, vc, pool_sel.reshape(-1, HEAD_DIM), page_table,
                             win_buf.reshape(-1, HEAD_DIM), kvs[1].reshape(B, S, W), kvs[2].reshape(B, S, W),
                             gl, rel_table)
    kv_cmp, kv_sel, kv_win = [kv.reshape(B, S, 2, KV_GROUPS, HEAD_DIM) for kv in kvt]
    all_win = jnp.concatenate([win_buf, kv_win], axis=1)
    return o, kv_cmp, kv_sel, all_win[:, lw + S - min(WINDOW, lw + S):]


def ffn(xp, xs, g, w_up32, w_down32, layer):
    D_ff = w_up32.shape[2]
    hid_p, w_up = matmul_wcast(rmsnorm(xp, g, BF16), w_up32, layer, D_ff, act="relu2", out_dtype=BF16)
    xp, w_down = matmul_wcast_ktiled(hid_p, w_down32, layer, xp)
    hid_s = matmul(rmsnorm(xs, g, BF16), w_up, act="relu2", out_dtype=BF16)
    return xp, matmul(hid_s, w_down, res=xs, tk=2048)


def kernel(x_prompt, x_sample, state_conv, cache_kv_cmp, cache_kv_sel, state_kv_win, page_table, norm_mix_g, norm_ffn_g, norm_final_g, rel_bias_table, conv_w_in, conv_b_in, conv_dw_w, conv_dw_b, conv_ln_g, conv_ln_b, conv_w_out, conv_b_out, nsa_w_in, nsa_cmp_pos, nsa_cmp_w1, nsa_cmp_b1, nsa_cmp_w2, nsa_cmp_b2, nsa_w_out, ffn_w_up, ffn_w_down):
    Bp, T, D = x_prompt.shape
    Bs, S, _ = x_sample.shape
    xp, xs = x_prompt.reshape(Bp * T, D), x_sample.reshape(Bs * S, D)
    qd = N_HEADS * HEAD_DIM
    kvd = 2 * KV_WIDTH

    def conv_tail(glu, B, L, buf):
        return conv_ln_silu(glu.reshape(B, L, D), buf, conv_dw_w[0], conv_dw_b[0], conv_ln_g[0], conv_ln_b[0])

    glu_p, wa, wg = matmul_glu_wcast(rmsnorm(xp, norm_mix_g[0], BF16), conv_w_in, 0, conv_b_in[0])
    zp, conv_p = conv_tail(glu_p, Bp, T, jnp.zeros((Bp, CONV_BUF, D), F32))
    xp, w_out0 = matmul_wcast(zp, conv_w_out, 0, D, bias=conv_b_out[0], res=xp)
    glu_s = matmul_glu(rmsnorm(xs, norm_mix_g[0], BF16), wa, wg, conv_b_in[0][:D], conv_b_in[0][D:])
    zs, conv_s = conv_tail(glu_s, Bs, S, state_conv[0])
    xs = matmul(zs, w_out0, bias=conv_b_out[0], res=xs)
    xp, xs = ffn(xp, xs, norm_ffn_g[0], ffn_w_up, ffn_w_down, 0)

    w_in = nsa_w_in[0]
    nbr = N_BRANCHES * HEADS_PER_GROUP
    gate_w = w_in[:, qd + N_BRANCHES * kvd:].reshape(D, N_BRANCHES, KV_GROUPS, HEADS_PER_GROUP)
    gate_w = gate_w.transpose(0, 2, 1, 3).reshape(D, KV_GROUPS, nbr)
    gate_w = jnp.pad(gate_w, ((0, 0), (0, 0), (0, 128 - nbr))).reshape(D, KV_GROUPS * 128)
    npar = dict(w_kv=[w_in[:, qd + i * kvd: qd + (i + 1) * kvd].astype(BF16) for i in range(N_BRANCHES)],
                w_gate=gate_w.astype(BF16), pos=nsa_cmp_pos[0], w1=nsa_cmp_w1[0].astype(BF16),
                b1=nsa_cmp_b1[0], w2=nsa_cmp_w2[0].astype(BF16), b2=nsa_cmp_b2[0])
    hp = rmsnorm(xp, norm_mix_g[1], BF16)
    q2p, w_q = matmul_wcast(hp, nsa_w_in, 0, qd, out_dtype=BF16)
    op, kcp, ksp, kwp = nsa_prompt(hp, q2p, Bp, T, npar, rel_bias_table)
    xp, w_out1 = matmul_wcast(op, nsa_w_out, 0, D, res=xp)
    hs = rmsnorm(xs, norm_mix_g[1], BF16)
    os_, kcs, kss, kws = nsa_sample(hs, matmul(hs, w_q), Bs, S, cache_kv_cmp[0], cache_kv_sel[0],
                                    state_kv_win[0], page_table, npar, rel_bias_table)
    xs = matmul(os_.astype(BF16), w_out1, res=xs)
    xp, xs = ffn(xp, xs, norm_ffn_g[1], ffn_w_up, ffn_w_down, 1)

    y_prompt = rmsnorm(xp, norm_final_g, F32).reshape(Bp, T, D)
    y_sample = rmsnorm(xs, norm_final_g, F32).reshape(Bs, S, D)
    return (y_prompt, y_sample, conv_p[None], conv_s[None], kcp[None], kcs[None],
            ksp[None], kss[None], kwp[None], kws[None])
```

```python
import functools
import math

import jax
import jax.numpy as jnp
import numpy as np
from jax import lax
from jax.experimental import pallas as pl
from jax.experimental.pallas import tpu as pltpu

D_MODEL = 4096
CONV_WIDTH = 31
CONV_BUF = CONV_WIDTH - 1
N_HEADS = 32
HEAD_DIM = D_MODEL // N_HEADS
KV_GROUPS = 4
HEADS_PER_GROUP = N_HEADS // KV_GROUPS
KV_WIDTH = KV_GROUPS * HEAD_DIM
BLOCK = 64
N_SELECT = 16
WINDOW = 512
WIN_Q_BLOCK = 128
SEL_Q_CHUNK = 16
N_BRANCHES = 3
PAGE_SIZE = 128
REL_BUCKETS = 32
REL_MAX_DIST = 128
EPS = 1e-6
NEG_INF = -1e30

V7X_VMEM_LIMIT_BYTES = 56 * 1024 * 1024
BF16 = jnp.bfloat16
F32 = jnp.float32


def _apply_act(y, act):
    if act == "relu2":
        return jnp.square(jnp.maximum(y, 0.0))
    if act == "silu":
        return y * jax.nn.sigmoid(y)
    assert act is None
    return y


def _mm_kernel(*refs, act, has_bias, has_res, has_tall, nk):
    x_ref, w_ref = refs[0], refs[1]
    pos = 2
    b_ref = r_ref = None
    if has_bias:
        b_ref = refs[pos]
        pos += 1
    if has_res:
        r_ref = refs[pos]
        pos += 1
    o_ref = refs[pos]
    pos += 1
    tall_ref = None
    if has_tall:
        tall_ref = refs[pos]
        pos += 1
    acc_ref = refs[pos] if nk > 1 else None

    def epilogue(y):
        if has_bias:
            y = y + b_ref[...]
        y = _apply_act(y, act)
        if has_res:
            y = y + r_ref[...]
        o_ref[...] = y.astype(o_ref.dtype)
        if has_tall:
            n_slabs = y.shape[1] // HEAD_DIM
            for c in range(n_slabs):
                tall_ref[pl.ds(c, y.shape[0], stride=n_slabs), :] = y[:, c * HEAD_DIM:(c + 1) * HEAD_DIM]

    part = jnp.dot(x_ref[...], w_ref[...], preferred_element_type=F32)
    if nk == 1:
        epilogue(part)
        return
    k = pl.program_id(2)

    @pl.when(k == 0)
    def _():
        acc_ref[...] = part

    @pl.when(jnp.logical_and(k > 0, k < nk - 1))
    def _():
        acc_ref[...] += part

    @pl.when(k == nk - 1)
    def _():
        epilogue(acc_ref[...] + part)


def _pick(dim, pref):
    t = min(dim, pref)
    assert dim % t == 0, (dim, pref)
    return t


def matmul(x, w, bias=None, res=None, act=None, out_dtype=F32, tm=1024, tn=1024, tk=4096, tall=False):
    M, K = x.shape
    K2, N = w.shape
    assert K == K2
    tm, tn, tk = _pick(M, tm), _pick(N, tn), _pick(K, tk)
    nk = K // tk
    in_specs = [pl.BlockSpec((tm, tk), lambda m, n, k: (m, k)),
                pl.BlockSpec((tk, tn), lambda m, n, k: (k, n))]
    args = [x, w]
    if bias is not None:
        in_specs.append(pl.BlockSpec((1, tn), lambda m, n, k: (0, n)))
        args.append(bias.reshape(1, N).astype(F32))
    if res is not None:
        in_specs.append(pl.BlockSpec((tm, tn), lambda m, n, k: (m, n)))
        args.append(res)
    scratch = [pltpu.VMEM((tm, tn), F32)] if nk > 1 else []
    out_specs = pl.BlockSpec((tm, tn), lambda m, n, k: (m, n))
    out_shape = jax.ShapeDtypeStruct((M, N), out_dtype)
    if tall:
        assert tn == N and N % HEAD_DIM == 0
        slabs = N // HEAD_DIM
        out_specs = [out_specs, pl.BlockSpec((tm * slabs, HEAD_DIM), lambda m, n, k: (m, 0))]
        out_shape = [out_shape, jax.ShapeDtypeStruct((M * slabs, HEAD_DIM), F32)]
    return pl.pallas_call(
        functools.partial(_mm_kernel, act=act, has_bias=bias is not None,
                          has_res=res is not None, has_tall=tall, nk=nk),
        grid=(M // tm, N // tn, nk),
        in_specs=in_specs,
        out_specs=out_specs,
        out_shape=out_shape,
        scratch_shapes=scratch,
        compiler_params=pltpu.CompilerParams(
            dimension_semantics=("parallel", "parallel", "arbitrary"),
            vmem_limit_bytes=V7X_VMEM_LIMIT_BYTES),
    )(*args)


def _glu_kernel(x_ref, wa_ref, wg_ref, ba_ref, bg_ref, o_ref):
    x = x_ref[...]
    a = jnp.dot(x, wa_ref[...], preferred_element_type=F32) + ba_ref[...]
    g = jnp.dot(x, wg_ref[...], preferred_element_type=F32) + bg_ref[...]
    o_ref[...] = a * jax.nn.sigmoid(g)


def matmul_glu(x, wa, wg, ba, bg, tm=1024, tn=512):
    M, K = x.shape
    N = wa.shape[1]
    tm, tn = _pick(M, tm), _pick(N, tn)
    wspec = pl.BlockSpec((K, tn), lambda m, n: (0, n))
    bspec = pl.BlockSpec((1, tn), lambda m, n: (0, n))
    return pl.pallas_call(
        _glu_kernel,
        grid=(M // tm, N // tn),
        in_specs=[pl.BlockSpec((tm, K), lambda m, n: (m, 0)), wspec, wspec, bspec, bspec],
        out_specs=pl.BlockSpec((tm, tn), lambda m, n: (m, n)),
        out_shape=jax.ShapeDtypeStruct((M, N), F32),
        compiler_params=pltpu.CompilerParams(
            dimension_semantics=("parallel", "parallel"),
            vmem_limit_bytes=V7X_VMEM_LIMIT_BYTES),
    )(x, wa, wg, ba.reshape(1, N), bg.reshape(1, N))


def _mm_wcast_kernel(*refs, act, has_bias, has_res):
    x_ref, w_ref = refs[0], refs[1]
    pos = 2
    b_ref = r_ref = None
    if has_bias:
        b_ref = refs[pos]
        pos += 1
    if has_res:
        r_ref = refs[pos]
        pos += 1
    o_ref, wbf_ref = refs[pos], refs[pos + 1]

    @pl.when(pl.program_id(1) == 0)
    def _():
        wbf_ref[...] = w_ref[...].astype(BF16)

    y = jnp.dot(x_ref[...], wbf_ref[...], preferred_element_type=F32)
    if has_bias:
        y = y + b_ref[...]
    y = _apply_act(y, act)
    if has_res:
        y = y + r_ref[...]
    o_ref[...] = y.astype(o_ref.dtype)


def matmul_wcast(x, w32, layer, n_cols, bias=None, res=None, act=None, out_dtype=F32, tm=1024, tn=512):
    M, K = x.shape
    assert w32.shape[1] == K and n_cols <= w32.shape[2]
    N = n_cols
    tm, tn = _pick(M, tm), _pick(N, tn)
    in_specs = [pl.BlockSpec((tm, K), lambda n, m: (m, 0)),
                pl.BlockSpec((None, K, tn), lambda n, m: (layer, 0, n))]
    args = [x, w32]
    if bias is not None:
        in_specs.append(pl.BlockSpec((1, tn), lambda n, m: (0, n)))
        args.append(bias.reshape(1, N).astype(F32))
    if res is not None:
        in_specs.append(pl.BlockSpec((tm, tn), lambda n, m: (m, n)))
        args.append(res)
    return pl.pallas_call(
        functools.partial(_mm_wcast_kernel, act=act, has_bias=bias is not None, has_res=res is not None),
        grid=(N // tn, M // tm), name="mm_wcast",
        in_specs=in_specs,
        out_specs=[pl.BlockSpec((tm, tn), lambda n, m: (m, n)), pl.BlockSpec((K, tn), lambda n, m: (0, n))],
        out_shape=[jax.ShapeDtypeStruct((M, N), out_dtype), jax.ShapeDtypeStruct((K, N), BF16)],
        compiler_params=pltpu.CompilerParams(
            dimension_semantics=("parallel", "arbitrary"),
            vmem_limit_bytes=V7X_VMEM_LIMIT_BYTES),
    )(*args)


def _mm_wcast_ktiled_kernel(x_ref, w_ref, r_ref, o_ref, wbf_ref, acc_ref, *, nk, tm):
    k = pl.program_id(1)
    m = pl.program_id(2)

    @pl.when(m == 0)
    def _():
        wbf_ref[...] = w_ref[...].astype(BF16)

    part = jnp.dot(x_ref[...], wbf_ref[...], preferred_element_type=F32)
    rows = pl.ds(pl.multiple_of(m * tm, tm), tm)

    @pl.when(k == 0)
    def _():
        acc_ref[rows, :] = part

    @pl.when(jnp.logical_and(k > 0, k < nk - 1))
    def _():
        acc_ref[rows, :] += part

    @pl.when(k == nk - 1)
    def _():
        o_ref[...] = acc_ref[rows, :] + part + r_ref[...]


def matmul_wcast_ktiled(x, w32, layer, res, tm=1024, tn=512, tk=2048):
    M, K = x.shape
    N = w32.shape[2]
    tm, tn, tk = _pick(M, tm), _pick(N, tn), _pick(K, tk)
    nk = K // tk
    assert nk >= 2
    last_rows = lambda n, k, m: (jnp.where(k == nk - 1, m, 0), n)
    return pl.pallas_call(
        functools.partial(_mm_wcast_ktiled_kernel, nk=nk, tm=tm),
        grid=(N // tn, nk, M // tm), name="mm_wcast_ktiled",
        in_specs=[pl.BlockSpec((tm, tk), lambda n, k, m: (m, k)),
                  pl.BlockSpec((None, tk, tn), lambda n, k, m: (layer, k, n)),
                  pl.BlockSpec((tm, tn), last_rows)],
        out_specs=[pl.BlockSpec((tm, tn), last_rows), pl.BlockSpec((tk, tn), lambda n, k, m: (k, n))],
        out_shape=[jax.ShapeDtypeStruct((M, N), F32), jax.ShapeDtypeStruct((K, N), BF16)],
        scratch_shapes=[pltpu.VMEM((M, tn), F32)],
        compiler_params=pltpu.CompilerParams(
            dimension_semantics=("parallel", "arbitrary", "arbitrary"),
            vmem_limit_bytes=V7X_VMEM_LIMIT_BYTES),
    )(x, w32, res)


def _glu_wcast_kernel(x_ref, wa_ref, wg_ref, ba_ref, bg_ref, o_ref, wabf_ref, wgbf_ref):
    @pl.when(pl.program_id(1) == 0)
    def _():
        wabf_ref[...] = wa_ref[...].astype(BF16)
        wgbf_ref[...] = wg_ref[...].astype(BF16)

    x = x_ref[...]
    a = jnp.dot(x, wabf_ref[...], preferred_element_type=F32) + ba_ref[...]
    g = jnp.dot(x, wgbf_ref[...], preferred_element_type=F32) + bg_ref[...]
    o_ref[...] = a * jax.nn.sigmoid(g)


def matmul_glu_wcast(x, w32, layer, b, tm=1024, tn=256):
    M, K = x.shape
    N = w32.shape[2] // 2
    tm, tn = _pick(M, tm), _pick(N, tn)
    nt = N // tn
    wspec = pl.BlockSpec((K, tn), lambda n, m: (0, n))
    return pl.pallas_call(
        _glu_wcast_kernel, grid=(nt, M // tm), name="mm_glu_wcast",
        in_specs=[pl.BlockSpec((tm, K), lambda n, m: (m, 0)),
                  pl.BlockSpec((None, K, tn), lambda n, m: (layer, 0, n)),
                  pl.BlockSpec((None, K, tn), lambda n, m: (layer, 0, nt + n)),
                  pl.BlockSpec((1, tn), lambda n, m: (0, n)), pl.BlockSpec((1, tn), lambda n, m: (0, nt + n))],
        out_specs=[pl.BlockSpec((tm, tn), lambda n, m: (m, n)), wspec, wspec],
        out_shape=[jax.ShapeDtypeStruct((M, N), F32), jax.ShapeDtypeStruct((K, N), BF16),
                   jax.ShapeDtypeStruct((K, N), BF16)],
        compiler_params=pltpu.CompilerParams(
            dimension_semantics=("parallel", "arbitrary"),
            vmem_limit_bytes=V7X_VMEM_LIMIT_BYTES),
    )(x, w32, w32, b.reshape(1, 2 * N), b.reshape(1, 2 * N))


TQ = 128
SEL_KT = 4 * TQ
IMP_FORCED = 16.0
IMP_FUTURE = -1.0
_NT = (((1,), (1,)), ((), ()))
_TN = (((0,), (0,)), ((), ()))


def _nsa_prompt_kernel(q_ref, ks_ref, vs_ref, kw_ref, vw_ref, kc_ref, vc_ref, gl_ref, toe_ref, cb_ref,
                       e_ref, o_ref, selm_ref, m_ref, l_ref, acc_ref, *, T):
    R = HEADS_PER_GROUP
    nb = T // BLOCK
    nqt = T // TQ
    i = pl.program_id(2)
    scale = HEAD_DIM ** -0.5
    qb = q_ref[...]
    qm = jnp.concatenate([qb[:, r * HEAD_DIM:(r + 1) * HEAD_DIM] for r in range(R)], axis=0)
    t_loc = lax.broadcasted_iota(jnp.int32, (TQ, 1), 0)
    q_pos = i * TQ + t_loc

    kc = kc_ref[0, 0].astype(BF16)
    vc = vc_ref[0, 0].astype(BF16)
    sc = lax.dot_general(kc, qm, _NT, preferred_element_type=F32) * scale + cb_ref[0, 0]
    jb = lax.broadcasted_iota(jnp.int32, (nb, R * TQ), 0)
    q_lane = i * TQ + lax.broadcasted_iota(jnp.int32, (1, R * TQ), 1) % TQ
    valid = (q_lane - (jb * BLOCK + (BLOCK - 1))) >= 0
    sc = jnp.where(valid, sc, NEG_INF)
    ec = jnp.where(valid, jnp.exp(sc - jnp.max(sc, axis=0, keepdims=True)), 0.0)
    lc = jnp.sum(ec, axis=0, keepdims=True)
    pc = ec / jnp.where(lc > 0.0, lc, 1.0)
    o_cmp = lax.dot_general(pc.astype(BF16), vc, _TN, preferred_element_type=F32).reshape(R, TQ, HEAD_DIM)
    imp = pc[:, 0:TQ]
    for r in range(1, R):
        imp = imp + pc[:, r * TQ:(r + 1) * TQ]

    jn = lax.broadcasted_iota(jnp.int32, (nb, TQ), 0)
    cur = (i * TQ + lax.broadcasted_iota(jnp.int32, (1, TQ), 1)) // BLOCK
    forced = (jn == 0) | (jn == cur) | (jn == cur - 1)
    impp = jnp.where(jn > cur, IMP_FUTURE, jnp.where(forced, IMP_FORCED, imp))
    rank = jnp.zeros((nb, TQ), jnp.int32)
    for jp in range(nb):
        row = impp[jp:jp + 1, :]
        beats = (row > impp) | ((row == impp) & (jn > jp))
        rank = rank + beats.astype(jnp.int32)
    sel = (rank < min(N_SELECT, nb)).astype(BF16)
    selm = lax.dot_general(sel, e_ref[...], _TN, preferred_element_type=F32)
    for st in range(T // SEL_KT):
        selm_ref[st] = selm[:, st * SEL_KT:(st + 1) * SEL_KT]

    def with_ones(v):
        return jnp.concatenate([v.astype(BF16), jnp.ones(v.shape, BF16)], axis=1)

    def scores(k_t, kt0, n_sub):
        s = lax.dot_general(qm, k_t, _NT, preferred_element_type=F32) * scale
        bias = jnp.concatenate([toe_ref[0, jnp.clip(i - kt0 - c, 0, 2)] for c in range(n_sub)], axis=1)
        j_loc = lax.broadcasted_iota(jnp.int32, (1, n_sub * TQ), 1)
        dist = (i - kt0) * TQ + t_loc - j_loc
        return (s + bias).reshape(R, TQ, n_sub * TQ), dist

    m_ref[...] = jnp.full(m_ref.shape, NEG_INF, F32)
    l_ref[...] = jnp.zeros(l_ref.shape, F32)
    acc_ref[...] = jnp.zeros(acc_ref.shape, F32)
    sub = SEL_KT // TQ

    def sel_body(st, carry):
        ks = pl.multiple_of(st * SEL_KT, SEL_KT)
        s, dist = scores(ks_ref[pl.ds(ks, SEL_KT), :].astype(BF16), st * sub, sub)
        mask = (dist >= 0) & (selm_ref[st] > 0.5)
        s = jnp.where(mask[None], s, NEG_INF)
        m_old = m_ref[...]
        m_new = jnp.maximum(m_old, jnp.max(s, axis=-1, keepdims=True))
        p = jnp.exp(s - m_new)
        alpha = jnp.exp(m_old - m_new)
        pv = jnp.dot(p.reshape(R * TQ, SEL_KT).astype(BF16), with_ones(vs_ref[pl.ds(ks, SEL_KT), :]),
                     preferred_element_type=F32)
        acc_ref[...] = alpha * acc_ref[...] + pv[:, :HEAD_DIM].reshape(R, TQ, HEAD_DIM)
        l_ref[...] = alpha * l_ref[...] + pv[:, HEAD_DIM:].reshape(R, TQ, HEAD_DIM)
        m_ref[...] = m_new
        return carry

    lax.fori_loop(0, i // sub + 1, sel_body, 0)
    o_sel = acc_ref[...] / l_ref[...]

    n_win = WINDOW // TQ + 1
    kt0 = jnp.clip(i - WINDOW // TQ, 0, nqt - n_win)
    ws = pl.multiple_of(kt0 * TQ, TQ)
    s, dist = scores(kw_ref[pl.ds(ws, n_win * TQ), :].astype(BF16), kt0, n_win)
    s = jnp.where(((dist >= 0) & (dist <= WINDOW))[None], s, NEG_INF)
    p = jnp.exp(s - jnp.max(s, axis=-1, keepdims=True))
    pv = jnp.dot(p.reshape(R * TQ, n_win * TQ).astype(BF16), with_ones(vw_ref[pl.ds(ws, n_win * TQ), :]),
                 preferred_element_type=F32)
    o_win = (pv[:, :HEAD_DIM] / pv[:, HEAD_DIM:]).reshape(R, TQ, HEAD_DIM)

    gates = jax.nn.sigmoid(gl_ref[...])
    for r in range(R):
        o = (gates[:, r:r + 1] * o_cmp[r] + gates[:, R + r:R + r + 1] * o_sel[r]
             + gates[:, 2 * R + r:2 * R + r + 1] * o_win[r])
        o_ref[:, r * HEAD_DIM:(r + 1) * HEAD_DIM] = o.astype(o_ref.dtype)


def nsa_prompt_attention(q2, kv_sel, kv_win, kc, vc, gl, toe, cb, B, T):
    G, R = KV_GROUPS, HEADS_PER_GROUP
    nb, nqt = T // BLOCK, T // TQ
    assert T % SEL_KT == 0 and nqt >= WINDOW // TQ + 1
    expand = (np.arange(T)[None, :] // BLOCK == np.arange(nb)[:, None])
    e = jnp.asarray(expand, BF16)
    qspec = pl.BlockSpec((TQ, R * HEAD_DIM), lambda b, g, i: (b * nqt + i, g))
    kspec = pl.BlockSpec((T, HEAD_DIM), lambda b, g, i: (b, g))
    vspec = pl.BlockSpec((T, HEAD_DIM), lambda b, g, i: (b, G + g))
    cspec = pl.BlockSpec((1, 1, nb, HEAD_DIM), lambda b, g, i: (b, g, 0, 0))
    return pl.pallas_call(
        functools.partial(_nsa_prompt_kernel, T=T), name="nsa_prompt_attn",
        grid=(B, G, nqt),
        in_specs=[qspec, kspec, vspec, kspec, vspec, cspec, cspec,
                  pl.BlockSpec((TQ, 128), lambda b, g, i: (b * nqt + i, g)),
                  pl.BlockSpec((1, 3, R * TQ, TQ), lambda b, g, i: (g, 0, 0, 0)),
                  pl.BlockSpec((1, 1, nb, R * TQ), lambda b, g, i: (g, i, 0, 0)),
                  pl.BlockSpec((nb, T), lambda b, g, i: (0, 0))],
        out_specs=qspec,
        out_shape=jax.ShapeDtypeStruct((B * T, G * R * HEAD_DIM), BF16),
        scratch_shapes=[pltpu.VMEM((T // SEL_KT, TQ, SEL_KT), F32), pltpu.VMEM((R, TQ, 1), F32),
                        pltpu.VMEM((R, TQ, HEAD_DIM), F32), pltpu.VMEM((R, TQ, HEAD_DIM), F32)],
        compiler_params=pltpu.CompilerParams(
            dimension_semantics=("parallel", "parallel", "arbitrary"),
            vmem_limit_bytes=V7X_VMEM_LIMIT_BYTES),
    )(q2, kv_sel, kv_sel, kv_win, kv_win, kc, vc, gl, toe, cb, e)


def toeplitz_bias(rel_table):
    d = jnp.arange(3)[:, None, None] * TQ + jnp.arange(TQ)[None, :, None] - jnp.arange(TQ)[None, None, :]
    b = bucket_bias(rel_table, d)
    b = b.reshape(3, TQ, TQ, KV_GROUPS, HEADS_PER_GROUP).transpose(3, 0, 4, 1, 2)
    return b.reshape(KV_GROUPS, 3, HEADS_PER_GROUP * TQ, TQ)


NEW_PAD = TQ
PAGES_PER_STEP = 4
SEL_DONE = -3.0
SEL_PAD = -2.0


BLOCKIFY_PAGES = 8
KV_ROWS = 2 * KV_GROUPS


def _blockify_tall_kernel(pt_ref, *refs):
    del pt_ref
    NP = BLOCKIFY_PAGES
    pages, pos8_ref, perm_ref, o_ref = refs[:NP], refs[NP], refs[NP + 1], refs[NP + 2]
    bpp = PAGE_SIZE // BLOCK
    nblk = NP * bpp
    perm = perm_ref[...]
    for j in range(BLOCK):
        pos8 = pos8_ref[j]
        tiles = [pages[n // bpp][pl.ds(((n % bpp) * BLOCK + j) * KV_ROWS, KV_ROWS), :] + pos8 for n in range(nblk)]
        a = jnp.concatenate(tiles, axis=0).astype(BF16)
        y = jnp.dot(perm, a, preferred_element_type=F32)
        for c in range(KV_ROWS):
            o_ref[c // KV_GROUPS, 0, c % KV_GROUPS, :, j * HEAD_DIM:(j + 1) * HEAD_DIM] = (
                y[c * nblk:(c + 1) * nblk, :].astype(o_ref.dtype))


def _blockify_wide_kernel(pt_ref, *refs):
    del pt_ref
    NP = BLOCKIFY_PAGES
    pages, pos_ref, o_ref, xs_ref = refs[:NP], refs[NP], refs[NP + 1], refs[NP + 2]
    rows = pages[0].shape[0]
    for pi, pg in enumerate(pages):
        for c in range(KV_ROWS):
            xs_ref[c, pi * rows:(pi + 1) * rows, :] = pg[:, c * HEAD_DIM:(c + 1) * HEAD_DIM]
    nblk = NP * PAGE_SIZE // BLOCK
    for kv in range(2):
        for g in range(KV_GROUPS):
            c = kv * KV_GROUPS + g
            for j in range(BLOCK):
                v = xs_ref[c, pl.ds(j, nblk, stride=BLOCK), :]
                o_ref[kv, 0, g, :, j * HEAD_DIM:(j + 1) * HEAD_DIM] = (v + pos_ref[kv, j:j + 1, :]).astype(o_ref.dtype)


def blockify(pool, page_table, pos, tall):
    B, n_pages = page_table.shape
    NP = BLOCKIFY_PAGES
    assert n_pages % NP == 0
    blk = (PAGE_SIZE * KV_ROWS, HEAD_DIM) if tall else (PAGE_SIZE, KV_ROWS * HEAD_DIM)
    page_specs = [pl.BlockSpec(blk, functools.partial(lambda b, s, pt, k: (pt[b, s * NP + k], 0), k=k))
                  for k in range(NP)]
    nbs = NP * PAGE_SIZE // BLOCK
    const = lambda shape: pl.BlockSpec(shape, lambda b, s, pt: (0,) * len(shape))
    if tall:
        assert nbs * KV_ROWS == 128
        pos8 = jnp.repeat(pos.transpose(1, 0, 2), KV_GROUPS, axis=1)
        src = np.arange(nbs * KV_ROWS)
        perm = np.zeros((nbs * KV_ROWS, nbs * KV_ROWS), np.float32)
        perm[(src % KV_ROWS) * nbs + src // KV_ROWS, src] = 1.0
        extra, extra_specs, scratch = [pos8, jnp.asarray(perm, BF16)], [const(pos8.shape), const(perm.shape)], []
        body = _blockify_tall_kernel
    else:
        extra, extra_specs = [pos], [const(pos.shape)]
        scratch = [pltpu.VMEM((KV_ROWS, NP * PAGE_SIZE, HEAD_DIM), F32)]
        body = _blockify_wide_kernel
    grid_spec = pltpu.PrefetchScalarGridSpec(
        num_scalar_prefetch=1, grid=(B, n_pages // NP),
        in_specs=page_specs + extra_specs,
        out_specs=pl.BlockSpec((2, 1, KV_GROUPS, nbs, BLOCK * HEAD_DIM), lambda b, s, pt: (0, b, 0, s, 0)),
        scratch_shapes=scratch)
    return pl.pallas_call(
        body, grid_spec=grid_spec, name="blockify",
        out_shape=jax.ShapeDtypeStruct((2, B, KV_GROUPS, n_pages * PAGE_SIZE // BLOCK, BLOCK * HEAD_DIM), BF16),
        compiler_params=pltpu.CompilerParams(dimension_semantics=("parallel", "arbitrary"),
                                             vmem_limit_bytes=V7X_VMEM_LIMIT_BYTES),
    )(page_table, *([pool] * NP), *extra)


def _compress_mlp_kernel(a_ref, w1_ref, b1_ref, w2_ref, b2_ref, o_ref):
    h = jnp.dot(a_ref[0], w1_ref[0], preferred_element_type=F32) + b1_ref[0]
    h = h * jax.nn.sigmoid(h)
    o_ref[0] = jnp.dot(h.astype(BF16), w2_ref[0], preferred_element_type=F32) + b2_ref[0]


def compress_mlp(a, w1, b1, w2, b2, tm=512):
    _, M, K = a.shape
    H = w1.shape[-1]
    tm = _pick(M, tm)
    return pl.pallas_call(
        _compress_mlp_kernel, grid=(2, M // tm), name="compress_mlp",
        in_specs=[pl.BlockSpec((1, tm, K), lambda k, m: (k, m, 0)),
                  pl.BlockSpec((1, K, H), lambda k, m: (k, 0, 0)),
                  pl.BlockSpec((1, 1, H), lambda k, m: (k, 0, 0)),
                  pl.BlockSpec((1, H, HEAD_DIM), lambda k, m: (k, 0, 0)),
                  pl.BlockSpec((1, 1, HEAD_DIM), lambda k, m: (k, 0, 0))],
        out_specs=pl.BlockSpec((1, tm, HEAD_DIM), lambda k, m: (k, m, 0)),
        out_shape=jax.ShapeDtypeStruct((2, M, HEAD_DIM), F32),
        compiler_params=pltpu.CompilerParams(dimension_semantics=("parallel", "parallel"),
                                             vmem_limit_bytes=V7X_VMEM_LIMIT_BYTES),
    )(a, w1, b1.reshape(2, 1, H), w2, b2.reshape(2, 1, HEAD_DIM))


def compress_paged(pool, page_table, p, tall):
    a = blockify(pool, page_table, p["pos"], tall)
    _, B, G, nb = a.shape[:4]
    out = compress_mlp(a.reshape(2, B * G * nb, BLOCK * HEAD_DIM), p["w1"], p["b1"], p["w2"], p["b2"])
    out = out.reshape(2, B, G, nb, HEAD_DIM)
    return out[0], out[1]


def _expand_rows(x8):
    return jnp.concatenate([x8] * HEADS_PER_GROUP, axis=0)


def _head_rows(q, g):
    R = HEADS_PER_GROUP
    return jnp.concatenate([q[:, (g * R + r) * HEAD_DIM:(g * R + r + 1) * HEAD_DIM] for r in range(R)], axis=0)


def _nsa_sample_select_kernel(q_ref, kc_ref, vc_ref, cb_ref, ocmp_ref, selsc_ref, sel_ref, *, S, n_pages):
    G, R = KV_GROUPS, HEADS_PER_GROUP
    RS = R * S
    nbc = kc_ref.shape[2]
    nbp = sel_ref.shape[3]
    n_blocks = -(-(n_pages * PAGE_SIZE + S) // BLOCK)
    past_len = n_pages * PAGE_SIZE
    scale = HEAD_DIM ** -0.5
    qi = lax.broadcasted_iota(jnp.int32, (RS, 1), 0) % S
    q = q_ref[...]
    jn = lax.broadcasted_iota(jnp.int32, (S, nbp), 1)
    jf = jn.astype(F32)
    cur = (past_len + lax.broadcasted_iota(jnp.int32, (S, 1), 0)) // BLOCK
    forced = (jn == 0) | (jn == cur) | (jn == cur - 1)
    jc = lax.broadcasted_iota(jnp.int32, (RS, nbc), 1)
    valid = (past_len + qi - (jc * BLOCK + (BLOCK - 1))) >= 0
    for g in range(G):
        qg = _head_rows(q, g).astype(BF16)
        sc = lax.dot_general(qg, kc_ref[0, g].astype(BF16), _NT, preferred_element_type=F32) * scale
        sc = jnp.where(valid, sc + cb_ref[g], NEG_INF)
        ec = jnp.where(valid, jnp.exp(sc - jnp.max(sc, axis=-1, keepdims=True)), 0.0)
        lc = jnp.sum(ec, axis=-1, keepdims=True)
        pc = ec / jnp.where(lc > 0.0, lc, 1.0)
        ocmp_ref[0, g] = jnp.dot(pc.astype(BF16), vc_ref[0, g].astype(BF16), preferred_element_type=F32)
        imp = pc[0:S]
        for r in range(1, R):
            imp = imp + pc[r * S:(r + 1) * S]
        imp = jnp.concatenate([imp, jnp.zeros((S, nbp - nbc), F32)], axis=1)
        work = jnp.where(jn > cur, IMP_FUTURE, jnp.where(forced, IMP_FORCED, imp))
        work = jnp.where(jn >= n_blocks, SEL_PAD, work)
        sel = jnp.zeros((S, nbp), F32)
        for _ in range(min(N_SELECT, n_blocks)):
            mx = jnp.max(work, axis=-1, keepdims=True)
            first = jnp.min(jnp.where(work == mx, jf, float(nbp)), axis=-1, keepdims=True)
            hit = jf == first
            sel = jnp.where(hit, 1.0, sel)
            work = jnp.where(hit, SEL_DONE, work)
        sel_ref[0, g] = sel
        for st in range(nbp // 8):
            selsc_ref[0, g, st] = jnp.concatenate(
                [sel[:, st * 8:(st + 1) * 8], jnp.zeros((S, 128 - 8), F32)], axis=1)


def _nsa_sample_kernel(pt_ref, fl_ref, *refs, S, n_pages, lw):
    del pt_ref
    PP = PAGES_PER_STEP
    pages = refs[:PP]
    (q_ref, ocmp_ref, selsc_ref, winb_ref, nsel_ref, nwin_ref, gl_ref, sb_ref, e_ref,
     o_ref, qm_ref, m_ref, l_ref, acc_ref) = refs[PP:]
    G, R = KV_GROUPS, HEADS_PER_GROUP
    RS = R * S
    steps = n_pages // PP
    b = pl.program_id(0)
    s_id = pl.program_id(1)
    scale = HEAD_DIM ** -0.5
    qi = lax.broadcasted_iota(jnp.int32, (RS, 1), 0) % S
    jl = lax.broadcasted_iota(jnp.int32, (1, TQ), 1)

    def attend(g, k_t, v_t, bias, mask):
        s = lax.dot_general(qm_ref[g], k_t.astype(BF16), _NT, preferred_element_type=F32) * scale + bias
        s = jnp.where(mask, s, NEG_INF)
        m_old = m_ref[g]
        m_new = jnp.maximum(m_old, jnp.max(s, axis=-1, keepdims=True))
        p = jnp.where(mask, jnp.exp(s - m_new), 0.0)
        alpha = jnp.exp(m_old - m_new)
        v_aug = jnp.concatenate([v_t.astype(BF16), jnp.ones(v_t.shape, BF16)], axis=1)
        pv = jnp.dot(p.astype(BF16), v_aug, preferred_element_type=F32)
        l_ref[g] = alpha * l_ref[g] + pv[:, HEAD_DIM:]
        acc_ref[g] = alpha * acc_ref[g] + pv[:, :HEAD_DIM]
        m_ref[g] = m_new

    def reset_state():
        m_ref[...] = jnp.full(m_ref.shape, NEG_INF, F32)
        l_ref[...] = jnp.zeros(l_ref.shape, F32)
        acc_ref[...] = jnp.zeros(acc_ref.shape, F32)

    @pl.when(s_id == 0)
    def _():
        reset_state()
        q = q_ref[...]
        for g in range(G):
            qm_ref[g] = _head_rows(q, g).astype(BF16)

    flags = [fl_ref[b * G + g, s_id * PP + pi] for g in range(G) for pi in range(PP)]

    @pl.when(sum(flags) > 0)
    def _():
        bias_tbl = [jnp.minimum(n_pages - (s_id * PP + pi), 2) for pi in range(PP)]
        for g in range(G):
            selm = jnp.dot(selsc_ref[0, g, s_id].astype(BF16), e_ref[...], preferred_element_type=F32)
            k_t = jnp.concatenate([pg[pl.ds(g, PAGE_SIZE, stride=KV_ROWS), :] for pg in pages], axis=0)
            v_t = jnp.concatenate([pg[pl.ds(G + g, PAGE_SIZE, stride=KV_ROWS), :] for pg in pages], axis=0)
            bias = jnp.concatenate([sb_ref[g, t] for t in bias_tbl], axis=1)
            attend(g, k_t, v_t, bias, _expand_rows(selm) > 0.5)

    @pl.when(s_id == steps - 1)
    def _():
        dist_new = qi - jl
        o_sel, o_win = [], []
        for g in range(G):
            selm = jnp.dot(selsc_ref[0, g, steps].astype(BF16), e_ref[...], preferred_element_type=F32)
            mask = (_expand_rows(selm[:, :TQ]) > 0.5) & (dist_new >= 0)
            attend(g, nsel_ref[0, :, g * HEAD_DIM:(g + 1) * HEAD_DIM],
                   nsel_ref[0, :, (G + g) * HEAD_DIM:(G + g + 1) * HEAD_DIM], sb_ref[g, 0], mask)
            o_sel.append(acc_ref[g] / l_ref[g])
        reset_state()
        for g in range(G):
            for t in range(lw // TQ):
                base = lw - t * TQ
                dist = base + qi - jl
                attend(g, winb_ref[pl.ds(t * TQ * KV_ROWS + g, TQ, stride=KV_ROWS), :],
                       winb_ref[pl.ds(t * TQ * KV_ROWS + G + g, TQ, stride=KV_ROWS), :],
                       sb_ref[g, min(base // TQ, 2)], (dist >= 0) & (dist <= WINDOW))
            attend(g, nwin_ref[0, :, g * HEAD_DIM:(g + 1) * HEAD_DIM],
                   nwin_ref[0, :, (G + g) * HEAD_DIM:(G + g + 1) * HEAD_DIM], sb_ref[g, 0],
                   (dist_new >= 0) & (dist_new <= WINDOW))
            o_win.append(acc_ref[g] / l_ref[g])
        gates = jax.nn.sigmoid(gl_ref[...])
        for g in range(G):
            oc = ocmp_ref[0, g]
            for r in range(R):
                c = g * 128 + r
                rows = slice(r * S, (r + 1) * S)
                o = (gates[:, c:c + 1] * oc[rows] + gates[:, c + R:c + R + 1] * o_sel[g][rows]
                     + gates[:, c + 2 * R:c + 2 * R + 1] * o_win[g][rows])
                o_ref[:, (g * R + r) * HEAD_DIM:(g * R + r + 1) * HEAD_DIM] = o


def sample_bias_tables(rel_table, S, past_len, nbc):
    G, R = KV_GROUPS, HEADS_PER_GROUP
    d = jnp.arange(3)[:, None, None] * TQ + jnp.arange(S)[None, :, None] - jnp.arange(TQ)[None, None, :]
    b = bucket_bias(rel_table, d).reshape(3, S, TQ, G, R)
    sb = b.transpose(3, 0, 4, 1, 2).reshape(G, 3, R * S, TQ)
    dc = past_len + jnp.arange(S)[:, None] - (jnp.arange(nbc) * BLOCK + (BLOCK - 1))[None, :]
    cb = head_bias(rel_table, dc).reshape(G, R * S, nbc)
    return sb, cb


def nsa_sample_attention(q, kc, vc, pool_sel, page_table, win_buf, new_sel, new_win, gl, rel_table):
    G, R = KV_GROUPS, HEADS_PER_GROUP
    B, n_pages = page_table.shape
    S = new_sel.shape[1]
    lw = win_buf.shape[0] // (B * KV_ROWS)
    W = 2 * KV_WIDTH
    nbc = kc.shape[2]
    PP = PAGES_PER_STEP
    steps = n_pages // PP
    assert n_pages % PP == 0 and lw % TQ == 0 and S <= BLOCK and PP * PAGE_SIZE // BLOCK == 8
    sb, cb = sample_bias_tables(rel_table, S, n_pages * PAGE_SIZE, nbc)
    nbp = (steps + 1) * 8

    brow = lambda shape: pl.BlockSpec(shape, lambda b: (b,) + (0,) * (len(shape) - 1))
    ocmp, selsc, sel = pl.pallas_call(
        functools.partial(_nsa_sample_select_kernel, S=S, n_pages=n_pages), grid=(B,), name="nsa_sample_select",
        in_specs=[brow((S, G * R * HEAD_DIM)), brow((1, G, nbc, HEAD_DIM)), brow((1, G, nbc, HEAD_DIM)),
                  pl.BlockSpec((G, R * S, nbc), lambda b: (0, 0, 0))],
        out_specs=[brow((1, G, R * S, HEAD_DIM)), brow((1, G, steps + 1, S, 128)), brow((1, G, S, nbp))],
        out_shape=[jax.ShapeDtypeStruct((B, G, R * S, HEAD_DIM), F32),
                   jax.ShapeDtypeStruct((B, G, steps + 1, S, 128), F32),
                   jax.ShapeDtypeStruct((B, G, S, nbp), F32)],
        compiler_params=pltpu.CompilerParams(dimension_semantics=("parallel",),
                                             vmem_limit_bytes=V7X_VMEM_LIMIT_BYTES),
    )(q, kc, vc, cb)
    bpp = PAGE_SIZE // BLOCK
    flags = sel[..., :n_pages * bpp].reshape(B, G, S, n_pages, bpp).max(axis=(2, 4))
    flags = (flags > 0.5).astype(jnp.int32).reshape(B * G, n_pages)

    pad = ((0, 0), (0, NEW_PAD - S), (0, 0))
    new_sel, new_win = jnp.pad(new_sel, pad), jnp.pad(new_win, pad)
    expand = np.zeros((128, PP * PAGE_SIZE), np.float32)
    expand[np.arange(PP * PAGE_SIZE) // BLOCK, np.arange(PP * PAGE_SIZE)] = 1.0
    e = jnp.asarray(expand, BF16)
    page_specs = [pl.BlockSpec((PAGE_SIZE * KV_ROWS, HEAD_DIM), functools.partial(
        lambda b, s, pt, fl, k: (pt[b, s * PP + k], 0), k=k)) for k in range(PP)]
    row = lambda shape: pl.BlockSpec(shape, lambda b, s, pt, fl: (b,) + (0,) * (len(shape) - 1))
    const = lambda shape: pl.BlockSpec(shape, lambda b, s, pt, fl: (0,) * len(shape))
    grid_spec = pltpu.PrefetchScalarGridSpec(
        num_scalar_prefetch=2, grid=(B, steps),
        in_specs=page_specs + [row((S, G * R * HEAD_DIM)), row((1, G, R * S, HEAD_DIM)),
                               row((1, G, steps + 1, S, 128)), row((lw * KV_ROWS, HEAD_DIM)),
                               row((1, NEW_PAD, W)), row((1, NEW_PAD, W)), row((S, G * 128)),
                               const((G, 3, R * S, TQ)), const((128, PP * PAGE_SIZE))],
        out_specs=row((S, G * R * HEAD_DIM)),
        scratch_shapes=[pltpu.VMEM((G, R * S, HEAD_DIM), BF16), pltpu.VMEM((G, R * S, 1), F32),
                        pltpu.VMEM((G, R * S, HEAD_DIM), F32), pltpu.VMEM((G, R * S, HEAD_DIM), F32)])
    return pl.pallas_call(
        functools.partial(_nsa_sample_kernel, S=S, n_pages=n_pages, lw=lw), grid_spec=grid_spec,
        name="nsa_sample_attn",
        out_shape=jax.ShapeDtypeStruct((B * S, G * R * HEAD_DIM), F32),
        compiler_params=pltpu.CompilerParams(dimension_semantics=("parallel", "arbitrary"),
                                             vmem_limit_bytes=V7X_VMEM_LIMIT_BYTES),
    )(page_table, flags, *([pool_sel] * PP), q, ocmp, selsc, win_buf, new_sel, new_win, gl, sb, e)


CONV_HALO = 32


def _rmsnorm_kernel(x_ref, g_ref, o_ref):
    x = x_ref[...]
    y = x * lax.rsqrt(jnp.mean(x * x, axis=-1, keepdims=True) + EPS) * g_ref[...]
    o_ref[...] = y.astype(o_ref.dtype)


def rmsnorm(x, g, out_dtype, tm=256):
    M, D = x.shape
    tm = _pick(M, tm)
    return pl.pallas_call(
        _rmsnorm_kernel, grid=(M // tm,), name="rmsnorm",
        in_specs=[pl.BlockSpec((tm, D), lambda m: (m, 0)), pl.BlockSpec((1, D), lambda m: (0, 0))],
        out_specs=pl.BlockSpec((tm, D), lambda m: (m, 0)),
        out_shape=jax.ShapeDtypeStruct((M, D), out_dtype),
        compiler_params=pltpu.CompilerParams(dimension_semantics=("parallel",),
                                             vmem_limit_bytes=V7X_VMEM_LIMIT_BYTES),
    )(x, g.reshape(1, D))


CONV_LANES = 512
CONV_ROWS = 64
SUBLANES = 8


def _conv_kernel(glu_ref, buf_ref, w_ref, b_ref, c_ref, st_ref, x_ref, *, tt):
    t = pl.program_id(2)
    lead = CONV_HALO - CONV_BUF

    @pl.when(t == 0)
    def _():
        x_ref[0:lead, :] = jnp.zeros((lead, x_ref.shape[1]), F32)
        x_ref[lead:CONV_HALO, :] = buf_ref[0]
        x_ref[CONV_HALO + tt:, :] = jnp.zeros((SUBLANES, x_ref.shape[1]), F32)

    x_ref[CONV_HALO:CONV_HALO + tt, :] = glu_ref[0]
    rc = min(tt, CONV_ROWS)
    for r0 in range(0, tt, rc):
        acc = jnp.zeros((rc, x_ref.shape[1]), F32) + b_ref[...]
        for phase in range(SUBLANES):
            taps = [k for k in range(CONV_WIDTH) if (lead + k) % SUBLANES == phase]
            part = None
            for k in taps:
                term = x_ref[pl.ds(r0 + lead + k - phase, rc + SUBLANES), :] * w_ref[k:k + 1, :]
                part = term if part is None else part + term
            acc = acc + part[phase:phase + rc, :]
        c_ref[0, r0:r0 + rc, :] = acc

    @pl.when(t == pl.num_programs(2) - 1)
    def _():
        st_ref[0] = x_ref[pl.ds(tt + lead, CONV_BUF), :]

    x_ref[0:CONV_HALO, :] = x_ref[pl.ds(tt, CONV_HALO), :]


def _ln_silu_kernel(c_ref, par_ref, z_ref):
    conv = c_ref[...]
    mu = jnp.mean(conv, axis=-1, keepdims=True)
    c = conv - mu
    var = jnp.mean(c * c, axis=-1, keepdims=True)
    y = c * lax.rsqrt(var + EPS) * par_ref[0:1, :] + par_ref[1:2, :]
    z_ref[...] = (y * jax.nn.sigmoid(y)).astype(z_ref.dtype)


def conv_ln_silu(glu, buf, dw_w, dw_b, ln_g, ln_b, tt=256, tm=256):
    B, S, D = glu.shape
    tt = _pick(S, tt)
    cw = _pick(D, CONV_LANES)
    assert tt % 8 == 0
    conv, state = pl.pallas_call(
        functools.partial(_conv_kernel, tt=tt), grid=(B, D // cw, S // tt), name="dwconv",
        in_specs=[pl.BlockSpec((1, tt, cw), lambda b, c, t: (b, t, c)),
                  pl.BlockSpec((1, CONV_BUF, cw), lambda b, c, t: (b, 0, c)),
                  pl.BlockSpec((CONV_WIDTH, cw), lambda b, c, t: (0, c)),
                  pl.BlockSpec((1, cw), lambda b, c, t: (0, c))],
        out_specs=[pl.BlockSpec((1, tt, cw), lambda b, c, t: (b, t, c)),
                   pl.BlockSpec((1, CONV_BUF, cw), lambda b, c, t: (b, 0, c))],
        out_shape=[jax.ShapeDtypeStruct((B, S, D), F32), jax.ShapeDtypeStruct((B, CONV_BUF, D), F32)],
        scratch_shapes=[pltpu.VMEM((CONV_HALO + tt + SUBLANES, cw), F32)],
        compiler_params=pltpu.CompilerParams(dimension_semantics=("parallel", "parallel", "arbitrary"),
                                             vmem_limit_bytes=V7X_VMEM_LIMIT_BYTES),
    )(glu, buf, dw_w, dw_b.reshape(1, D))
    M = B * S
    tm = _pick(M, tm)
    z = pl.pallas_call(
        _ln_silu_kernel, grid=(M // tm,), name="ln_silu",
        in_specs=[pl.BlockSpec((tm, D), lambda m: (m, 0)), pl.BlockSpec((2, D), lambda m: (0, 0))],
        out_specs=pl.BlockSpec((tm, D), lambda m: (m, 0)),
        out_shape=jax.ShapeDtypeStruct((M, D), BF16),
        compiler_params=pltpu.CompilerParams(dimension_semantics=("parallel",),
                                             vmem_limit_bytes=V7X_VMEM_LIMIT_BYTES),
    )(conv.reshape(M, D), jnp.stack([ln_g, ln_b]))
    return z, state


def rel_bucket(dist):
    n = jnp.maximum(dist, 0)
    max_exact = REL_BUCKETS // 2
    nf = jnp.maximum(n, 1).astype(jnp.float32)
    large = max_exact + (jnp.log(nf / max_exact) / math.log(REL_MAX_DIST / max_exact)
                         * (REL_BUCKETS - max_exact)).astype(jnp.int32)
    large = jnp.minimum(large, REL_BUCKETS - 1)
    return jnp.where(n < max_exact, n, large)


def bucket_bias(rel_table, dist):
    onehot = jax.nn.one_hot(rel_bucket(dist), REL_BUCKETS, dtype=F32)
    return jnp.dot(onehot, rel_table.astype(F32), precision=lax.Precision.HIGHEST)


def head_bias(rel_table, dist):
    b = bucket_bias(rel_table, dist)
    return b.reshape(dist.shape + (KV_GROUPS, HEADS_PER_GROUP)).transpose(2, 3, 0, 1)


def nsa_project(hb, p):
    kv_pairs = [matmul(hb, p["w_kv"][i], tall=True) for i in range(N_BRANCHES)]
    gl = matmul(hb, p["w_gate"])
    return [w for w, _ in kv_pairs], [t for _, t in kv_pairs], gl


def nsa_prompt(hb, q2, B, T, p, rel_table):
    kvs, kvt, gl = nsa_project(hb, p)
    ident = jnp.arange(B * T // PAGE_SIZE, dtype=jnp.int32).reshape(B, T // PAGE_SIZE)
    kc, vc = compress_paged(kvs[0], ident, p, tall=False)
    nb = T // BLOCK
    dist_c = jnp.arange(T)[:, None] - (jnp.arange(nb) * BLOCK + (BLOCK - 1))[None, :]
    cb = head_bias(rel_table, dist_c.T).reshape(KV_GROUPS, HEADS_PER_GROUP, nb, T // TQ, TQ)
    cb = cb.transpose(0, 3, 2, 1, 4).reshape(KV_GROUPS, T // TQ, nb, HEADS_PER_GROUP * TQ)
    o = nsa_prompt_attention(q2, kvs[1], kvs[2], kc, vc, gl, toeplitz_bias(rel_table), cb, B, T)
    kv_cmp, kv_sel, kv_win = [kv.reshape(B, T, 2, KV_GROUPS, HEAD_DIM) for kv in kvt]
    return o, kv_cmp, kv_sel, kv_win[:, T - min(WINDOW, T):]


def nsa_sample(hb, q2, B, S, pool_cmp, pool_sel, win_buf, page_table, p, rel_table):
    W = 2 * KV_WIDTH
    n_pages = page_table.shape[1]
    kvs, kvt, gl = nsa_project(hb, p)
    assert (n_pages * PAGE_SIZE + S) // BLOCK == n_pages * PAGE_SIZE // BLOCK
    kc, vc = compress_paged(pool_cmp.reshape(-1, HEAD_DIM), page_table, p, tall=True)
    lw = win_buf.shape[1]
    o = nsa_sample_attention(q2, kc, vc, pool_sel.reshape(-1, HEAD_DIM), page_table,
                             win_buf.reshape(-1, HEAD_DIM), kvs[1].reshape(B, S, W), kvs[2].reshape(B, S, W),
                             gl, rel_table)
    kv_cmp, kv_sel, kv_win = [kv.reshape(B, S, 2, KV_GROUPS, HEAD_DIM) for kv in kvt]
    all_win = jnp.concatenate([win_buf, kv_win], axis=1)
    return o, kv_cmp, kv_sel, all_win[:, lw + S - min(WINDOW, lw + S):]


def ffn(xp, xs, g, w_up32, w_down32, layer):
    D_ff = w_up32.shape[2]
    hid_p, w_up = matmul_wcast(rmsnorm(xp, g, BF16), w_up32, layer, D_ff, act="relu2", out_dtype=BF16)
    xp, w_down = matmul_wcast_ktiled(hid_p, w_down32, layer, xp)
    hid_s = matmul(rmsnorm(xs, g, BF16), w_up, act="relu2", out_dtype=BF16)
    return xp, matmul(hid_s, w_down, res=xs, tk=2048)


def kernel(x_prompt, x_sample, state_conv, cache_kv_cmp, cache_kv_sel, state_kv_win, page_table, norm_mix_g, norm_ffn_g, norm_final_g, rel_bias_table, conv_w_in, conv_b_in, conv_dw_w, conv_dw_b, conv_ln_g, conv_ln_b, conv_w_out, conv_b_out, nsa_w_in, nsa_cmp_pos, nsa_cmp_w1, nsa_cmp_b1, nsa_cmp_w2, nsa_cmp_b2, nsa_w_out, ffn_w_up, ffn_w_down):
    Bp, T, D = x_prompt.shape
    Bs, S, _ = x_sample.shape
    xp, xs = x_prompt.reshape(Bp * T, D), x_sample.reshape(Bs * S, D)
    qd = N_HEADS * HEAD_DIM
    kvd = 2 * KV_WIDTH

    def conv_tail(glu, B, L, buf):
        return conv_ln_silu(glu.reshape(B, L, D), buf, conv_dw_w[0], conv_dw_b[0], conv_ln_g[0], conv_ln_b[0])

    glu_p, wa, wg = matmul_glu_wcast(rmsnorm(xp, norm_mix_g[0], BF16), conv_w_in, 0, conv_b_in[0])
    zp, conv_p = conv_tail(glu_p, Bp, T, jnp.zeros((Bp, CONV_BUF, D), F32))
    xp, w_out0 = matmul_wcast(zp, conv_w_out, 0, D, bias=conv_b_out[0], res=xp)
    glu_s = matmul_glu(rmsnorm(xs, norm_mix_g[0], BF16), wa, wg, conv_b_in[0][:D], conv_b_in[0][D:])
    zs, conv_s = conv_tail(glu_s, Bs, S, state_conv[0])
    xs = matmul(zs, w_out0, bias=conv_b_out[0], res=xs)
    xp, xs = ffn(xp, xs, norm_ffn_g[0], ffn_w_up, ffn_w_down, 0)

    w_in = nsa_w_in[0]
    nbr = N_BRANCHES * HEADS_PER_GROUP
    gate_w = w_in[:, qd + N_BRANCHES * kvd:].reshape(D, N_BRANCHES, KV_GROUPS, HEADS_PER_GROUP)
    gate_w = gate_w.transpose(0, 2, 1, 3).reshape(D, KV_GROUPS, nbr)
    gate_w = jnp.pad(gate_w, ((0, 0), (0, 0), (0, 128 - nbr))).reshape(D, KV_GROUPS * 128)
    npar = dict(w_kv=[w_in[:, qd + i * kvd: qd + (i + 1) * kvd].astype(BF16) for i in range(N_BRANCHES)],
                w_gate=gate_w.astype(BF16), pos=nsa_cmp_pos[0], w1=nsa_cmp_w1[0].astype(BF16),
                b1=nsa_cmp_b1[0], w2=nsa_cmp_w2[0].astype(BF16), b2=nsa_cmp_b2[0])
    hp = rmsnorm(xp, norm_mix_g[1], BF16)
    q2p, w_q = matmul_wcast(hp, nsa_w_in, 0, qd, out_dtype=BF16)
    op, kcp, ksp, kwp = nsa_prompt(hp, q2p, Bp, T, npar, rel_bias_table)
    xp, w_out1 = matmul_wcast(op, nsa_w_out, 0, D, res=xp)
    hs = rmsnorm(xs, norm_mix_g[1], BF16)
    os_, kcs, kss, kws = nsa_sample(hs, matmul(hs, w_q), Bs, S, cache_kv_cmp[0], cache_kv_sel[0],
                                    state_kv_win[0], page_table, npar, rel_bias_table)
    xs = matmul(os_.astype(BF16), w_out1, res=xs)
    xp, xs = ffn(xp, xs, norm_ffn_g[1], ffn_w_up, ffn_w_down, 1)

    y_prompt = rmsnorm(xp, norm_final_g, F32).reshape(Bp, T, D)
    y_sample = rmsnorm(xs, norm_final_g, F32).reshape(Bs, S, D)
    return (y_prompt, y_sample, conv_p[None], conv_s[None], kcp[None], kcs[None],
            ksp[None], kss[None], kwp[None], kws[None])
```
